```python
import jax, jax.numpy as jnp
from jax import lax
import numpy as np

D_MODEL = 1024
BATCH = 8
SEQ = 8192
DEPTH = 1

D_MIX = D_MODEL
D_CONV = D_MIX // 2
D_MLSTM = D_MIX - D_CONV
N_HEADS = 4
HEAD_DIM = D_MLSTM // N_HEADS
CONV_WIDTH = 31
QK_CONV_WIDTH = 4
CHUNK = 128
N_GROUPS = 4
EXPERTS_PER_GROUP = 8
N_EXPERTS = N_GROUPS * EXPERTS_PER_GROUP
TOP_K = 2
D_EXPERT = D_MODEL // 2
MOE_BLOCK = 256
EPS = 1e-6
F_BIAS_LO = 3.0
F_BIAS_HI = 6.0
D_IN = 2 * D_CONV + 4 * D_MLSTM + 2 * N_HEADS

kernel_name = "hymba_conv_mlstm_hier_moe"


def rmsnorm(x, g):
    xf = x.astype(jnp.float32)
    y = xf * lax.rsqrt(jnp.mean(xf * xf, axis=-1, keepdims=True) + EPS)
    return (y * g.astype(jnp.float32)).astype(x.dtype)


def layernorm(x, g, b):
    xf = x.astype(jnp.float32)
    mu = jnp.mean(xf, axis=-1, keepdims=True)
    var = jnp.mean(jnp.square(xf - mu), axis=-1, keepdims=True)
    y = (xf - mu) * lax.rsqrt(var + EPS)
    return (y * g.astype(jnp.float32) + b.astype(jnp.float32)).astype(x.dtype)


def causal_depthwise_conv(x, w, b):
    width, ch = w.shape
    y = lax.conv_general_dilated(
        x, w[:, None, :].astype(x.dtype), window_strides=(1,), padding=[(width - 1, 0)],
        dimension_numbers=("NWC", "WIO", "NWC"), feature_group_count=ch)
    return y + b.astype(x.dtype)


def mlstm_chunkwise(q, k, v, i_pre, f_pre):
    B, H, S, Dh = q.shape
    nc = S // CHUNK
    k = k * (Dh ** -0.5)
    log_f = jax.nn.log_sigmoid(f_pre)

    def to_chunks(a):
        a = a.reshape(a.shape[:2] + (nc, CHUNK) + a.shape[3:])
        return jnp.moveaxis(a, 2, 0)

    qc, kc, vc = to_chunks(q), to_chunks(k), to_chunks(v)
    ic, fc = to_chunks(i_pre), to_chunks(log_f)
    causal = jnp.tril(jnp.ones((CHUNK, CHUNK), dtype=bool))

    def step(carry, inp):
        C, n, m = carry
        qb, kb, vb, ib, fb = inp
        b = jnp.cumsum(fb, axis=-1)
        d = b[..., :, None] - b[..., None, :] + ib[..., None, :]
        d = jnp.where(causal, d, -jnp.inf)
        g = b + m[..., None]
        m_t = jnp.maximum(g, jnp.max(d, axis=-1))
        w_intra = jnp.exp(d - m_t[..., None])
        w_inter = jnp.exp(g - m_t)
        s = jnp.einsum("bhtd,bhsd->bhts", qb, kb) * w_intra
        num = (jnp.einsum("bhts,bhsd->bhtd", s, vb)
               + w_inter[..., None] * jnp.einsum("bhvk,bhtk->bhtv", C, qb))
        den = jnp.sum(s, axis=-1) + w_inter * jnp.einsum("bhk,bhtk->bht", n, qb)
        h = num / jnp.maximum(jnp.abs(den), jnp.exp(-m_t))[..., None]
        b_last = b[..., -1]
        log_w = b_last[..., None] - b + ib
        m_new = jnp.maximum(b_last + m, jnp.max(log_w, axis=-1))
        decay = jnp.exp(b_last + m - m_new)
        w_state = jnp.exp(log_w - m_new[..., None])
        kw = kb * w_state[..., None]
        C_new = decay[..., None, None] * C + jnp.einsum("bhsv,bhsk->bhvk", vb, kw)
        n_new = decay[..., None] * n + jnp.sum(kw, axis=2)
        return (C_new, n_new, m_new), h

    init = (jnp.zeros((B, H, Dh, Dh), jnp.float32),
            jnp.zeros((B, H, Dh), jnp.float32),
            jnp.zeros((B, H), jnp.float32))
    _, hc = lax.scan(step, init, (qc, kc, vc, ic, fc))
    return jnp.moveaxis(hc, 0, 2).reshape(B, H, S, Dh)


def hybrid_mixer(h, w_in, b_in, conv_w, conv_b, conv_ln_g, conv_ln_b,
                 qk_conv_w, qk_conv_b, mlstm_norm_g, w_out):
    B, S, _ = h.shape
    z = jnp.einsum("bsd,de->bse", h, w_in) + b_in
    splits = np.cumsum([D_CONV, D_CONV, D_MLSTM, D_MLSTM, D_MLSTM, D_MLSTM, N_HEADS])
    conv_a, conv_gate, q, k, v, o, i_pre, f_pre = jnp.split(z, [int(i) for i in splits], axis=-1)

    u = conv_a * jax.nn.sigmoid(conv_gate)
    u = causal_depthwise_conv(u, conv_w, conv_b)
    u = jax.nn.silu(layernorm(u, conv_ln_g, conv_ln_b))

    qk = jax.nn.silu(causal_depthwise_conv(jnp.concatenate([q, k], axis=-1), qk_conv_w, qk_conv_b))
    q, k = qk[..., :D_MLSTM], qk[..., D_MLSTM:]

    def heads(a):
        return jnp.transpose(a.reshape(B, S, N_HEADS, HEAD_DIM), (0, 2, 1, 3)).astype(jnp.float32)

    ht = mlstm_chunkwise(heads(q), heads(k), heads(v),
                         jnp.transpose(i_pre, (0, 2, 1)).astype(jnp.float32),
                         jnp.transpose(f_pre, (0, 2, 1)).astype(jnp.float32))
    ht = jnp.transpose(ht, (0, 2, 1, 3))
    hm = jax.nn.sigmoid(o.astype(jnp.float32)).reshape(B, S, N_HEADS, HEAD_DIM) * ht
    mu = jnp.mean(hm, axis=-1, keepdims=True)
    var = jnp.mean(jnp.square(hm - mu), axis=-1, keepdims=True)
    hm = (hm - mu) * lax.rsqrt(var + EPS) * mlstm_norm_g.astype(jnp.float32).reshape(N_HEADS, HEAD_DIM)
    hm = hm.reshape(B, S, D_MLSTM).astype(h.dtype)

    y = jnp.concatenate([u, hm], axis=-1)
    return jnp.einsum("bse,ed->bsd", y, w_out)


def hierarchical_moe(h, rg_w, rg_b, re_w, re_b, w_gate, w_up, w_down):
    B, S, D = h.shape
    T = B * S
    xt = h.reshape(T, D)
    group_logits = (xt @ rg_w + rg_b).astype(jnp.float32)
    group_prob = jax.nn.softmax(group_logits, axis=-1)
    grp = jnp.argmax(group_logits, axis=-1).astype(jnp.int32)
    p_grp = jnp.take_along_axis(group_prob, grp[:, None], axis=-1)
    expert_logits = (xt @ re_w + re_b).astype(jnp.float32).reshape(T, N_GROUPS, EXPERTS_PER_GROUP)
    in_group = expert_logits[jnp.arange(T), grp]
    top_val, top_idx = lax.top_k(in_group, TOP_K)
    gate = p_grp * jax.nn.softmax(top_val, axis=-1)
    expert_id = grp[:, None] * EXPERTS_PER_GROUP + top_idx.astype(jnp.int32)

    n_assign = T * TOP_K
    e_flat = expert_id.reshape(-1)
    tok_flat = jnp.repeat(jnp.arange(T, dtype=jnp.int32), TOP_K)
    g_flat = gate.reshape(-1)
    order = jnp.argsort(e_flat)
    e_sorted, tok_sorted, g_sorted = e_flat[order], tok_flat[order], g_flat[order]
    counts = jnp.zeros((N_EXPERTS,), jnp.int32).at[e_flat].add(1)
    starts = jnp.cumsum(counts) - counts
    padded = (counts + MOE_BLOCK - 1) // MOE_BLOCK * MOE_BLOCK
    padded_ends = jnp.cumsum(padded)
    padded_starts = padded_ends - padded
    dest = padded_starts[e_sorted] + (jnp.arange(n_assign, dtype=jnp.int32) - starts[e_sorted])
    n_blocks = -(-n_assign // MOE_BLOCK) + N_EXPERTS
    n_slots = n_blocks * MOE_BLOCK
    slot_tok = jnp.zeros((n_slots,), jnp.int32).at[dest].set(tok_sorted)
    slot_gate = jnp.zeros((n_slots,), jnp.float32).at[dest].set(g_sorted)
    block_start = jnp.arange(n_blocks, dtype=jnp.int32) * MOE_BLOCK
    block_expert = jnp.minimum(jnp.sum(block_start[:, None] >= padded_ends[None, :], axis=1),
                               N_EXPERTS - 1).astype(jnp.int32)

    xs = xt[slot_tok].reshape(n_blocks, MOE_BLOCK, D)

    def expert_block(args):
        xb, e = args
        a = xb @ w_gate[e]
        u = xb @ w_up[e]
        return (jax.nn.silu(a) * u) @ w_down[e]

    ys = lax.map(expert_block, (xs, block_expert)).reshape(n_slots, D)
    ys = ys * slot_gate[:, None].astype(ys.dtype)
    out = jnp.zeros((T, D), ys.dtype).at[slot_tok].add(ys)
    return out.reshape(B, S, D).astype(h.dtype)


def setup_inputs(seed: int = 0) -> dict:
    key = jax.random.key(seed)
    ks = jax.random.split(key, 21)
    f32 = jnp.float32
    L = DEPTH

    def nrm(k, shape, scale):
        return jax.random.normal(k, shape, f32) * scale

    b_in = nrm(ks[3], (L, D_IN), 0.02)
    b_in = b_in.at[:, D_IN - N_HEADS:].add(jnp.linspace(F_BIAS_LO, F_BIAS_HI, N_HEADS, dtype=f32))
    return {
        "x": nrm(ks[0], (BATCH, SEQ, D_MODEL), 1.0),
        "norm_mix_g": 1.0 + nrm(ks[1], (L, D_MODEL), 0.02),
        "w_in": nrm(ks[2], (L, D_MODEL, D_IN), D_MODEL ** -0.5),
        "b_in": b_in,
        "conv_w": nrm(ks[4], (L, CONV_WIDTH, D_CONV), CONV_WIDTH ** -0.5),
        "conv_b": nrm(ks[5], (L, D_CONV), 0.02),
        "conv_ln_g": 1.0 + nrm(ks[6], (L, D_CONV), 0.02),
        "conv_ln_b": nrm(ks[7], (L, D_CONV), 0.02),
        "qk_conv_w": nrm(ks[8], (L, QK_CONV_WIDTH, 2 * D_MLSTM), QK_CONV_WIDTH ** -0.5),
        "qk_conv_b": nrm(ks[9], (L, 2 * D_MLSTM), 0.02),
        "mlstm_norm_g": 1.0 + nrm(ks[10], (L, D_MLSTM), 0.02),
        "w_out": nrm(ks[11], (L, D_MIX, D_MODEL), D_MIX ** -0.5),
        "norm_ffn_g": 1.0 + nrm(ks[12], (L, D_MODEL), 0.02),
        "router_group_w": nrm(ks[13], (L, D_MODEL, N_GROUPS), D_MODEL ** -0.5),
        "router_group_b": nrm(ks[14], (L, N_GROUPS), 0.01),
        "router_expert_w": nrm(ks[15], (L, D_MODEL, N_EXPERTS), D_MODEL ** -0.5),
        "router_expert_b": nrm(ks[16], (L, N_EXPERTS), 0.01),
        "expert_w_gate": nrm(ks[17], (L, N_EXPERTS, D_MODEL, D_EXPERT), D_MODEL ** -0.5),
        "expert_w_up": nrm(ks[18], (L, N_EXPERTS, D_MODEL, D_EXPERT), D_MODEL ** -0.5),
        "expert_w_down": nrm(ks[19], (L, N_EXPERTS, D_EXPERT, D_MODEL), D_EXPERT ** -0.5),
        "final_norm_g": 1.0 + nrm(ks[20], (D_MODEL,), 0.02),
    }


def reference(x, norm_mix_g, w_in, b_in, conv_w, conv_b, conv_ln_g, conv_ln_b,
              qk_conv_w, qk_conv_b, mlstm_norm_g, w_out, norm_ffn_g,
              router_group_w, router_group_b, router_expert_w, router_expert_b,
              expert_w_gate, expert_w_up, expert_w_down, final_norm_g):
    for l in range(DEPTH):
        h = rmsnorm(x, norm_mix_g[l])
        x = x + hybrid_mixer(h, w_in[l], b_in[l], conv_w[l], conv_b[l], conv_ln_g[l], conv_ln_b[l],
                             qk_conv_w[l], qk_conv_b[l], mlstm_norm_g[l], w_out[l])
        h = rmsnorm(x, norm_ffn_g[l])
        x = x + hierarchical_moe(h, router_group_w[l], router_group_b[l],
                                 router_expert_w[l], router_expert_b[l],
                                 expert_w_gate[l], expert_w_up[l], expert_w_down[l])
    return rmsnorm(x, final_norm_g)
```

```python
import functools

import jax
import jax.numpy as jnp
from jax import lax
from jax.experimental import pallas as pl
from jax.experimental.pallas import tpu as pltpu

EPS = 1e-6
LANES = 128
SUBLANES = 8
CHUNK = 128
TOP_K = 2
SEQ_TILE = 512
CONV_ROWS = 32
CONV_HALO = 32
QK_HALO = 8
EXPERT_BLOCK = 256
DISPATCH_TILE = 2048
DMA_BATCH = 256
COMBINE_TILE = 256
VMEM_LIMIT = 56 * 1024 * 1024

_NT = (((1,), (1,)), ((), ()))
_TN = (((0,), (0,)), ((), ()))


def _dot(a, b):
    return jnp.dot(a, b, preferred_element_type=jnp.float32)


def _silu(x):
    return x * jax.nn.sigmoid(x)


def _log_sigmoid(x):
    return jnp.minimum(x, 0.0) - jnp.log1p(jnp.exp(-jnp.abs(x)))


def _split_bf16(x, parts):
    out = []
    for _ in range(parts):
        p = x.astype(jnp.bfloat16)
        out.append(p)
        x = x - p.astype(jnp.float32)
    return out


def _mixer_kernel(x_ref, gmix_ref, wcv_ref, bcv_ref, wqk_ref, bqk_ref, wvo_ref, bvo_ref,
                  wif_ref, bif_ref, wift_ref, bift_ref, cw_ref, cb_ref, lng_ref, lnb_ref,
                  qkw_ref, qkb_ref, mng_ref, wout_ref, gffn_ref, wrh_ref, wrl_ref, br_ref,
                  su_ref, tril_ref,
                  x1_ref, h2_ref, ri_ref, gc_ref, cnt_ref,
                  uext, qkext, q_s, k_s, v_s, o_s, cn_s, m_s, y_s, cnt_s,
                  *, n_heads, conv_width, qk_width, n_groups, epg):
    ts, d = x_ref.shape
    dc = cw_ref.shape[1]
    dm = mng_ref.shape[1]
    dh = dm // n_heads
    n_chunks = ts // CHUNK
    n_experts = n_groups * epg
    b_idx = pl.program_id(0)
    t_idx = pl.program_id(1)

    @pl.when(t_idx == 0)
    def _reset_sequence_state():
        uext[0:CONV_HALO, :] = jnp.zeros((CONV_HALO, dc), jnp.float32)
        qkext[0:QK_HALO, :] = jnp.zeros((QK_HALO, 2 * dm), jnp.float32)
        cn_s[...] = jnp.zeros(cn_s.shape, jnp.float32)
        m_s[...] = jnp.zeros(m_s.shape, jnp.float32)

    @pl.when((t_idx == 0) & (b_idx == 0))
    def _reset_counts():
        cnt_s[...] = jnp.zeros(cnt_s.shape, jnp.float32)

    x = x_ref[...]
    h = x * lax.rsqrt(jnp.mean(x * x, axis=-1, keepdims=True) + EPS) * gmix_ref[...]
    hb = h.astype(jnp.bfloat16)

    zc = _dot(hb, wcv_ref[...]) + bcv_ref[...]
    uext[CONV_HALO:CONV_HALO + ts, :] = zc[:, :dc] * jax.nn.sigmoid(zc[:, dc:])
    first = CONV_HALO - (conv_width - 1)
    for rb in range(ts // CONV_ROWS):
        r0 = rb * CONV_ROWS
        acc = jnp.zeros((CONV_ROWS, dc), jnp.float32) + cb_ref[...]
        for j in range(conv_width):
            acc = acc + cw_ref[j:j + 1, :] * uext[r0 + first + j:r0 + first + j + CONV_ROWS, :]
        mu = jnp.mean(acc, axis=-1, keepdims=True)
        cen = acc - mu
        var = jnp.mean(cen * cen, axis=-1, keepdims=True)
        yn = cen * lax.rsqrt(var + EPS) * lng_ref[...] + lnb_ref[...]
        y_s[r0:r0 + CONV_ROWS, 0:dc] = _silu(yn).astype(jnp.bfloat16)
    uext[0:CONV_HALO, :] = uext[ts:ts + CONV_HALO, :]

    qkext[QK_HALO:QK_HALO + ts, :] = _dot(hb, wqk_ref[...]) + bqk_ref[...]
    qfirst = QK_HALO - (qk_width - 1)
    k_scale = dh ** -0.5
    for rb in range(ts // CONV_ROWS):
        r0 = rb * CONV_ROWS
        acc = jnp.zeros((CONV_ROWS, 2 * dm), jnp.float32) + qkb_ref[...]
        for j in range(qk_width):
            acc = acc + qkw_ref[j:j + 1, :] * qkext[r0 + qfirst + j:r0 + qfirst + j + CONV_ROWS, :]
        act = _silu(acc)
        q_s[r0:r0 + CONV_ROWS, :] = act[:, :dm].astype(jnp.bfloat16)
        k_s[r0:r0 + CONV_ROWS, :] = (act[:, dm:] * k_scale).astype(jnp.bfloat16)
    qkext[0:QK_HALO, :] = qkext[ts:ts + QK_HALO, :]

    zvo = _dot(hb, wvo_ref[...]) + bvo_ref[...]
    v_s[...] = zvo[:, :dm].astype(jnp.bfloat16)
    o_s[...] = jax.nn.sigmoid(zvo[:, dm:])

    zif_col = _dot(hb, wif_ref[...]) + bif_ref[...]
    zif_row = lax.dot_general(wift_ref[...], hb, _NT,
                              preferred_element_type=jnp.float32) + bift_ref[...]
    logf_col = _log_sigmoid(zif_col)
    logf_row = _log_sigmoid(zif_row)
    lane = lax.broadcasted_iota(jnp.int32, (SUBLANES, CHUNK), 1)
    causal = (lax.broadcasted_iota(jnp.int32, (CHUNK, CHUNK), 0)
              >= lax.broadcasted_iota(jnp.int32, (CHUNK, CHUNK), 1))
    ones_blk = jnp.ones((CHUNK, dh), jnp.bfloat16)
    tril = tril_ref[...]

    for c in range(n_chunks):
        r0 = c * CHUNK
        b_row = logf_row[:, r0:r0 + CHUNK]
        shift = 1
        while shift < CHUNK:
            b_row = b_row + jnp.where(lane >= shift, pltpu.roll(b_row, shift, axis=1), 0.0)
            shift *= 2
        b_col = sum(_dot(tril, p) for p in _split_bf16(logf_col[r0:r0 + CHUNK, :], 3))
        i_col_all = zif_col[r0:r0 + CHUNK, :]
        i_row_all = zif_row[:, r0:r0 + CHUNK]
        for hd in range(n_heads):
            c0 = hd * dh
            b_c = b_col[:, n_heads + hd:n_heads + hd + 1]
            i_c = i_col_all[:, hd:hd + 1]
            b_r = b_row[n_heads + hd:n_heads + hd + 1, :]
            i_r = i_row_all[hd:hd + 1, :]
            m_prev = m_s[hd, 0:1, 0:1]
            q = q_s[r0:r0 + CHUNK, c0:c0 + dh]
            k = k_s[r0:r0 + CHUNK, c0:c0 + dh]
            vaug = jnp.concatenate([v_s[r0:r0 + CHUNK, c0:c0 + dh], ones_blk], axis=-1)
            cn = cn_s[hd]

            dmat = jnp.where(causal, b_c + (i_r - b_r), -jnp.inf)
            g = b_c + m_prev
            m_t = jnp.maximum(g, jnp.max(dmat, axis=-1, keepdims=True))
            w_intra = jnp.exp(dmat - m_t)
            w_inter = jnp.exp(g - m_t)
            s = lax.dot_general(q, k, _NT, preferred_element_type=jnp.float32) * w_intra
            nd = _dot(s.astype(jnp.bfloat16), vaug) + w_inter * _dot(q, cn.astype(jnp.bfloat16))
            hval = nd[:, :dh] / jnp.maximum(jnp.abs(nd[:, dh:]), jnp.exp(-m_t))

            b_last = b_r[:, CHUNK - 1:CHUNK]
            logw_c = b_last - b_c + i_c
            m_new = jnp.maximum(b_last + m_prev, jnp.max(logw_c, axis=0, keepdims=True))
            decay = jnp.exp(b_last + m_prev - m_new)
            kw = (k.astype(jnp.float32) * jnp.exp(logw_c - m_new)).astype(jnp.bfloat16)
            cn_s[hd] = decay * cn + lax.dot_general(kw, vaug, _TN,
                                                    preferred_element_type=jnp.float32)
            m_s[hd] = jnp.broadcast_to(m_new, m_s.shape[1:])

            hm = o_s[r0:r0 + CHUNK, c0:c0 + dh] * hval
            mu = jnp.mean(hm, axis=-1, keepdims=True)
            cen = hm - mu
            var = jnp.mean(cen * cen, axis=-1, keepdims=True)
            hn = cen * lax.rsqrt(var + EPS) * mng_ref[:, c0:c0 + dh]
            y_s[r0:r0 + CHUNK, dc + c0:dc + c0 + dh] = hn.astype(jnp.bfloat16)

    x1 = x + _dot(y_s[...], wout_ref[...])
    x1_ref[...] = x1
    h2 = x1 * lax.rsqrt(jnp.mean(x1 * x1, axis=-1, keepdims=True) + EPS) * gffn_ref[...]
    for sblk in range(d // LANES):
        h2_ref[:, sblk, :] = h2[:, sblk * LANES:(sblk + 1) * LANES]

    h2_hi, h2_lo = _split_bf16(h2, 2)
    lg = (lax.dot_general(wrh_ref[...], h2_hi, _NT, preferred_element_type=jnp.float32)
          + lax.dot_general(wrl_ref[...], h2_hi, _NT, preferred_element_type=jnp.float32)
          + lax.dot_general(wrh_ref[...], h2_lo, _NT, preferred_element_type=jnp.float32)
          + br_ref[...])
    gl = lg[0:n_groups, :]
    gidx = lax.broadcasted_iota(jnp.int32, (n_groups, ts), 0)
    gmax = jnp.max(gl, axis=0, keepdims=True)
    grp = jnp.min(jnp.where(gl == gmax, gidx, n_groups), axis=0, keepdims=True)
    p_grp = 1.0 / jnp.sum(jnp.exp(gl - gmax), axis=0, keepdims=True)
    in_group = jnp.zeros((epg, ts), jnp.float32)
    for gi in range(n_groups):
        in_group = jnp.where(grp == gi, lg[SUBLANES + gi * epg:SUBLANES + (gi + 1) * epg, :], in_group)
    eidx = lax.broadcasted_iota(jnp.int32, (epg, ts), 0)
    v1 = jnp.max(in_group, axis=0, keepdims=True)
    i1 = jnp.min(jnp.where(in_group == v1, eidx, epg), axis=0, keepdims=True)
    rest = jnp.where(eidx == i1, -jnp.inf, in_group)
    v2 = jnp.max(rest, axis=0, keepdims=True)
    i2 = jnp.min(jnp.where(rest == v2, eidx, epg), axis=0, keepdims=True)
    e21 = jnp.exp(v2 - v1)
    g1 = p_grp / (1.0 + e21)
    g2 = g1 * e21
    e1 = grp * epg + i1
    e2 = grp * epg + i2

    xidx = lax.broadcasted_iota(jnp.int32, (n_experts, ts), 0)
    sel1 = xidx == e1
    sel2 = xidx == e2
    onehot = jnp.where(sel1 | sel2, 1.0, 0.0).astype(jnp.bfloat16)
    before = cnt_s[:, 0:1] + _dot(onehot, su_ref[...])
    rank1 = jnp.sum(jnp.where(sel1, before, 0.0), axis=0, keepdims=True)
    rank2 = jnp.sum(jnp.where(sel2, before, 0.0), axis=0, keepdims=True)
    cnt_new = cnt_s[...] + jnp.sum(onehot.astype(jnp.float32), axis=1, keepdims=True)
    cnt_s[...] = cnt_new
    cnt_ref[...] = cnt_new

    zero_i = jnp.zeros((SUBLANES - 4, ts), jnp.int32)
    ri_ref[...] = jnp.concatenate(
        [e1, e2, rank1.astype(jnp.int32), rank2.astype(jnp.int32), zero_i], axis=0)
    gates_row = jnp.concatenate([g1, g2, jnp.zeros((LANES - 2, ts), jnp.float32)], axis=0)
    gc_ref[...] = gates_row.T


def _mixer_call(x, p, *, n_heads, conv_width, qk_width, n_groups, epg):
    bsz, seq, d = x.shape
    ts = min(SEQ_TILE, seq)
    nt = seq // ts
    dc = p["cw"].shape[1]
    dm = p["mng"].shape[1]
    dh = dm // n_heads
    n_experts = n_groups * epg
    assert seq % ts == 0 and ts % CHUNK == 0 and dh == LANES and d % LANES == 0
    assert epg == SUBLANES and n_groups <= SUBLANES and conv_width - 1 <= CONV_HALO

    consts = [p[n] for n in ("gmix", "wcv", "bcv", "wqk", "bqk", "wvo", "bvo", "wif", "bif",
                             "wift", "bift", "cw", "cb", "lng", "lnb", "qkw", "qkb", "mng",
                             "wout", "gffn", "wrh", "wrl", "br", "su", "tril")]

    def const_spec(a):
        return pl.BlockSpec(a.shape, lambda b, t: (0,) * a.ndim)

    kern = functools.partial(_mixer_kernel, n_heads=n_heads, conv_width=conv_width,
                             qk_width=qk_width, n_groups=n_groups, epg=epg)
    return pl.pallas_call(
        kern,
        grid=(bsz, nt),
        in_specs=[pl.BlockSpec((None, ts, d), lambda b, t: (b, t, 0))] + [const_spec(a) for a in consts],
        out_specs=[
            pl.BlockSpec((None, ts, d), lambda b, t: (b, t, 0)),
            pl.BlockSpec((None, ts, d // LANES, LANES), lambda b, t: (b, t, 0, 0)),
            pl.BlockSpec((None, None, SUBLANES, ts), lambda b, t: (b, t, 0, 0)),
            pl.BlockSpec((None, ts, LANES), lambda b, t: (b, t, 0)),
            pl.BlockSpec((n_experts, LANES), lambda b, t: (0, 0)),
        ],
        out_shape=[
            jax.ShapeDtypeStruct((bsz, seq, d), jnp.float32),
            jax.ShapeDtypeStruct((bsz, seq, d // LANES, LANES), jnp.float32),
            jax.ShapeDtypeStruct((bsz, nt, SUBLANES, ts), jnp.int32),
            jax.ShapeDtypeStruct((bsz, seq, LANES), jnp.float32),
            jax.ShapeDtypeStruct((n_experts, LANES), jnp.float32),
        ],
        scratch_shapes=[
            pltpu.VMEM((CONV_HALO + ts, dc), jnp.float32),
            pltpu.VMEM((QK_HALO + ts, 2 * dm), jnp.float32),
            pltpu.VMEM((ts, dm), jnp.bfloat16),
            pltpu.VMEM((ts, dm), jnp.bfloat16),
            pltpu.VMEM((ts, dm), jnp.bfloat16),
            pltpu.VMEM((ts, dm), jnp.float32),
            pltpu.VMEM((n_heads, dh, 2 * dh), jnp.float32),
            pltpu.VMEM((n_heads, SUBLANES, LANES), jnp.float32),
            pltpu.VMEM((ts, d), jnp.bfloat16),
            pltpu.VMEM((n_experts, LANES), jnp.float32),
        ],
        compiler_params=pltpu.CompilerParams(
            dimension_semantics=("arbitrary", "arbitrary"), vmem_limit_bytes=VMEM_LIMIT),
        name="mixer_router",
    )(x, *consts)


def _dispatch_kernel(pad_ref, dest_ref, h2_hbm, xs_hbm, zrow, zsem, sem):
    n = dest_ref.shape[-1]
    base_tok = pl.program_id(0) * (n // TOP_K)
    n_seg = pad_ref.shape[0] // 2

    @pl.when(pl.program_id(0) == 0)
    def _zero_padding_rows():
        zrow[...] = jnp.zeros(zrow.shape, zrow.dtype)

        def seg(e, carry):
            def zcopy(r):
                return pltpu.make_async_copy(zrow, xs_hbm.at[pad_ref[e] + r], zsem)

            def zissue(r, c):
                zcopy(r).start()
                return c

            def zdrain(r, c):
                zcopy(r).wait()
                return c

            lax.fori_loop(0, pad_ref[n_seg + e], zissue, 0)
            lax.fori_loop(0, pad_ref[n_seg + e], zdrain, 0)
            return carry

        lax.fori_loop(0, n_seg, seg, 0)

    def copy(a):
        tok = base_tok + a // TOP_K
        return pltpu.make_async_copy(h2_hbm.at[tok], xs_hbm.at[dest_ref[0, 0, a]], sem)

    def round_(r, carry):
        def issue(a, c):
            copy(r * DMA_BATCH + a).start()
            return c

        def drain(a, c):
            copy(r * DMA_BATCH + a).wait()
            return c

        lax.fori_loop(0, DMA_BATCH, issue, 0)
        lax.fori_loop(0, DMA_BATCH, drain, 0)
        return carry

    lax.fori_loop(0, n // DMA_BATCH, round_, 0)


def _dispatch_call(pad_runs, dest, h2_rows, n_slots):
    n_tok, s, l = h2_rows.shape
    n_assign = dest.shape[0]
    per_step = min(DISPATCH_TILE * TOP_K, n_assign)
    assert n_assign % per_step == 0 and per_step % DMA_BATCH == 0
    steps = n_assign // per_step
    return pl.pallas_call(
        _dispatch_kernel,
        grid_spec=pltpu.PrefetchScalarGridSpec(
            num_scalar_prefetch=1,
            grid=(steps,),
            in_specs=[
                pl.BlockSpec((1, 1, per_step), lambda i, pad: (i, 0, 0), memory_space=pltpu.SMEM),
                pl.BlockSpec(memory_space=pl.ANY),
            ],
            out_specs=pl.BlockSpec(memory_space=pl.ANY),
            scratch_shapes=[pltpu.VMEM((s, l), h2_rows.dtype), pltpu.SemaphoreType.DMA(()),
                            pltpu.SemaphoreType.DMA(())],
        ),
        out_shape=jax.ShapeDtypeStruct((n_slots, s, l), h2_rows.dtype),
        compiler_params=pltpu.CompilerParams(dimension_semantics=("arbitrary",)),
        name="dispatch",
    )(pad_runs, dest.reshape(steps, 1, per_step), h2_rows)


def _expert_kernel(be_ref, nused_ref, xs_ref, wg_ref, wu_ref, wd_ref, ys_ref):
    i = pl.program_id(0)
    blk, s, l = xs_ref.shape

    @pl.when(i < nused_ref[0])
    def _compute():
        xb = jnp.concatenate([xs_ref[:, j, :].astype(jnp.bfloat16) for j in range(s)], axis=-1)
        a = _dot(xb, wg_ref[...])
        u = _dot(xb, wu_ref[...])
        y = _dot((_silu(a) * u).astype(jnp.bfloat16), wd_ref[...])
        for j in range(s):
            ys_ref[:, j, :] = y[:, j * l:(j + 1) * l]

    @pl.when(i >= nused_ref[0])
    def _unused_block():
        ys_ref[...] = jnp.zeros(ys_ref.shape, ys_ref.dtype)


def _expert_call(block_expert, n_used, xs, wg, wu, wd):
    n_slots, s, l = xs.shape
    n_blocks = n_slots // EXPERT_BLOCK
    _, d, de = wg.shape

    def row_map(i, be, nu):
        return (i, 0, 0)

    def w_map(i, be, nu):
        return (be[i], 0, 0)

    return pl.pallas_call(
        _expert_kernel,
        grid_spec=pltpu.PrefetchScalarGridSpec(
            num_scalar_prefetch=2,
            grid=(n_blocks,),
            in_specs=[
                pl.BlockSpec((EXPERT_BLOCK, s, l), row_map),
                pl.BlockSpec((None, d, de), w_map),
                pl.BlockSpec((None, d, de), w_map),
                pl.BlockSpec((None, de, d), w_map),
            ],
            out_specs=pl.BlockSpec((EXPERT_BLOCK, s, l), row_map),
        ),
        out_shape=jax.ShapeDtypeStruct((n_slots, s, l), jnp.float32),
        compiler_params=pltpu.CompilerParams(
            dimension_semantics=("arbitrary",), vmem_limit_bytes=VMEM_LIMIT),
        name="experts",
    )(block_expert, n_used, xs, wg, wu, wd)


def _combine_kernel(dest_ref, x1_ref, gc_ref, gfin_ref, ys_hbm, out_ref, ybuf, sem):
    tc, d = x1_ref.shape
    n = tc * TOP_K

    def copy(a):
        return pltpu.make_async_copy(ys_hbm.at[dest_ref[0, 0, a]], ybuf.at[a % TOP_K, a // TOP_K], sem)

    def issue(a, c):
        copy(a).start()
        return c

    def drain(a, c):
        copy(a).wait()
        return c

    lax.fori_loop(0, n, issue, 0)
    lax.fori_loop(0, n, drain, 0)

    g1 = gc_ref[:, 0:1]
    g2 = gc_ref[:, 1:2]
    pieces = []
    ssq = jnp.zeros((tc, 1), jnp.float32)
    for j in range(d // LANES):
        z = x1_ref[:, j * LANES:(j + 1) * LANES] + (g1 * ybuf[0, :, j, :] + g2 * ybuf[1, :, j, :])
        pieces.append(z)
        ssq = ssq + jnp.sum(z * z, axis=-1, keepdims=True)
    scale = lax.rsqrt(ssq / d + EPS)
    for j, z in enumerate(pieces):
        out_ref[:, j * LANES:(j + 1) * LANES] = z * scale * gfin_ref[:, j * LANES:(j + 1) * LANES]


def _combine_call(dest, x1, gates_col, gfin, ys):
    n_tok, d = x1.shape
    tc = min(COMBINE_TILE, n_tok)
    assert n_tok % tc == 0
    steps = n_tok // tc
    _, s, l = ys.shape
    return pl.pallas_call(
        _combine_kernel,
        grid=(steps,),
        in_specs=[
            pl.BlockSpec((1, 1, tc * TOP_K), lambda i: (i, 0, 0), memory_space=pltpu.SMEM),
            pl.BlockSpec((tc, d), lambda i: (i, 0)),
            pl.BlockSpec((tc, LANES), lambda i: (i, 0)),
            pl.BlockSpec((1, d), lambda i: (0, 0)),
            pl.BlockSpec(memory_space=pl.ANY),
        ],
        out_specs=pl.BlockSpec((tc, d), lambda i: (i, 0)),
        out_shape=jax.ShapeDtypeStruct((n_tok, d), jnp.float32),
        scratch_shapes=[pltpu.VMEM((TOP_K, tc, s, l), jnp.float32), pltpu.SemaphoreType.DMA(())],
        compiler_params=pltpu.CompilerParams(
            dimension_semantics=("arbitrary",), vmem_limit_bytes=VMEM_LIMIT),
        name="combine",
    )(dest.reshape(steps, 1, tc * TOP_K), x1, gates_col, gfin, ys)


def _prepare_layer(l, norm_mix_g, w_in, b_in, conv_w, conv_b, conv_ln_g, conv_ln_b, qk_conv_w,
                   qk_conv_b, mlstm_norm_g, w_out, norm_ffn_g, router_group_w, router_group_b,
                   router_expert_w, router_expert_b, ts):
    f32, bf16 = jnp.float32, jnp.bfloat16
    d = w_in.shape[1]
    dc = conv_w.shape[-1]
    dm = mlstm_norm_g.shape[-1]
    n_heads = (w_in.shape[-1] - 2 * dc - 4 * dm) // 2
    n_groups = router_group_w.shape[-1]
    n_experts = router_expert_w.shape[-1]
    wi, bi = w_in[l], b_in[l]
    o_qk, o_vo, o_if = 2 * dc, 2 * dc + 2 * dm, 2 * dc + 4 * dm
    w_if = wi[:, o_if:]
    b_if = bi[o_if:]
    wr_t = jnp.zeros((SUBLANES + n_experts, d), f32)
    wr_t = wr_t.at[:n_groups].set(router_group_w[l].T).at[SUBLANES:].set(router_expert_w[l].T)
    br = jnp.zeros((SUBLANES + n_experts, 1), f32)
    br = br.at[:n_groups, 0].set(router_group_b[l]).at[SUBLANES:, 0].set(router_expert_b[l])
    wr_hi = wr_t.astype(bf16)
    wr_lo = (wr_t - wr_hi.astype(f32)).astype(bf16)
    pad_rows = lambda a, n: jnp.pad(a, ((0, n - a.shape[0]), (0, 0)))
    idx = jnp.arange(ts)
    cidx = jnp.arange(CHUNK)
    return dict(
        gmix=norm_mix_g[l][None, :],
        wcv=wi[:, :o_qk].astype(bf16), bcv=bi[None, :o_qk],
        wqk=wi[:, o_qk:o_vo].astype(bf16), bqk=bi[None, o_qk:o_vo],
        wvo=wi[:, o_vo:o_if].astype(bf16), bvo=bi[None, o_vo:o_if],
        wif=jnp.pad(w_if, ((0, 0), (0, LANES - 2 * n_heads))).astype(bf16),
        bif=jnp.pad(b_if, (0, LANES - 2 * n_heads))[None, :],
        wift=pad_rows(w_if.T, SUBLANES).astype(bf16),
        bift=pad_rows(b_if[:, None], SUBLANES),
        cw=pad_rows(conv_w[l], CONV_HALO), cb=conv_b[l][None, :],
        lng=conv_ln_g[l][None, :], lnb=conv_ln_b[l][None, :],
        qkw=pad_rows(qk_conv_w[l], QK_HALO), qkb=qk_conv_b[l][None, :],
        mng=mlstm_norm_g[l][None, :],
        wout=w_out[l].astype(bf16), gffn=norm_ffn_g[l][None, :],
        wrh=wr_hi, wrl=wr_lo, br=br,
        su=(idx[:, None] < idx[None, :]).astype(bf16),
        tril=(cidx[:, None] >= cidx[None, :]).astype(bf16),
    ), dict(n_heads=n_heads, conv_width=conv_w.shape[1], qk_width=qk_conv_w.shape[1],
            n_groups=n_groups, epg=n_experts // n_groups)


def kernel(x, norm_mix_g, w_in, b_in, conv_w, conv_b, conv_ln_g, conv_ln_b, qk_conv_w, qk_conv_b,
           mlstm_norm_g, w_out, norm_ffn_g, router_group_w, router_group_b, router_expert_w,
           router_expert_b, expert_w_gate, expert_w_up, expert_w_down, final_norm_g):
    bsz, seq, d = x.shape
    n_tok = bsz * seq
    depth = w_in.shape[0]
    assert depth == 1, "the combine kernel fuses the final norm, so exactly one layer is supported"
    n_experts = router_expert_w.shape[-1]
    n_assign = n_tok * TOP_K
    n_blocks = -(-n_assign // EXPERT_BLOCK) + n_experts
    n_slots = n_blocks * EXPERT_BLOCK
    ts = min(SEQ_TILE, seq)
    out = None
    for l in range(depth):
        params, dims = _prepare_layer(
            l, norm_mix_g, w_in, b_in, conv_w, conv_b, conv_ln_g, conv_ln_b, qk_conv_w, qk_conv_b,
            mlstm_norm_g, w_out, norm_ffn_g, router_group_w, router_group_b, router_expert_w,
            router_expert_b, ts)
        x1, h2_rows, route_i, gates_col, counts = _mixer_call(x, params, **dims)

        cnt = counts[:, 0].astype(jnp.int32)
        padded = (cnt + EXPERT_BLOCK - 1) // EXPERT_BLOCK * EXPERT_BLOCK
        padded_ends = jnp.cumsum(padded)
        padded_starts = padded_ends - padded
        route = jnp.transpose(route_i, (2, 0, 1, 3)).reshape(SUBLANES, n_tok)
        dest = jnp.stack([padded_starts[route[0]] + route[2],
                          padded_starts[route[1]] + route[3]], axis=-1).reshape(-1)
        block_start = jnp.arange(n_blocks, dtype=jnp.int32) * EXPERT_BLOCK
        block_expert = jnp.minimum(
            jnp.sum(block_start[:, None] >= padded_ends[None, :], axis=1), n_experts - 1).astype(jnp.int32)
        n_used = (padded_ends[-1:] // EXPERT_BLOCK).astype(jnp.int32)
        pad_runs = jnp.concatenate([padded_starts + cnt, padded_ends[-1:],
                                    padded - cnt, n_slots - padded_ends[-1:]]).astype(jnp.int32)

        xs = _dispatch_call(pad_runs, dest, h2_rows.reshape(n_tok, d // LANES, LANES), n_slots)
        ys = _expert_call(block_expert, n_used, xs,
                          expert_w_gate[l].astype(jnp.bfloat16), expert_w_up[l].astype(jnp.bfloat16),
                          expert_w_down[l].astype(jnp.bfloat16))
        out = _combine_call(dest, x1.reshape(n_tok, d), gates_col.reshape(n_tok, LANES),
                            final_norm_g[None, :], ys)
        x = out.reshape(bsz, seq, d)
    return x
```

```python
import functools

import jax
import jax.numpy as jnp
from jax import lax
from jax.experimental import pallas as pl
from jax.experimental.pallas import tpu as pltpu

EPS = 1e-6
LANES = 128
SUBLANES = 8
CHUNK = 128
TOP_K = 2
SEQ_TILE = 512
CONV_ROWS = 32
CONV_HALO = 32
QK_HALO = 8
EXPERT_BLOCK = 256
DISPATCH_TILE = 1024
DMA_UNROLL = 8
COMBINE_TILE = 256
VMEM_LIMIT = 56 * 1024 * 1024

_NT = (((1,), (1,)), ((), ()))
_TN = (((0,), (0,)), ((), ()))


def _dot(a, b):
    return jnp.dot(a, b, preferred_element_type=jnp.float32)


def _silu(x):
    return x * jax.nn.sigmoid(x)


def _log_sigmoid(x):
    return jnp.minimum(x, 0.0) - jnp.log1p(jnp.exp(-jnp.abs(x)))


def _split_bf16(x, parts):
    out = []
    for _ in range(parts):
        p = x.astype(jnp.bfloat16)
        out.append(p)
        x = x - p.astype(jnp.float32)
    return out


def _mixer_kernel(x_ref, gmix_ref, wcv_ref, bcv_ref, wqk_ref, bqk_ref, wvo_ref, bvo_ref,
                  wif_ref, bif_ref, wift_ref, bift_ref, cw_ref, cb_ref, lng_ref, lnb_ref,
                  qkw_ref, qkb_ref, mng_ref, wout_ref, gffn_ref, wrh_ref, wrl_ref, br_ref,
                  su_ref, tril_ref,
                  x1_ref, h2_ref, ri_ref, gc_ref, cnt_ref,
                  uext, qkext, q_s, k_s, v_s, o_s, cn_s, m_s, y_s, cnt_s,
                  *, n_heads, conv_width, qk_width, n_groups, epg):
    ts, d = x_ref.shape
    dc = cw_ref.shape[1]
    dm = mng_ref.shape[1]
    dh = dm // n_heads
    n_chunks = ts // CHUNK
    n_experts = n_groups * epg
    b_idx = pl.program_id(0)
    t_idx = pl.program_id(1)

    @pl.when(t_idx == 0)
    def _reset_sequence_state():
        uext[0:CONV_HALO, :] = jnp.zeros((CONV_HALO, dc), jnp.float32)
        qkext[0:QK_HALO, :] = jnp.zeros((QK_HALO, 2 * dm), jnp.float32)
        cn_s[...] = jnp.zeros(cn_s.shape, jnp.float32)
        m_s[...] = jnp.zeros(m_s.shape, jnp.float32)

    @pl.when((t_idx == 0) & (b_idx == 0))
    def _reset_counts():
        cnt_s[...] = jnp.zeros(cnt_s.shape, jnp.float32)

    x = x_ref[...]
    h = x * lax.rsqrt(jnp.mean(x * x, axis=-1, keepdims=True) + EPS) * gmix_ref[...]
    hb = h.astype(jnp.bfloat16)

    zc = _dot(hb, wcv_ref[...]) + bcv_ref[...]
    uext[CONV_HALO:CONV_HALO + ts, :] = zc[:, :dc] * jax.nn.sigmoid(zc[:, dc:])
    first = CONV_HALO - (conv_width - 1)
    for rb in range(ts // CONV_ROWS):
        r0 = rb * CONV_ROWS
        acc = jnp.zeros((CONV_ROWS, dc), jnp.float32) + cb_ref[...]
        for j in range(conv_width):
            acc = acc + cw_ref[j:j + 1, :] * uext[r0 + first + j:r0 + first + j + CONV_ROWS, :]
        mu = jnp.mean(acc, axis=-1, keepdims=True)
        cen = acc - mu
        var = jnp.mean(cen * cen, axis=-1, keepdims=True)
        yn = cen * lax.rsqrt(var + EPS) * lng_ref[...] + lnb_ref[...]
        y_s[r0:r0 + CONV_ROWS, 0:dc] = _silu(yn).astype(jnp.bfloat16)
    uext[0:CONV_HALO, :] = uext[ts:ts + CONV_HALO, :]

    qkext[QK_HALO:QK_HALO + ts, :] = _dot(hb, wqk_ref[...]) + bqk_ref[...]
    qfirst = QK_HALO - (qk_width - 1)
    k_scale = dh ** -0.5
    for rb in range(ts // CONV_ROWS):
        r0 = rb * CONV_ROWS
        acc = jnp.zeros((CONV_ROWS, 2 * dm), jnp.float32) + qkb_ref[...]
        for j in range(qk_width):
            acc = acc + qkw_ref[j:j + 1, :] * qkext[r0 + qfirst + j:r0 + qfirst + j + CONV_ROWS, :]
        act = _silu(acc)
        q_s[r0:r0 + CONV_ROWS, :] = act[:, :dm].astype(jnp.bfloat16)
        k_s[r0:r0 + CONV_ROWS, :] = (act[:, dm:] * k_scale).astype(jnp.bfloat16)
    qkext[0:QK_HALO, :] = qkext[ts:ts + QK_HALO, :]

    zvo = _dot(hb, wvo_ref[...]) + bvo_ref[...]
    v_s[...] = zvo[:, :dm].astype(jnp.bfloat16)
    o_s[...] = jax.nn.sigmoid(zvo[:, dm:])

    zif_col = _dot(hb, wif_ref[...]) + bif_ref[...]
    zif_row = lax.dot_general(wift_ref[...], hb, _NT,
                              preferred_element_type=jnp.float32) + bift_ref[...]
    logf_col = _log_sigmoid(zif_col)
    logf_row = _log_sigmoid(zif_row)
    lane = lax.broadcasted_iota(jnp.int32, (SUBLANES, CHUNK), 1)
    causal = (lax.broadcasted_iota(jnp.int32, (CHUNK, CHUNK), 0)
              >= lax.broadcasted_iota(jnp.int32, (CHUNK, CHUNK), 1))
    ones_blk = jnp.ones((CHUNK, dh), jnp.bfloat16)
    tril = tril_ref[...]

    for c in range(n_chunks):
        r0 = c * CHUNK
        b_row = logf_row[:, r0:r0 + CHUNK]
        shift = 1
        while shift < CHUNK:
            b_row = b_row + jnp.where(lane >= shift, pltpu.roll(b_row, shift, axis=1), 0.0)
            shift *= 2
        b_col = sum(_dot(tril, p) for p in _split_bf16(logf_col[r0:r0 + CHUNK, :], 3))
        i_col_all = zif_col[r0:r0 + CHUNK, :]
        i_row_all = zif_row[:, r0:r0 + CHUNK]
        for hd in range(n_heads):
            c0 = hd * dh
            b_c = b_col[:, n_heads + hd:n_heads + hd + 1]
            i_c = i_col_all[:, hd:hd + 1]
            b_r = b_row[n_heads + hd:n_heads + hd + 1, :]
            i_r = i_row_all[hd:hd + 1, :]
            m_prev = m_s[hd, 0:1, 0:1]
            q = q_s[r0:r0 + CHUNK, c0:c0 + dh]
            k = k_s[r0:r0 + CHUNK, c0:c0 + dh]
            vaug = jnp.concatenate([v_s[r0:r0 + CHUNK, c0:c0 + dh], ones_blk], axis=-1)
            cn = cn_s[hd]

            dmat = jnp.where(causal, b_c + (i_r - b_r), -jnp.inf)
            g = b_c + m_prev
            m_t = jnp.maximum(g, jnp.max(dmat, axis=-1, keepdims=True))
            w_intra = jnp.exp(dmat - m_t)
            w_inter = jnp.exp(g - m_t)
            s = lax.dot_general(q, k, _NT, preferred_element_type=jnp.float32) * w_intra
            nd = _dot(s.astype(jnp.bfloat16), vaug) + w_inter * _dot(q, cn.astype(jnp.bfloat16))
            hval = nd[:, :dh] / jnp.maximum(jnp.abs(nd[:, dh:]), jnp.exp(-m_t))

            b_last = b_r[:, CHUNK - 1:CHUNK]
            logw_c = b_last - b_c + i_c
            m_new = jnp.maximum(b_last + m_prev, jnp.max(logw_c, axis=0, keepdims=True))
            decay = jnp.exp(b_last + m_prev - m_new)
            kw = (k.astype(jnp.float32) * jnp.exp(logw_c - m_new)).astype(jnp.bfloat16)
            cn_s[hd] = decay * cn + lax.dot_general(kw, vaug, _TN,
                                                    preferred_element_type=jnp.float32)
            m_s[hd] = jnp.broadcast_to(m_new, m_s.shape[1:])

            hm = o_s[r0:r0 + CHUNK, c0:c0 + dh] * hval
            mu = jnp.mean(hm, axis=-1, keepdims=True)
            cen = hm - mu
            var = jnp.mean(cen * cen, axis=-1, keepdims=True)
            hn = cen * lax.rsqrt(var + EPS) * mng_ref[:, c0:c0 + dh]
            y_s[r0:r0 + CHUNK, dc + c0:dc + c0 + dh] = hn.astype(jnp.bfloat16)

    x1 = x + _dot(y_s[...], wout_ref[...])
    x1_ref[...] = x1
    h2 = x1 * lax.rsqrt(jnp.mean(x1 * x1, axis=-1, keepdims=True) + EPS) * gffn_ref[...]
    nsub = d // LANES
    for sblk in range(nsub):
        h2_ref[pl.ds(sblk, ts, stride=nsub), :] = h2[:, sblk * LANES:(sblk + 1) * LANES]

    h2_hi, h2_lo = _split_bf16(h2, 2)
    lg = (lax.dot_general(wrh_ref[...], h2_hi, _NT, preferred_element_type=jnp.float32)
          + lax.dot_general(wrl_ref[...], h2_hi, _NT, preferred_element_type=jnp.float32)
          + lax.dot_general(wrh_ref[...], h2_lo, _NT, preferred_element_type=jnp.float32)
          + br_ref[...])
    gl = lg[0:n_groups, :]
    gidx = lax.broadcasted_iota(jnp.int32, (n_groups, ts), 0)
    gmax = jnp.max(gl, axis=0, keepdims=True)
    grp = jnp.min(jnp.where(gl == gmax, gidx, n_groups), axis=0, keepdims=True)
    p_grp = 1.0 / jnp.sum(jnp.exp(gl - gmax), axis=0, keepdims=True)
    in_group = jnp.zeros((epg, ts), jnp.float32)
    for gi in range(n_groups):
        in_group = jnp.where(grp == gi, lg[SUBLANES + gi * epg:SUBLANES + (gi + 1) * epg, :], in_group)
    eidx = lax.broadcasted_iota(jnp.int32, (epg, ts), 0)
    v1 = jnp.max(in_group, axis=0, keepdims=True)
    i1 = jnp.min(jnp.where(in_group == v1, eidx, epg), axis=0, keepdims=True)
    rest = jnp.where(eidx == i1, -jnp.inf, in_group)
    v2 = jnp.max(rest, axis=0, keepdims=True)
    i2 = jnp.min(jnp.where(rest == v2, eidx, epg), axis=0, keepdims=True)
    e21 = jnp.exp(v2 - v1)
    g1 = p_grp / (1.0 + e21)
    g2 = g1 * e21
    e1 = grp * epg + i1
    e2 = grp * epg + i2

    xidx = lax.broadcasted_iota(jnp.int32, (n_experts, ts), 0)
    sel1 = xidx == e1
    sel2 = xidx == e2
    onehot = jnp.where(sel1 | sel2, 1.0, 0.0).astype(jnp.bfloat16)
    before = cnt_s[:, 0:1] + _dot(onehot, su_ref[...])
    rank1 = jnp.sum(jnp.where(sel1, before, 0.0), axis=0, keepdims=True)
    rank2 = jnp.sum(jnp.where(sel2, before, 0.0), axis=0, keepdims=True)
    cnt_new = cnt_s[...] + jnp.sum(onehot.astype(jnp.float32), axis=1, keepdims=True)
    cnt_s[...] = cnt_new
    cnt_ref[...] = cnt_new

    zero_i = jnp.zeros((SUBLANES - 4, ts), jnp.int32)
    ri_ref[...] = jnp.concatenate(
        [e1, e2, rank1.astype(jnp.int32), rank2.astype(jnp.int32), zero_i], axis=0)
    gates_row = jnp.concatenate([g1, g2, jnp.zeros((LANES - 2, ts), jnp.float32)], axis=0)
    gc_ref[...] = gates_row.T


def _mixer_call(x, p, *, n_heads, conv_width, qk_width, n_groups, epg):
    bsz, seq, d = x.shape
    ts = min(SEQ_TILE, seq)
    nt = seq // ts
    dc = p["cw"].shape[1]
    dm = p["mng"].shape[1]
    dh = dm // n_heads
    n_experts = n_groups * epg
    assert seq % ts == 0 and ts % CHUNK == 0 and dh == LANES and d % LANES == 0
    assert epg == SUBLANES and n_groups <= SUBLANES and conv_width - 1 <= CONV_HALO

    consts = [p[n] for n in ("gmix", "wcv", "bcv", "wqk", "bqk", "wvo", "bvo", "wif", "bif",
                             "wift", "bift", "cw", "cb", "lng", "lnb", "qkw", "qkb", "mng",
                             "wout", "gffn", "wrh", "wrl", "br", "su", "tril")]

    def const_spec(a):
        return pl.BlockSpec(a.shape, lambda b, t: (0,) * a.ndim)

    kern = functools.partial(_mixer_kernel, n_heads=n_heads, conv_width=conv_width,
                             qk_width=qk_width, n_groups=n_groups, epg=epg)
    return pl.pallas_call(
        kern,
        grid=(bsz, nt),
        in_specs=[pl.BlockSpec((None, ts, d), lambda b, t: (b, t, 0))] + [const_spec(a) for a in consts],
        out_specs=[
            pl.BlockSpec((None, ts, d), lambda b, t: (b, t, 0)),
            pl.BlockSpec((None, ts * (d // LANES), LANES), lambda b, t: (b, t, 0)),
            pl.BlockSpec((None, None, SUBLANES, ts), lambda b, t: (b, t, 0, 0)),
            pl.BlockSpec((None, ts, LANES), lambda b, t: (b, t, 0)),
            pl.BlockSpec((n_experts, LANES), lambda b, t: (0, 0)),
        ],
        out_shape=[
            jax.ShapeDtypeStruct((bsz, seq, d), jnp.float32),
            jax.ShapeDtypeStruct((bsz, seq * (d // LANES), LANES), jnp.float32),
            jax.ShapeDtypeStruct((bsz, nt, SUBLANES, ts), jnp.int32),
            jax.ShapeDtypeStruct((bsz, seq, LANES), jnp.float32),
            jax.ShapeDtypeStruct((n_experts, LANES), jnp.float32),
        ],
        scratch_shapes=[
            pltpu.VMEM((CONV_HALO + ts, dc), jnp.float32),
            pltpu.VMEM((QK_HALO + ts, 2 * dm), jnp.float32),
            pltpu.VMEM((ts, dm), jnp.bfloat16),
            pltpu.VMEM((ts, dm), jnp.bfloat16),
            pltpu.VMEM((ts, dm), jnp.bfloat16),
            pltpu.VMEM((ts, dm), jnp.float32),
            pltpu.VMEM((n_heads, dh, 2 * dh), jnp.float32),
            pltpu.VMEM((n_heads, SUBLANES, LANES), jnp.float32),
            pltpu.VMEM((ts, d), jnp.bfloat16),
            pltpu.VMEM((n_experts, LANES), jnp.float32),
        ],
        compiler_params=pltpu.CompilerParams(
            dimension_semantics=("arbitrary", "arbitrary"), vmem_limit_bytes=VMEM_LIMIT),
        name="mixer_router",
    )(x, *consts)


def _row(ref, idx, nsub):
    return ref.at[pl.ds(pl.multiple_of(idx * nsub, nsub), nsub)]


def _unrolled_rounds(n, fn):
    def trip(g, carry):
        for u in range(DMA_UNROLL):
            fn(g * DMA_UNROLL + u)
        return carry

    lax.fori_loop(0, n // DMA_UNROLL, trip, 0)


def _dispatch_kernel(pad_ref, dest_ref, h2_ref, xs_hbm, zrow, zsem, sem, *, nsub):
    n = dest_ref.shape[-1]
    n_seg = pad_ref.shape[0] // 2

    @pl.when(pl.program_id(0) == 0)
    def _zero_padding_rows():
        zrow[...] = jnp.zeros(zrow.shape, zrow.dtype)

        def seg(e, carry):
            def zcopy(r):
                return pltpu.make_async_copy(zrow, _row(xs_hbm, pad_ref[e] + r, nsub), zsem)

            def zissue(r, c):
                zcopy(r).start()
                return c

            def zdrain(r, c):
                zcopy(r).wait()
                return c

            lax.fori_loop(0, pad_ref[n_seg + e], zissue, 0)
            lax.fori_loop(0, pad_ref[n_seg + e], zdrain, 0)
            return carry

        lax.fori_loop(0, n_seg, seg, 0)

    def copy(a):
        return pltpu.make_async_copy(_row(h2_ref, a // TOP_K, nsub),
                                     _row(xs_hbm, dest_ref[0, 0, a], nsub), sem)

    _unrolled_rounds(n, lambda a: copy(a).start())
    _unrolled_rounds(n, lambda a: copy(a).wait())


def _dispatch_call(pad_runs, dest, h2_rows, n_slots, nsub):
    rows, l = h2_rows.shape
    n_assign = dest.shape[0]
    per_step = min(DISPATCH_TILE * TOP_K, n_assign)
    assert n_assign % per_step == 0 and per_step % DMA_UNROLL == 0
    steps = n_assign // per_step
    return pl.pallas_call(
        functools.partial(_dispatch_kernel, nsub=nsub),
        grid_spec=pltpu.PrefetchScalarGridSpec(
            num_scalar_prefetch=1,
            grid=(steps,),
            in_specs=[
                pl.BlockSpec((1, 1, per_step), lambda i, pad: (i, 0, 0), memory_space=pltpu.SMEM),
                pl.BlockSpec((per_step // TOP_K * nsub, l), lambda i, pad: (i, 0)),
            ],
            out_specs=pl.BlockSpec(memory_space=pl.ANY),
            scratch_shapes=[pltpu.VMEM((nsub, l), h2_rows.dtype), pltpu.SemaphoreType.DMA(()),
                            pltpu.SemaphoreType.DMA(())],
        ),
        out_shape=jax.ShapeDtypeStruct((n_slots * nsub, l), h2_rows.dtype),
        compiler_params=pltpu.CompilerParams(
            dimension_semantics=("arbitrary",), vmem_limit_bytes=VMEM_LIMIT),
        name="dispatch",
    )(pad_runs, dest.reshape(steps, 1, per_step), h2_rows)


def _expert_kernel(be_ref, nused_ref, xs_ref, wg_ref, wu_ref, wd_ref, ys_ref, *, nsub):
    i = pl.program_id(0)
    blk = xs_ref.shape[0] // nsub
    l = xs_ref.shape[1]

    @pl.when(i < nused_ref[0])
    def _compute():
        xb = jnp.concatenate(
            [xs_ref[pl.ds(j, blk, stride=nsub), :].astype(jnp.bfloat16) for j in range(nsub)], axis=-1)
        a = _dot(xb, wg_ref[...])
        u = _dot(xb, wu_ref[...])
        y = _dot((_silu(a) * u).astype(jnp.bfloat16), wd_ref[...])
        for j in range(nsub):
            ys_ref[pl.ds(j, blk, stride=nsub), :] = y[:, j * l:(j + 1) * l]

    @pl.when(i >= nused_ref[0])
    def _unused_block():
        ys_ref[...] = jnp.zeros(ys_ref.shape, ys_ref.dtype)


def _expert_call(block_expert, n_used, xs, wg, wu, wd, nsub):
    rows, l = xs.shape
    n_blocks = rows // (EXPERT_BLOCK * nsub)
    _, d, de = wg.shape

    def row_map(i, be, nu):
        return (i, 0)

    def w_map(i, be, nu):
        return (be[i], 0, 0)

    return pl.pallas_call(
        functools.partial(_expert_kernel, nsub=nsub),
        grid_spec=pltpu.PrefetchScalarGridSpec(
            num_scalar_prefetch=2,
            grid=(n_blocks,),
            in_specs=[
                pl.BlockSpec((EXPERT_BLOCK * nsub, l), row_map),
                pl.BlockSpec((None, d, de), w_map),
                pl.BlockSpec((None, d, de), w_map),
                pl.BlockSpec((None, de, d), w_map),
            ],
            out_specs=pl.BlockSpec((EXPERT_BLOCK * nsub, l), row_map),
        ),
        out_shape=jax.ShapeDtypeStruct((rows, l), jnp.float32),
        compiler_params=pltpu.CompilerParams(
            dimension_semantics=("arbitrary",), vmem_limit_bytes=VMEM_LIMIT),
        name="experts",
    )(block_expert, n_used, xs, wg, wu, wd)


def _combine_kernel(dest_ref, dnext_ref, x1_ref, gc_ref, gfin_ref, ys_hbm, out_ref, ybuf, sems, *, nsub):
    tc, d = x1_ref.shape
    n = tc * TOP_K
    i = pl.program_id(0)
    slot = i % 2

    def copy(dref, buf, a):
        return pltpu.make_async_copy(_row(ys_hbm, dref[0, 0, a], nsub),
                                     _row(ybuf.at[buf, a % TOP_K], a // TOP_K, nsub), sems.at[buf])

    @pl.when(i == 0)
    def _first_tile():
        _unrolled_rounds(n, lambda a: copy(dest_ref, 0, a).start())

    @pl.when(i + 1 < pl.num_programs(0))
    def _prefetch_next_tile():
        _unrolled_rounds(n, lambda a: copy(dnext_ref, 1 - slot, a).start())

    _unrolled_rounds(n, lambda a: copy(dest_ref, slot, a).wait())

    g1 = gc_ref[:, 0:1]
    g2 = gc_ref[:, 1:2]
    pieces = []
    ssq = jnp.zeros((tc, 1), jnp.float32)
    for j in range(nsub):
        y1 = ybuf[slot, 0, pl.ds(j, tc, stride=nsub), :]
        y2 = ybuf[slot, 1, pl.ds(j, tc, stride=nsub), :]
        z = x1_ref[:, j * LANES:(j + 1) * LANES] + (g1 * y1 + g2 * y2)
        pieces.append(z)
        ssq = ssq + jnp.sum(z * z, axis=-1, keepdims=True)
    scale = lax.rsqrt(ssq / d + EPS)
    for j, z in enumerate(pieces):
        out_ref[:, j * LANES:(j + 1) * LANES] = z * scale * gfin_ref[:, j * LANES:(j + 1) * LANES]


def _combine_call(dest, x1, gates_col, gfin, ys, nsub):
    n_tok, d = x1.shape
    tc = min(COMBINE_TILE, n_tok)
    assert n_tok % tc == 0 and (tc * TOP_K) % DMA_UNROLL == 0
    steps = n_tok // tc
    l = ys.shape[1]
    dest3 = dest.reshape(steps, 1, tc * TOP_K)
    return pl.pallas_call(
        functools.partial(_combine_kernel, nsub=nsub),
        grid=(steps,),
        in_specs=[
            pl.BlockSpec((1, 1, tc * TOP_K), lambda i: (i, 0, 0), memory_space=pltpu.SMEM),
            pl.BlockSpec((1, 1, tc * TOP_K), lambda i: (jnp.minimum(i + 1, steps - 1), 0, 0),
                         memory_space=pltpu.SMEM),
            pl.BlockSpec((tc, d), lambda i: (i, 0)),
            pl.BlockSpec((tc, LANES), lambda i: (i, 0)),
            pl.BlockSpec((1, d), lambda i: (0, 0)),
            pl.BlockSpec(memory_space=pl.ANY),
        ],
        out_specs=pl.BlockSpec((tc, d), lambda i: (i, 0)),
        out_shape=jax.ShapeDtypeStruct((n_tok, d), jnp.float32),
        scratch_shapes=[pltpu.VMEM((2, TOP_K, tc * nsub, l), jnp.float32),
                        pltpu.SemaphoreType.DMA((2,))],
        compiler_params=pltpu.CompilerParams(
            dimension_semantics=("arbitrary",), vmem_limit_bytes=VMEM_LIMIT),
        name="combine",
    )(dest3, dest3, x1, gates_col, gfin, ys)


def _prepare_layer(l, norm_mix_g, w_in, b_in, conv_w, conv_b, conv_ln_g, conv_ln_b, qk_conv_w,
                   qk_conv_b, mlstm_norm_g, w_out, norm_ffn_g, router_group_w, router_group_b,
                   router_expert_w, router_expert_b, ts):
    f32, bf16 = jnp.float32, jnp.bfloat16
    d = w_in.shape[1]
    dc = conv_w.shape[-1]
    dm = mlstm_norm_g.shape[-1]
    n_heads = (w_in.shape[-1] - 2 * dc - 4 * dm) // 2
    n_groups = router_group_w.shape[-1]
    n_experts = router_expert_w.shape[-1]
    wi, bi = w_in[l], b_in[l]
    o_qk, o_vo, o_if = 2 * dc, 2 * dc + 2 * dm, 2 * dc + 4 * dm
    w_if = wi[:, o_if:]
    b_if = bi[o_if:]
    wr_t = jnp.zeros((SUBLANES + n_experts, d), f32)
    wr_t = wr_t.at[:n_groups].set(router_group_w[l].T).at[SUBLANES:].set(router_expert_w[l].T)
    br = jnp.zeros((SUBLANES + n_experts, 1), f32)
    br = br.at[:n_groups, 0].set(router_group_b[l]).at[SUBLANES:, 0].set(router_expert_b[l])
    wr_hi = wr_t.astype(bf16)
    wr_lo = (wr_t - wr_hi.astype(f32)).astype(bf16)
    pad_rows = lambda a, n: jnp.pad(a, ((0, n - a.shape[0]), (0, 0)))
    idx = jnp.arange(ts)
    cidx = jnp.arange(CHUNK)
    return dict(
        gmix=norm_mix_g[l][None, :],
        wcv=wi[:, :o_qk].astype(bf16), bcv=bi[None, :o_qk],
        wqk=wi[:, o_qk:o_vo].astype(bf16), bqk=bi[None, o_qk:o_vo],
        wvo=wi[:, o_vo:o_if].astype(bf16), bvo=bi[None, o_vo:o_if],
        wif=jnp.pad(w_if, ((0, 0), (0, LANES - 2 * n_heads))).astype(bf16),
        bif=jnp.pad(b_if, (0, LANES - 2 * n_heads))[None, :],
        wift=pad_rows(w_if.T, SUBLANES).astype(bf16),
        bift=pad_rows(b_if[:, None], SUBLANES),
        cw=pad_rows(conv_w[l], CONV_HALO), cb=conv_b[l][None, :],
        lng=conv_ln_g[l][None, :], lnb=conv_ln_b[l][None, :],
        qkw=pad_rows(qk_conv_w[l], QK_HALO), qkb=qk_conv_b[l][None, :],
        mng=mlstm_norm_g[l][None, :],
        wout=w_out[l].astype(bf16), gffn=norm_ffn_g[l][None, :],
        wrh=wr_hi, wrl=wr_lo, br=br,
        su=(idx[:, None] < idx[None, :]).astype(bf16),
        tril=(cidx[:, None] >= cidx[None, :]).astype(bf16),
    ), dict(n_heads=n_heads, conv_width=conv_w.shape[1], qk_width=qk_conv_w.shape[1],
            n_groups=n_groups, epg=n_experts // n_groups)


def kernel(x, norm_mix_g, w_in, b_in, conv_w, conv_b, conv_ln_g, conv_ln_b, qk_conv_w, qk_conv_b,
           mlstm_norm_g, w_out, norm_ffn_g, router_group_w, router_group_b, router_expert_w,
           router_expert_b, expert_w_gate, expert_w_up, expert_w_down, final_norm_g):
    bsz, seq, d = x.shape
    n_tok = bsz * seq
    depth = w_in.shape[0]
    assert depth == 1, "the combine kernel fuses the final norm, so exactly one layer is supported"
    n_experts = router_expert_w.shape[-1]
    n_assign = n_tok * TOP_K
    n_blocks = -(-n_assign // EXPERT_BLOCK) + n_experts
    n_slots = n_blocks * EXPERT_BLOCK
    ts = min(SEQ_TILE, seq)
    out = None
    for l in range(depth):
        params, dims = _prepare_layer(
            l, norm_mix_g, w_in, b_in, conv_w, conv_b, conv_ln_g, conv_ln_b, qk_conv_w, qk_conv_b,
            mlstm_norm_g, w_out, norm_ffn_g, router_group_w, router_group_b, router_expert_w,
            router_expert_b, ts)
        x1, h2_rows, route_i, gates_col, counts = _mixer_call(x, params, **dims)

        cnt = counts[:, 0].astype(jnp.int32)
        padded = (cnt + EXPERT_BLOCK - 1) // EXPERT_BLOCK * EXPERT_BLOCK
        padded_ends = jnp.cumsum(padded)
        padded_starts = padded_ends - padded
        route = jnp.transpose(route_i, (2, 0, 1, 3)).reshape(SUBLANES, n_tok)
        dest = jnp.stack([padded_starts[route[0]] + route[2],
                          padded_starts[route[1]] + route[3]], axis=-1).reshape(-1)
        block_start = jnp.arange(n_blocks, dtype=jnp.int32) * EXPERT_BLOCK
        block_expert = jnp.minimum(
            jnp.sum(block_start[:, None] >= padded_ends[None, :], axis=1), n_experts - 1).astype(jnp.int32)
        n_used = (padded_ends[-1:] // EXPERT_BLOCK).astype(jnp.int32)
        pad_runs = jnp.concatenate([padded_starts + cnt, padded_ends[-1:],
                                    padded - cnt, n_slots - padded_ends[-1:]]).astype(jnp.int32)

        nsub = d // LANES
        xs = _dispatch_call(pad_runs, dest, h2_rows.reshape(n_tok * nsub, LANES), n_slots, nsub)
        ys = _expert_call(block_expert, n_used, xs,
                          expert_w_gate[l].astype(jnp.bfloat16), expert_w_up[l].astype(jnp.bfloat16),
                          expert_w_down[l].astype(jnp.bfloat16), nsub)
        out = _combine_call(dest, x1.reshape(n_tok, d), gates_col.reshape(n_tok, LANES),
                            final_norm_g[None, :], ys, nsub)
        x = out.reshape(bsz, seq, d)
    return x
```

```python
import functools

import jax
import jax.numpy as jnp
from jax import lax
from jax.experimental import pallas as pl
from jax.experimental.pallas import tpu as pltpu
from jax.experimental.pallas import tpu_sc as plsc

EPS = 1e-6
LANES = 128
SUBLANES = 8
CHUNK = 128
TOP_K = 2
SEQ_TILE = 512
CONV_ROWS = 32
CONV_HALO = 32
QK_HALO = 8
EXPERT_BLOCK = 256
SC_CHUNK = 32
COMBINE_TILE = 512
VMEM_LIMIT = 56 * 1024 * 1024

_NT = (((1,), (1,)), ((), ()))
_TN = (((0,), (0,)), ((), ()))


def _dot(a, b):
    return jnp.dot(a, b, preferred_element_type=jnp.float32)


def _silu(x):
    return x * jax.nn.sigmoid(x)


def _log_sigmoid(x):
    return jnp.minimum(x, 0.0) - jnp.log1p(jnp.exp(-jnp.abs(x)))


def _split_bf16(x, parts):
    out = []
    for _ in range(parts):
        p = x.astype(jnp.bfloat16)
        out.append(p)
        x = x - p.astype(jnp.float32)
    return out


def _mixer_kernel(x_ref, gmix_ref, wcv_ref, bcv_ref, wqk_ref, bqk_ref, wvo_ref, bvo_ref,
                  wif_ref, bif_ref, wift_ref, bift_ref, cw_ref, cb_ref, lng_ref, lnb_ref,
                  qkw_ref, qkb_ref, mng_ref, wout_ref, gffn_ref, wrh_ref, wrl_ref, br_ref,
                  su_ref, tril_ref,
                  x1_ref, h2_ref, ri_ref, gc_ref, cnt_ref,
                  uext, qkext, q_s, k_s, v_s, o_s, cn_s, m_s, y_s, cnt_s,
                  *, n_heads, conv_width, qk_width, n_groups, epg):
    ts, d = x_ref.shape
    dc = cw_ref.shape[1]
    dm = mng_ref.shape[1]
    dh = dm // n_heads
    n_chunks = ts // CHUNK
    n_experts = n_groups * epg
    b_idx = pl.program_id(0)
    t_idx = pl.program_id(1)

    @pl.when(t_idx == 0)
    def _reset_sequence_state():
        uext[0:CONV_HALO, :] = jnp.zeros((CONV_HALO, dc), jnp.float32)
        uext[CONV_HALO + ts:, :] = jnp.zeros((SUBLANES, dc), jnp.float32)
        qkext[0:QK_HALO, :] = jnp.zeros((QK_HALO, 2 * dm), jnp.float32)
        cn_s[...] = jnp.zeros(cn_s.shape, jnp.float32)
        m_s[...] = jnp.zeros(m_s.shape, jnp.float32)

    @pl.when((t_idx == 0) & (b_idx == 0))
    def _reset_counts():
        cnt_s[...] = jnp.zeros(cnt_s.shape, jnp.float32)

    x = x_ref[...]
    h = x * lax.rsqrt(jnp.mean(x * x, axis=-1, keepdims=True) + EPS) * gmix_ref[...]
    hb = h.astype(jnp.bfloat16)

    zc = _dot(hb, wcv_ref[...]) + bcv_ref[...]
    uext[CONV_HALO:CONV_HALO + ts, :] = zc[:, :dc] * jax.nn.sigmoid(zc[:, dc:])
    first = CONV_HALO - (conv_width - 1)
    sub = lax.broadcasted_iota(jnp.int32, (SUBLANES, dc), 0)
    n_grp = CONV_ROWS // SUBLANES
    for rb in range(ts // CONV_ROWS):
        r0 = rb * CONV_ROWS
        acc = jnp.zeros((CONV_ROWS, dc), jnp.float32) + cb_ref[...]
        for r in range(SUBLANES):
            part = None
            for j in range(conv_width):
                if (first + j) % SUBLANES != r:
                    continue
                a0 = r0 + (first + j) - r
                w8 = cw_ref[j * SUBLANES:(j + 1) * SUBLANES, :]
                term = jnp.concatenate([w8] * (n_grp + 1), axis=0) * uext[a0:a0 + CONV_ROWS + SUBLANES, :]
                part = term if part is None else part + term
            if part is None:
                continue
            if r == 0:
                acc = acc + part[:CONV_ROWS, :]
                continue
            rolled = [pltpu.roll(part[g * SUBLANES:(g + 1) * SUBLANES, :], SUBLANES - r, axis=0)
                      for g in range(n_grp + 1)]
            acc = acc + jnp.concatenate(
                [jnp.where(sub < SUBLANES - r, rolled[g], rolled[g + 1]) for g in range(n_grp)], axis=0)
        mu = jnp.mean(acc, axis=-1, keepdims=True)
        cen = acc - mu
        var = jnp.mean(cen * cen, axis=-1, keepdims=True)
        yn = cen * lax.rsqrt(var + EPS) * lng_ref[...] + lnb_ref[...]
        y_s[r0:r0 + CONV_ROWS, 0:dc] = _silu(yn).astype(jnp.bfloat16)
    uext[0:CONV_HALO, :] = uext[ts:ts + CONV_HALO, :]

    qkext[QK_HALO:QK_HALO + ts, :] = _dot(hb, wqk_ref[...]) + bqk_ref[...]
    qfirst = QK_HALO - (qk_width - 1)
    k_scale = dh ** -0.5
    for rb in range(ts // CONV_ROWS):
        r0 = rb * CONV_ROWS
        acc = jnp.zeros((CONV_ROWS, 2 * dm), jnp.float32) + qkb_ref[...]
        for j in range(qk_width):
            w8 = qkw_ref[j * SUBLANES:(j + 1) * SUBLANES, :]
            acc = acc + (jnp.concatenate([w8] * (CONV_ROWS // SUBLANES), axis=0)
                         * qkext[r0 + qfirst + j:r0 + qfirst + j + CONV_ROWS, :])
        act = _silu(acc)
        q_s[r0:r0 + CONV_ROWS, :] = act[:, :dm].astype(jnp.bfloat16)
        k_s[r0:r0 + CONV_ROWS, :] = (act[:, dm:] * k_scale).astype(jnp.bfloat16)
    qkext[0:QK_HALO, :] = qkext[ts:ts + QK_HALO, :]

    zvo = _dot(hb, wvo_ref[...]) + bvo_ref[...]
    v_s[...] = zvo[:, :dm].astype(jnp.bfloat16)
    o_s[...] = jax.nn.sigmoid(zvo[:, dm:])

    zif_col = _dot(hb, wif_ref[...]) + bif_ref[...]
    zif_row = lax.dot_general(wift_ref[...], hb, _NT,
                              preferred_element_type=jnp.float32) + bift_ref[...]
    logf_col = _log_sigmoid(zif_col)
    logf_row = _log_sigmoid(zif_row)
    lane = lax.broadcasted_iota(jnp.int32, (SUBLANES, CHUNK), 1)
    causal = (lax.broadcasted_iota(jnp.int32, (CHUNK, CHUNK), 0)
              >= lax.broadcasted_iota(jnp.int32, (CHUNK, CHUNK), 1))
    ones_blk = jnp.ones((CHUNK, dh), jnp.bfloat16)
    tril = tril_ref[...]

    for c in range(n_chunks):
        r0 = c * CHUNK
        b_row = logf_row[:, r0:r0 + CHUNK]
        shift = 1
        while shift < CHUNK:
            b_row = b_row + jnp.where(lane >= shift, pltpu.roll(b_row, shift, axis=1), 0.0)
            shift *= 2
        b_col = sum(_dot(tril, p) for p in _split_bf16(logf_col[r0:r0 + CHUNK, :], 3))
        i_col_all = zif_col[r0:r0 + CHUNK, :]
        i_row_all = zif_row[:, r0:r0 + CHUNK]
        for hd in range(n_heads):
            c0 = hd * dh
            b_c = b_col[:, n_heads + hd:n_heads + hd + 1]
            i_c = i_col_all[:, hd:hd + 1]
            b_r = b_row[n_heads + hd:n_heads + hd + 1, :]
            i_r = i_row_all[hd:hd + 1, :]
            m_prev = m_s[hd, 0:1, 0:1]
            q = q_s[r0:r0 + CHUNK, c0:c0 + dh]
            k = k_s[r0:r0 + CHUNK, c0:c0 + dh]
            vaug = jnp.concatenate([v_s[r0:r0 + CHUNK, c0:c0 + dh], ones_blk], axis=-1)
            cn = cn_s[hd]

            dmat = jnp.where(causal, b_c + (i_r - b_r), -jnp.inf)
            g = b_c + m_prev
            m_t = jnp.maximum(g, jnp.max(dmat, axis=-1, keepdims=True))
            w_intra = jnp.exp(dmat - m_t)
            w_inter = jnp.exp(g - m_t)
            s = lax.dot_general(q, k, _NT, preferred_element_type=jnp.float32) * w_intra
            nd = _dot(s.astype(jnp.bfloat16), vaug) + w_inter * _dot(q, cn.astype(jnp.bfloat16))
            hval = nd[:, :dh] / jnp.maximum(jnp.abs(nd[:, dh:]), jnp.exp(-m_t))

            b_last = b_r[:, CHUNK - 1:CHUNK]
            logw_c = b_last - b_c + i_c
            m_new = jnp.maximum(b_last + m_prev, jnp.max(logw_c, axis=0, keepdims=True))
            decay = jnp.exp(b_last + m_prev - m_new)
            kw = (k.astype(jnp.float32) * jnp.exp(logw_c - m_new)).astype(jnp.bfloat16)
            cn_s[hd] = decay * cn + lax.dot_general(kw, vaug, _TN,
                                                    preferred_element_type=jnp.float32)
            m_s[hd] = jnp.broadcast_to(m_new, m_s.shape[1:])

            hm = o_s[r0:r0 + CHUNK, c0:c0 + dh] * hval
            mu = jnp.mean(hm, axis=-1, keepdims=True)
            cen = hm - mu
            var = jnp.mean(cen * cen, axis=-1, keepdims=True)
            hn = cen * lax.rsqrt(var + EPS) * mng_ref[:, c0:c0 + dh]
            y_s[r0:r0 + CHUNK, dc + c0:dc + c0 + dh] = hn.astype(jnp.bfloat16)

    x1 = x + _dot(y_s[...], wout_ref[...])
    x1_ref[...] = x1
    h2 = x1 * lax.rsqrt(jnp.mean(x1 * x1, axis=-1, keepdims=True) + EPS) * gffn_ref[...]
    nsub = d // LANES
    for sblk in range(nsub):
        h2_ref[pl.ds(sblk, ts, stride=nsub), :] = h2[:, sblk * LANES:(sblk + 1) * LANES]

    h2_hi, h2_lo = _split_bf16(h2, 2)
    lg = (lax.dot_general(wrh_ref[...], h2_hi, _NT, preferred_element_type=jnp.float32)
          + lax.dot_general(wrl_ref[...], h2_hi, _NT, preferred_element_type=jnp.float32)
          + lax.dot_general(wrh_ref[...], h2_lo, _NT, preferred_element_type=jnp.float32)
          + br_ref[...])
    gl = lg[0:n_groups, :]
    gidx = lax.broadcasted_iota(jnp.int32, (n_groups, ts), 0)
    gmax = jnp.max(gl, axis=0, keepdims=True)
    grp = jnp.min(jnp.where(gl == gmax, gidx, n_groups), axis=0, keepdims=True)
    p_grp = 1.0 / jnp.sum(jnp.exp(gl - gmax), axis=0, keepdims=True)
    in_group = jnp.zeros((epg, ts), jnp.float32)
    for gi in range(n_groups):
        in_group = jnp.where(grp == gi, lg[SUBLANES + gi * epg:SUBLANES + (gi + 1) * epg, :], in_group)
    eidx = lax.broadcasted_iota(jnp.int32, (epg, ts), 0)
    v1 = jnp.max(in_group, axis=0, keepdims=True)
    i1 = jnp.min(jnp.where(in_group == v1, eidx, epg), axis=0, keepdims=True)
    rest = jnp.where(eidx == i1, -jnp.inf, in_group)
    v2 = jnp.max(rest, axis=0, keepdims=True)
    i2 = jnp.min(jnp.where(rest == v2, eidx, epg), axis=0, keepdims=True)
    e21 = jnp.exp(v2 - v1)
    g1 = p_grp / (1.0 + e21)
    g2 = g1 * e21
    e1 = grp * epg + i1
    e2 = grp * epg + i2

    xidx = lax.broadcasted_iota(jnp.int32, (n_experts, ts), 0)
    sel1 = xidx == e1
    sel2 = xidx == e2
    onehot = jnp.where(sel1 | sel2, 1.0, 0.0).astype(jnp.bfloat16)
    before = cnt_s[:, 0:1] + _dot(onehot, su_ref[...])
    rank1 = jnp.sum(jnp.where(sel1, before, 0.0), axis=0, keepdims=True)
    rank2 = jnp.sum(jnp.where(sel2, before, 0.0), axis=0, keepdims=True)
    cnt_new = cnt_s[...] + jnp.sum(onehot.astype(jnp.float32), axis=1, keepdims=True)
    cnt_s[...] = cnt_new
    cnt_ref[...] = cnt_new

    zero_i = jnp.zeros((SUBLANES - 4, ts), jnp.int32)
    ri_ref[...] = jnp.concatenate(
        [e1, e2, rank1.astype(jnp.int32), rank2.astype(jnp.int32), zero_i], axis=0)
    gates_row = jnp.concatenate([g1, g2, jnp.zeros((LANES - 2, ts), jnp.float32)], axis=0)
    gc_ref[...] = gates_row.T


def _mixer_call(x, p, *, n_heads, conv_width, qk_width, n_groups, epg):
    bsz, seq, d = x.shape
    ts = min(SEQ_TILE, seq)
    nt = seq // ts
    dc = p["cw"].shape[1]
    dm = p["mng"].shape[1]
    dh = dm // n_heads
    n_experts = n_groups * epg
    assert seq % ts == 0 and ts % CHUNK == 0 and dh == LANES and d % LANES == 0
    assert epg == SUBLANES and n_groups <= SUBLANES and conv_width - 1 <= CONV_HALO

    consts = [p[n] for n in ("gmix", "wcv", "bcv", "wqk", "bqk", "wvo", "bvo", "wif", "bif",
                             "wift", "bift", "cw", "cb", "lng", "lnb", "qkw", "qkb", "mng",
                             "wout", "gffn", "wrh", "wrl", "br", "su", "tril")]

    def const_spec(a):
        return pl.BlockSpec(a.shape, lambda b, t: (0,) * a.ndim)

    kern = functools.partial(_mixer_kernel, n_heads=n_heads, conv_width=conv_width,
                             qk_width=qk_width, n_groups=n_groups, epg=epg)
    return pl.pallas_call(
        kern,
        grid=(bsz, nt),
        in_specs=[pl.BlockSpec((None, ts, d), lambda b, t: (b, t, 0))] + [const_spec(a) for a in consts],
        out_specs=[
            pl.BlockSpec((None, ts, d), lambda b, t: (b, t, 0)),
            pl.BlockSpec((None, ts * (d // LANES), LANES), lambda b, t: (b, t, 0)),
            pl.BlockSpec((None, None, SUBLANES, ts), lambda b, t: (b, t, 0, 0)),
            pl.BlockSpec((None, ts, LANES), lambda b, t: (b, t, 0)),
            pl.BlockSpec((n_experts, LANES), lambda b, t: (0, 0)),
        ],
        out_shape=[
            jax.ShapeDtypeStruct((bsz, seq, d), jnp.float32),
            jax.ShapeDtypeStruct((bsz, seq * (d // LANES), LANES), jnp.float32),
            jax.ShapeDtypeStruct((bsz, nt, SUBLANES, ts), jnp.int32),
            jax.ShapeDtypeStruct((bsz, seq, LANES), jnp.float32),
            jax.ShapeDtypeStruct((n_experts, LANES), jnp.float32),
        ],
        scratch_shapes=[
            pltpu.VMEM((CONV_HALO + ts + SUBLANES, dc), jnp.float32),
            pltpu.VMEM((QK_HALO + ts, 2 * dm), jnp.float32),
            pltpu.VMEM((ts, dm), jnp.bfloat16),
            pltpu.VMEM((ts, dm), jnp.bfloat16),
            pltpu.VMEM((ts, dm), jnp.bfloat16),
            pltpu.VMEM((ts, dm), jnp.float32),
            pltpu.VMEM((n_heads, dh, 2 * dh), jnp.float32),
            pltpu.VMEM((n_heads, SUBLANES, LANES), jnp.float32),
            pltpu.VMEM((ts, d), jnp.bfloat16),
            pltpu.VMEM((n_experts, LANES), jnp.float32),
        ],
        compiler_params=pltpu.CompilerParams(
            dimension_semantics=("arbitrary", "arbitrary"), vmem_limit_bytes=VMEM_LIMIT),
        name="mixer_router",
    )(x, *consts)


def _sc_workers():
    info = plsc.get_sparse_core_info()
    return info.num_cores, info.num_cores * info.num_subcores


def _worker_index_table(dest, n_workers):
    top_k, n_tok = dest.shape
    assert n_tok % (n_workers * SC_CHUNK) == 0
    return jnp.transpose(dest.reshape(top_k, n_workers, n_tok // (n_workers * SC_CHUNK), SC_CHUNK),
                         (1, 0, 2, 3))


def _sc_dispatch(dest_w, rows, n_slots):
    n_tok, nsub, l = rows.shape
    nw, top_k, n_chunks, ch = dest_w.shape
    tpw = n_tok // nw
    n_cores, n_workers = _sc_workers()
    assert nw == n_workers and tpw == n_chunks * ch

    @functools.partial(
        pl.kernel,
        mesh=plsc.VectorSubcoreMesh(core_axis_name="c", subcore_axis_name="s"),
        out_type=jax.ShapeDtypeStruct((n_slots, nsub, l), rows.dtype),
        scratch_types=[pltpu.VMEM((top_k, n_chunks, ch), jnp.int32),
                       pltpu.VMEM((ch, nsub, l), rows.dtype),
                       pltpu.SemaphoreType.DMA],
        name="sc_dispatch",
    )
    def run(dest_hbm, rows_hbm, out_hbm, idx_v, buf, sem):
        wid = lax.axis_index("s") * n_cores + lax.axis_index("c")
        pltpu.sync_copy(dest_hbm.at[wid], idx_v)

        @pl.loop(0, n_chunks)
        def _(j):
            pltpu.sync_copy(rows_hbm.at[pl.ds(wid * tpw + j * ch, ch)], buf)
            for k in range(top_k):
                pltpu.async_copy(buf, out_hbm.at[idx_v.at[k, j]], sem).wait()

    return run(dest_w, rows)


def _sc_gather(dest_w, ys, n_tok):
    _, nsub, l = ys.shape
    nw, top_k, n_chunks, ch = dest_w.shape
    tpw = n_tok // nw
    n_cores, n_workers = _sc_workers()
    assert nw == n_workers and tpw == n_chunks * ch

    @functools.partial(
        pl.kernel,
        mesh=plsc.VectorSubcoreMesh(core_axis_name="c", subcore_axis_name="s"),
        out_type=jax.ShapeDtypeStruct((top_k, n_tok, nsub, l), ys.dtype),
        scratch_types=[pltpu.VMEM((top_k, n_chunks, ch), jnp.int32),
                       pltpu.VMEM((ch, nsub, l), ys.dtype),
                       pltpu.SemaphoreType.DMA],
        name="sc_gather",
    )
    def run(dest_hbm, ys_hbm, out_hbm, idx_v, buf, sem):
        wid = lax.axis_index("s") * n_cores + lax.axis_index("c")
        pltpu.sync_copy(dest_hbm.at[wid], idx_v)

        @pl.loop(0, n_chunks)
        def _(j):
            for k in range(top_k):
                pltpu.async_copy(ys_hbm.at[idx_v.at[k, j]], buf, sem).wait()
                pltpu.sync_copy(buf, out_hbm.at[k, pl.ds(wid * tpw + j * ch, ch)])

    return run(dest_w, ys)


def _expert_kernel(be_ref, valid_ref, nused_ref, xs_ref, wg_ref, wu_ref, wd_ref, ys_ref, *, nsub):
    i = pl.program_id(0)
    blk = xs_ref.shape[0] // nsub
    l = xs_ref.shape[1]

    @pl.when(i < nused_ref[0])
    def _compute():
        live = lax.broadcasted_iota(jnp.int32, (blk, l), 0) < valid_ref[i]
        xb = jnp.concatenate(
            [jnp.where(live, xs_ref[pl.ds(j, blk, stride=nsub), :], 0.0).astype(jnp.bfloat16)
             for j in range(nsub)], axis=-1)
        a = _dot(xb, wg_ref[...])
        u = _dot(xb, wu_ref[...])
        y = _dot((_silu(a) * u).astype(jnp.bfloat16), wd_ref[...])
        for j in range(nsub):
            ys_ref[pl.ds(j, blk, stride=nsub), :] = y[:, j * l:(j + 1) * l]

    @pl.when(i >= nused_ref[0])
    def _unused_block():
        ys_ref[...] = jnp.zeros(ys_ref.shape, ys_ref.dtype)


def _expert_call(block_expert, block_valid, n_used, xs, wg, wu, wd, nsub):
    rows, l = xs.shape
    n_blocks = rows // (EXPERT_BLOCK * nsub)
    _, d, de = wg.shape

    def row_map(i, be, bv, nu):
        return (i, 0)

    def w_map(i, be, bv, nu):
        return (be[i], 0, 0)

    return pl.pallas_call(
        functools.partial(_expert_kernel, nsub=nsub),
        grid_spec=pltpu.PrefetchScalarGridSpec(
            num_scalar_prefetch=3,
            grid=(n_blocks,),
            in_specs=[
                pl.BlockSpec((EXPERT_BLOCK * nsub, l), row_map),
                pl.BlockSpec((None, d, de), w_map),
                pl.BlockSpec((None, d, de), w_map),
                pl.BlockSpec((None, de, d), w_map),
            ],
            out_specs=pl.BlockSpec((EXPERT_BLOCK * nsub, l), row_map),
        ),
        out_shape=jax.ShapeDtypeStruct((rows, l), jnp.float32),
        compiler_params=pltpu.CompilerParams(
            dimension_semantics=("arbitrary",), vmem_limit_bytes=VMEM_LIMIT),
        name="experts",
    )(block_expert, block_valid, n_used, xs, wg, wu, wd)


def _combine_kernel(x1_ref, gc_ref, gfin_ref, y_ref, out_ref, *, nsub):
    tc, d = x1_ref.shape
    g1 = gc_ref[:, 0:1]
    g2 = gc_ref[:, 1:2]
    pieces = []
    ssq = jnp.zeros((tc, 1), jnp.float32)
    for j in range(nsub):
        y1 = y_ref[0, pl.ds(j, tc, stride=nsub), :]
        y2 = y_ref[1, pl.ds(j, tc, stride=nsub), :]
        z = x1_ref[:, j * LANES:(j + 1) * LANES] + (g1 * y1 + g2 * y2)
        pieces.append(z)
        ssq = ssq + jnp.sum(z * z, axis=-1, keepdims=True)
    scale = lax.rsqrt(ssq / d + EPS)
    for j, z in enumerate(pieces):
        out_ref[:, j * LANES:(j + 1) * LANES] = z * scale * gfin_ref[:, j * LANES:(j + 1) * LANES]


def _combine_call(x1, gates_col, gfin, y_tok, nsub):
    n_tok, d = x1.shape
    tc = min(COMBINE_TILE, n_tok)
    assert n_tok % tc == 0
    l = y_tok.shape[-1]
    return pl.pallas_call(
        functools.partial(_combine_kernel, nsub=nsub),
        grid=(n_tok // tc,),
        in_specs=[
            pl.BlockSpec((tc, d), lambda i: (i, 0)),
            pl.BlockSpec((tc, LANES), lambda i: (i, 0)),
            pl.BlockSpec((1, d), lambda i: (0, 0)),
            pl.BlockSpec((TOP_K, tc * nsub, l), lambda i: (0, i, 0)),
        ],
        out_specs=pl.BlockSpec((tc, d), lambda i: (i, 0)),
        out_shape=jax.ShapeDtypeStruct((n_tok, d), jnp.float32),
        compiler_params=pltpu.CompilerParams(
            dimension_semantics=("arbitrary",), vmem_limit_bytes=VMEM_LIMIT),
        name="combine",
    )(x1, gates_col, gfin, y_tok)


def _prepare_layer(l, norm_mix_g, w_in, b_in, conv_w, conv_b, conv_ln_g, conv_ln_b, qk_conv_w,
                   qk_conv_b, mlstm_norm_g, w_out, norm_ffn_g, router_group_w, router_group_b,
                   router_expert_w, router_expert_b, ts):
    f32, bf16 = jnp.float32, jnp.bfloat16
    d = w_in.shape[1]
    dc = conv_w.shape[-1]
    dm = mlstm_norm_g.shape[-1]
    n_heads = (w_in.shape[-1] - 2 * dc - 4 * dm) // 2
    n_groups = router_group_w.shape[-1]
    n_experts = router_expert_w.shape[-1]
    wi, bi = w_in[l], b_in[l]
    o_qk, o_vo, o_if = 2 * dc, 2 * dc + 2 * dm, 2 * dc + 4 * dm
    w_if = wi[:, o_if:]
    b_if = bi[o_if:]
    wr_t = jnp.zeros((SUBLANES + n_experts, d), f32)
    wr_t = wr_t.at[:n_groups].set(router_group_w[l].T).at[SUBLANES:].set(router_expert_w[l].T)
    br = jnp.zeros((SUBLANES + n_experts, 1), f32)
    br = br.at[:n_groups, 0].set(router_group_b[l]).at[SUBLANES:, 0].set(router_expert_b[l])
    wr_hi = wr_t.astype(bf16)
    wr_lo = (wr_t - wr_hi.astype(f32)).astype(bf16)
    pad_rows = lambda a, n: jnp.pad(a, ((0, n - a.shape[0]), (0, 0)))
    idx = jnp.arange(ts)
    cidx = jnp.arange(CHUNK)
    return dict(
        gmix=norm_mix_g[l][None, :],
        wcv=wi[:, :o_qk].astype(bf16), bcv=bi[None, :o_qk],
        wqk=wi[:, o_qk:o_vo].astype(bf16), bqk=bi[None, o_qk:o_vo],
        wvo=wi[:, o_vo:o_if].astype(bf16), bvo=bi[None, o_vo:o_if],
        wif=jnp.pad(w_if, ((0, 0), (0, LANES - 2 * n_heads))).astype(bf16),
        bif=jnp.pad(b_if, (0, LANES - 2 * n_heads))[None, :],
        wift=pad_rows(w_if.T, SUBLANES).astype(bf16),
        bift=pad_rows(b_if[:, None], SUBLANES),
        cw=jnp.repeat(conv_w[l], SUBLANES, axis=0), cb=conv_b[l][None, :],
        lng=conv_ln_g[l][None, :], lnb=conv_ln_b[l][None, :],
        qkw=jnp.repeat(qk_conv_w[l], SUBLANES, axis=0), qkb=qk_conv_b[l][None, :],
        mng=mlstm_norm_g[l][None, :],
        wout=w_out[l].astype(bf16), gffn=norm_ffn_g[l][None, :],
        wrh=wr_hi, wrl=wr_lo, br=br,
        su=(idx[:, None] < idx[None, :]).astype(bf16),
        tril=(cidx[:, None] >= cidx[None, :]).astype(bf16),
    ), dict(n_heads=n_heads, conv_width=conv_w.shape[1], qk_width=qk_conv_w.shape[1],
            n_groups=n_groups, epg=n_experts // n_groups)


def kernel(x, norm_mix_g, w_in, b_in, conv_w, conv_b, conv_ln_g, conv_ln_b, qk_conv_w, qk_conv_b,
           mlstm_norm_g, w_out, norm_ffn_g, router_group_w, router_group_b, router_expert_w,
           router_expert_b, expert_w_gate, expert_w_up, expert_w_down, final_norm_g):
    bsz, seq, d = x.shape
    n_tok = bsz * seq
    depth = w_in.shape[0]
    assert depth == 1, "the combine kernel fuses the final norm, so exactly one layer is supported"
    n_experts = router_expert_w.shape[-1]
    n_assign = n_tok * TOP_K
    n_blocks = -(-n_assign // EXPERT_BLOCK) + n_experts
    n_slots = n_blocks * EXPERT_BLOCK
    ts = min(SEQ_TILE, seq)
    out = None
    for l in range(depth):
        params, dims = _prepare_layer(
            l, norm_mix_g, w_in, b_in, conv_w, conv_b, conv_ln_g, conv_ln_b, qk_conv_w, qk_conv_b,
            mlstm_norm_g, w_out, norm_ffn_g, router_group_w, router_group_b, router_expert_w,
            router_expert_b, ts)
        x1, h2_rows, route_i, gates_col, counts = _mixer_call(x, params, **dims)

        cnt = counts[:, 0].astype(jnp.int32)
        padded = (cnt + EXPERT_BLOCK - 1) // EXPERT_BLOCK * EXPERT_BLOCK
        padded_ends = jnp.cumsum(padded)
        padded_starts = padded_ends - padded
        route = jnp.transpose(route_i, (2, 0, 1, 3)).reshape(SUBLANES, n_tok)
        dest = jnp.stack([padded_starts[route[0]] + route[2],
                          padded_starts[route[1]] + route[3]], axis=0)
        block_start = jnp.arange(n_blocks, dtype=jnp.int32) * EXPERT_BLOCK
        block_expert = jnp.minimum(
            jnp.sum(block_start[:, None] >= padded_ends[None, :], axis=1), n_experts - 1).astype(jnp.int32)
        block_valid = jnp.clip((padded_starts + cnt)[block_expert] - block_start,
                               0, EXPERT_BLOCK).astype(jnp.int32)
        n_used = (padded_ends[-1:] // EXPERT_BLOCK).astype(jnp.int32)

        nsub = d // LANES
        dest_w = _worker_index_table(dest, _sc_workers()[1])
        xs = _sc_dispatch(dest_w, h2_rows.reshape(n_tok, nsub, LANES), n_slots)
        ys = _expert_call(block_expert, block_valid, n_used, xs.reshape(n_slots * nsub, LANES),
                          expert_w_gate[l].astype(jnp.bfloat16), expert_w_up[l].astype(jnp.bfloat16),
                          expert_w_down[l].astype(jnp.bfloat16), nsub)
        y_tok = _sc_gather(dest_w, ys.reshape(n_slots, nsub, LANES), n_tok)
        out = _combine_call(x1.reshape(n_tok, d), gates_col.reshape(n_tok, LANES),
                            final_norm_g[None, :], y_tok.reshape(TOP_K, n_tok * nsub, LANES), nsub)
        x = out.reshape(bsz, seq, d)
    return x
```

```python
import functools

import jax
import jax.numpy as jnp
from jax import lax
from jax.experimental import pallas as pl
from jax.experimental.pallas import tpu as pltpu
from jax.experimental.pallas import tpu_sc as plsc

EPS = 1e-6
LANES = 128
SUBLANES = 8
CHUNK = 128
TOP_K = 2
SEQ_TILE = 512
CONV_ROWS = 32
CONV_HALO = 32
QK_HALO = 8
EXPERT_BLOCK = 256
SC_CHUNK = 32
COMBINE_TILE = 512
N_PARTS = 2
VMEM_LIMIT = 56 * 1024 * 1024

_NT = (((1,), (1,)), ((), ()))
_TN = (((0,), (0,)), ((), ()))


def _dot(a, b):
    return jnp.dot(a, b, preferred_element_type=jnp.float32)


def _silu(x):
    return x * jax.nn.sigmoid(x)


def _log_sigmoid(x):
    return jnp.minimum(x, 0.0) - jnp.log1p(jnp.exp(-jnp.abs(x)))


def _split_bf16(x, parts):
    out = []
    for _ in range(parts):
        p = x.astype(jnp.bfloat16)
        out.append(p)
        x = x - p.astype(jnp.float32)
    return out


def _mixer_kernel(x_ref, gmix_ref, wcv_ref, bcv_ref, wqk_ref, bqk_ref, wvo_ref, bvo_ref,
                  wif_ref, bif_ref, wift_ref, bift_ref, cw_ref, cb_ref, lng_ref, lnb_ref,
                  qkw_ref, qkb_ref, mng_ref, wout_ref, gffn_ref, wrh_ref, wrl_ref, br_ref,
                  su_ref, tril_ref,
                  x1_ref, h2_ref, ri_ref, gc_ref, cnt_ref,
                  uext, qkext, q_s, k_s, v_s, o_s, cn_s, m_s, y_s, cnt_s,
                  *, n_heads, conv_width, qk_width, n_groups, epg):
    ts, d = x_ref.shape
    dc = cw_ref.shape[1]
    dm = mng_ref.shape[1]
    dh = dm // n_heads
    n_chunks = ts // CHUNK
    n_experts = n_groups * epg
    b_idx = pl.program_id(0)
    t_idx = pl.program_id(1)

    @pl.when(t_idx == 0)
    def _reset_sequence_state():
        uext[0:CONV_HALO, :] = jnp.zeros((CONV_HALO, dc), jnp.float32)
        uext[CONV_HALO + ts:, :] = jnp.zeros((SUBLANES, dc), jnp.float32)
        qkext[0:QK_HALO, :] = jnp.zeros((QK_HALO, 2 * dm), jnp.float32)
        cn_s[...] = jnp.zeros(cn_s.shape, jnp.float32)
        m_s[...] = jnp.zeros(m_s.shape, jnp.float32)

    @pl.when((t_idx == 0) & (b_idx == 0))
    def _reset_counts():
        cnt_s[...] = jnp.zeros(cnt_s.shape, jnp.float32)

    x = x_ref[...]
    h = x * lax.rsqrt(jnp.mean(x * x, axis=-1, keepdims=True) + EPS) * gmix_ref[...]
    hb = h.astype(jnp.bfloat16)

    zc = _dot(hb, wcv_ref[...]) + bcv_ref[...]
    uext[CONV_HALO:CONV_HALO + ts, :] = zc[:, :dc] * jax.nn.sigmoid(zc[:, dc:])
    first = CONV_HALO - (conv_width - 1)
    sub = lax.broadcasted_iota(jnp.int32, (SUBLANES, dc), 0)
    n_grp = CONV_ROWS // SUBLANES
    for rb in range(ts // CONV_ROWS):
        r0 = rb * CONV_ROWS
        acc = jnp.zeros((CONV_ROWS, dc), jnp.float32) + cb_ref[...]
        for r in range(SUBLANES):
            part = None
            for j in range(conv_width):
                if (first + j) % SUBLANES != r:
                    continue
                a0 = r0 + (first + j) - r
                w8 = cw_ref[j * SUBLANES:(j + 1) * SUBLANES, :]
                term = jnp.concatenate([w8] * (n_grp + 1), axis=0) * uext[a0:a0 + CONV_ROWS + SUBLANES, :]
                part = term if part is None else part + term
            if part is None:
                continue
            if r == 0:
                acc = acc + part[:CONV_ROWS, :]
                continue
            rolled = [pltpu.roll(part[g * SUBLANES:(g + 1) * SUBLANES, :], SUBLANES - r, axis=0)
                      for g in range(n_grp + 1)]
            acc = acc + jnp.concatenate(
                [jnp.where(sub < SUBLANES - r, rolled[g], rolled[g + 1]) for g in range(n_grp)], axis=0)
        mu = jnp.mean(acc, axis=-1, keepdims=True)
        cen = acc - mu
        var = jnp.mean(cen * cen, axis=-1, keepdims=True)
        yn = cen * lax.rsqrt(var + EPS) * lng_ref[...] + lnb_ref[...]
        y_s[r0:r0 + CONV_ROWS, 0:dc] = _silu(yn).astype(jnp.bfloat16)
    uext[0:CONV_HALO, :] = uext[ts:ts + CONV_HALO, :]

    qkext[QK_HALO:QK_HALO + ts, :] = _dot(hb, wqk_ref[...]) + bqk_ref[...]
    qfirst = QK_HALO - (qk_width - 1)
    k_scale = dh ** -0.5
    for rb in range(ts // CONV_ROWS):
        r0 = rb * CONV_ROWS
        acc = jnp.zeros((CONV_ROWS, 2 * dm), jnp.float32) + qkb_ref[...]
        for j in range(qk_width):
            w8 = qkw_ref[j * SUBLANES:(j + 1) * SUBLANES, :]
            acc = acc + (jnp.concatenate([w8] * (CONV_ROWS // SUBLANES), axis=0)
                         * qkext[r0 + qfirst + j:r0 + qfirst + j + CONV_ROWS, :])
        act = _silu(acc)
        q_s[r0:r0 + CONV_ROWS, :] = act[:, :dm].astype(jnp.bfloat16)
        k_s[r0:r0 + CONV_ROWS, :] = (act[:, dm:] * k_scale).astype(jnp.bfloat16)
    qkext[0:QK_HALO, :] = qkext[ts:ts + QK_HALO, :]

    zvo = _dot(hb, wvo_ref[...]) + bvo_ref[...]
    v_s[...] = zvo[:, :dm].astype(jnp.bfloat16)
    o_s[...] = jax.nn.sigmoid(zvo[:, dm:])

    zif_col = _dot(hb, wif_ref[...]) + bif_ref[...]
    zif_row = lax.dot_general(wift_ref[...], hb, _NT,
                              preferred_element_type=jnp.float32) + bift_ref[...]
    logf_col = _log_sigmoid(zif_col)
    logf_row = _log_sigmoid(zif_row)
    lane = lax.broadcasted_iota(jnp.int32, (SUBLANES, CHUNK), 1)
    causal = (lax.broadcasted_iota(jnp.int32, (CHUNK, CHUNK), 0)
              >= lax.broadcasted_iota(jnp.int32, (CHUNK, CHUNK), 1))
    ones_blk = jnp.ones((CHUNK, dh), jnp.bfloat16)
    tril = tril_ref[...]

    for c in range(n_chunks):
        r0 = c * CHUNK
        b_row = logf_row[:, r0:r0 + CHUNK]
        shift = 1
        while shift < CHUNK:
            b_row = b_row + jnp.where(lane >= shift, pltpu.roll(b_row, shift, axis=1), 0.0)
            shift *= 2
        b_col = sum(_dot(tril, p) for p in _split_bf16(logf_col[r0:r0 + CHUNK, :], 3))
        i_col_all = zif_col[r0:r0 + CHUNK, :]
        i_row_all = zif_row[:, r0:r0 + CHUNK]
        for hd in range(n_heads):
            c0 = hd * dh
            b_c = b_col[:, n_heads + hd:n_heads + hd + 1]
            i_c = i_col_all[:, hd:hd + 1]
            b_r = b_row[n_heads + hd:n_heads + hd + 1, :]
            i_r = i_row_all[hd:hd + 1, :]
            m_prev = m_s[hd, 0:1, 0:1]
            q = q_s[r0:r0 + CHUNK, c0:c0 + dh]
            k = k_s[r0:r0 + CHUNK, c0:c0 + dh]
            vaug = jnp.concatenate([v_s[r0:r0 + CHUNK, c0:c0 + dh], ones_blk], axis=-1)
            cn = cn_s[hd]

            dmat = jnp.where(causal, b_c + (i_r - b_r), -jnp.inf)
            g = b_c + m_prev
            m_t = jnp.maximum(g, jnp.max(dmat, axis=-1, keepdims=True))
            w_intra = jnp.exp(dmat - m_t)
            w_inter = jnp.exp(g - m_t)
            s = lax.dot_general(q, k, _NT, preferred_element_type=jnp.float32) * w_intra
            nd = _dot(s.astype(jnp.bfloat16), vaug) + w_inter * _dot(q, cn.astype(jnp.bfloat16))
            hval = nd[:, :dh] / jnp.maximum(jnp.abs(nd[:, dh:]), jnp.exp(-m_t))

            b_last = b_r[:, CHUNK - 1:CHUNK]
            logw_c = b_last - b_c + i_c
            m_new = jnp.maximum(b_last + m_prev, jnp.max(logw_c, axis=0, keepdims=True))
            decay = jnp.exp(b_last + m_prev - m_new)
            kw = (k.astype(jnp.float32) * jnp.exp(logw_c - m_new)).astype(jnp.bfloat16)
            cn_s[hd] = decay * cn + lax.dot_general(kw, vaug, _TN,
                                                    preferred_element_type=jnp.float32)
            m_s[hd] = jnp.broadcast_to(m_new, m_s.shape[1:])

            hm = o_s[r0:r0 + CHUNK, c0:c0 + dh] * hval
            mu = jnp.mean(hm, axis=-1, keepdims=True)
            cen = hm - mu
            var = jnp.mean(cen * cen, axis=-1, keepdims=True)
            hn = cen * lax.rsqrt(var + EPS) * mng_ref[:, c0:c0 + dh]
            y_s[r0:r0 + CHUNK, dc + c0:dc + c0 + dh] = hn.astype(jnp.bfloat16)

    x1 = x + _dot(y_s[...], wout_ref[...])
    x1_ref[...] = x1
    h2 = x1 * lax.rsqrt(jnp.mean(x1 * x1, axis=-1, keepdims=True) + EPS) * gffn_ref[...]
    nsub = d // LANES
    for sblk in range(nsub):
        h2_ref[pl.ds(sblk, ts, stride=nsub), :] = h2[:, sblk * LANES:(sblk + 1) * LANES]

    h2_hi, h2_lo = _split_bf16(h2, 2)
    lg = (lax.dot_general(wrh_ref[...], h2_hi, _NT, preferred_element_type=jnp.float32)
          + lax.dot_general(wrl_ref[...], h2_hi, _NT, preferred_element_type=jnp.float32)
          + lax.dot_general(wrh_ref[...], h2_lo, _NT, preferred_element_type=jnp.float32)
          + br_ref[...])
    gl = lg[0:n_groups, :]
    gidx = lax.broadcasted_iota(jnp.int32, (n_groups, ts), 0)
    gmax = jnp.max(gl, axis=0, keepdims=True)
    grp = jnp.min(jnp.where(gl == gmax, gidx, n_groups), axis=0, keepdims=True)
    p_grp = 1.0 / jnp.sum(jnp.exp(gl - gmax), axis=0, keepdims=True)
    in_group = jnp.zeros((epg, ts), jnp.float32)
    for gi in range(n_groups):
        in_group = jnp.where(grp == gi, lg[SUBLANES + gi * epg:SUBLANES + (gi + 1) * epg, :], in_group)
    eidx = lax.broadcasted_iota(jnp.int32, (epg, ts), 0)
    v1 = jnp.max(in_group, axis=0, keepdims=True)
    i1 = jnp.min(jnp.where(in_group == v1, eidx, epg), axis=0, keepdims=True)
    rest = jnp.where(eidx == i1, -jnp.inf, in_group)
    v2 = jnp.max(rest, axis=0, keepdims=True)
    i2 = jnp.min(jnp.where(rest == v2, eidx, epg), axis=0, keepdims=True)
    e21 = jnp.exp(v2 - v1)
    g1 = p_grp / (1.0 + e21)
    g2 = g1 * e21
    e1 = grp * epg + i1
    e2 = grp * epg + i2

    xidx = lax.broadcasted_iota(jnp.int32, (n_experts, ts), 0)
    sel1 = xidx == e1
    sel2 = xidx == e2
    onehot = jnp.where(sel1 | sel2, 1.0, 0.0).astype(jnp.bfloat16)
    before = cnt_s[:, 0:1] + _dot(onehot, su_ref[...])
    rank1 = jnp.sum(jnp.where(sel1, before, 0.0), axis=0, keepdims=True)
    rank2 = jnp.sum(jnp.where(sel2, before, 0.0), axis=0, keepdims=True)
    cnt_new = cnt_s[...] + jnp.sum(onehot.astype(jnp.float32), axis=1, keepdims=True)
    cnt_s[...] = cnt_new
    cnt_ref[...] = cnt_new

    zero_i = jnp.zeros((SUBLANES - 4, ts), jnp.int32)
    ri_ref[...] = jnp.concatenate(
        [e1, e2, rank1.astype(jnp.int32), rank2.astype(jnp.int32), zero_i], axis=0)
    gates_row = jnp.concatenate([g1, g2, jnp.zeros((LANES - 2, ts), jnp.float32)], axis=0)
    gc_ref[...] = gates_row.T


def _mixer_call(x, p, b0, bsz, *, n_heads, conv_width, qk_width, n_groups, epg):
    _, seq, d = x.shape
    ts = min(SEQ_TILE, seq)
    nt = seq // ts
    dc = p["cw"].shape[1]
    dm = p["mng"].shape[1]
    dh = dm // n_heads
    n_experts = n_groups * epg
    assert seq % ts == 0 and ts % CHUNK == 0 and dh == LANES and d % LANES == 0
    assert epg == SUBLANES and n_groups <= SUBLANES and conv_width - 1 <= CONV_HALO

    consts = [p[n] for n in ("gmix", "wcv", "bcv", "wqk", "bqk", "wvo", "bvo", "wif", "bif",
                             "wift", "bift", "cw", "cb", "lng", "lnb", "qkw", "qkb", "mng",
                             "wout", "gffn", "wrh", "wrl", "br", "su", "tril")]

    def const_spec(a):
        return pl.BlockSpec(a.shape, lambda b, t: (0,) * a.ndim)

    kern = functools.partial(_mixer_kernel, n_heads=n_heads, conv_width=conv_width,
                             qk_width=qk_width, n_groups=n_groups, epg=epg)
    return pl.pallas_call(
        kern,
        grid=(bsz, nt),
        in_specs=([pl.BlockSpec((None, ts, d), lambda b, t: (b + b0, t, 0))]
                  + [const_spec(a) for a in consts]),
        out_specs=[
            pl.BlockSpec((None, ts, d), lambda b, t: (b, t, 0)),
            pl.BlockSpec((None, ts * (d // LANES), LANES), lambda b, t: (b, t, 0)),
            pl.BlockSpec((None, None, SUBLANES, ts), lambda b, t: (b, t, 0, 0)),
            pl.BlockSpec((None, ts, LANES), lambda b, t: (b, t, 0)),
            pl.BlockSpec((n_experts, LANES), lambda b, t: (0, 0)),
        ],
        out_shape=[
            jax.ShapeDtypeStruct((bsz, seq, d), jnp.float32),
            jax.ShapeDtypeStruct((bsz, seq * (d // LANES), LANES), jnp.float32),
            jax.ShapeDtypeStruct((bsz, nt, SUBLANES, ts), jnp.int32),
            jax.ShapeDtypeStruct((bsz, seq, LANES), jnp.float32),
            jax.ShapeDtypeStruct((n_experts, LANES), jnp.float32),
        ],
        scratch_shapes=[
            pltpu.VMEM((CONV_HALO + ts + SUBLANES, dc), jnp.float32),
            pltpu.VMEM((QK_HALO + ts, 2 * dm), jnp.float32),
            pltpu.VMEM((ts, dm), jnp.bfloat16),
            pltpu.VMEM((ts, dm), jnp.bfloat16),
            pltpu.VMEM((ts, dm), jnp.bfloat16),
            pltpu.VMEM((ts, dm), jnp.float32),
            pltpu.VMEM((n_heads, dh, 2 * dh), jnp.float32),
            pltpu.VMEM((n_heads, SUBLANES, LANES), jnp.float32),
            pltpu.VMEM((ts, d), jnp.bfloat16),
            pltpu.VMEM((n_experts, LANES), jnp.float32),
        ],
        compiler_params=pltpu.CompilerParams(
            dimension_semantics=("arbitrary", "arbitrary"), vmem_limit_bytes=VMEM_LIMIT),
        name="mixer_router",
    )(x, *consts)


def _sc_workers():
    info = plsc.get_sparse_core_info()
    return info.num_cores, info.num_cores * info.num_subcores


def _worker_index_table(dest, n_workers):
    top_k, n_tok = dest.shape
    assert n_tok % (n_workers * SC_CHUNK) == 0
    return jnp.transpose(dest.reshape(top_k, n_workers, n_tok // (n_workers * SC_CHUNK), SC_CHUNK),
                         (1, 0, 2, 3))


def _sc_dispatch(dest_w, rows, n_slots):
    n_tok, nsub, l = rows.shape
    nw, top_k, n_chunks, ch = dest_w.shape
    tpw = n_tok // nw
    n_cores, n_workers = _sc_workers()
    assert nw == n_workers and tpw == n_chunks * ch

    @functools.partial(
        pl.kernel,
        mesh=plsc.VectorSubcoreMesh(core_axis_name="c", subcore_axis_name="s"),
        out_type=jax.ShapeDtypeStruct((n_slots, nsub, l), rows.dtype),
        scratch_types=[pltpu.VMEM((top_k, n_chunks, ch), jnp.int32),
                       pltpu.VMEM((ch, nsub, l), rows.dtype),
                       pltpu.SemaphoreType.DMA],
        name="sc_dispatch",
    )
    def run(dest_hbm, rows_hbm, out_hbm, idx_v, buf, sem):
        wid = lax.axis_index("s") * n_cores + lax.axis_index("c")
        pltpu.sync_copy(dest_hbm.at[wid], idx_v)

        @pl.loop(0, n_chunks)
        def _(j):
            pltpu.sync_copy(rows_hbm.at[pl.ds(wid * tpw + j * ch, ch)], buf)
            for k in range(top_k):
                pltpu.async_copy(buf, out_hbm.at[idx_v.at[k, j]], sem).wait()

    return run(dest_w, rows)


def _sc_gather(dest_w, ys, n_tok):
    _, nsub, l = ys.shape
    nw, top_k, n_chunks, ch = dest_w.shape
    tpw = n_tok // nw
    n_cores, n_workers = _sc_workers()
    assert nw == n_workers and tpw == n_chunks * ch

    @functools.partial(
        pl.kernel,
        mesh=plsc.VectorSubcoreMesh(core_axis_name="c", subcore_axis_name="s"),
        out_type=jax.ShapeDtypeStruct((top_k, n_tok, nsub, l), ys.dtype),
        scratch_types=[pltpu.VMEM((top_k, n_chunks, ch), jnp.int32),
                       pltpu.VMEM((ch, nsub, l), ys.dtype),
                       pltpu.SemaphoreType.DMA],
        name="sc_gather",
    )
    def run(dest_hbm, ys_hbm, out_hbm, idx_v, buf, sem):
        wid = lax.axis_index("s") * n_cores + lax.axis_index("c")
        pltpu.sync_copy(dest_hbm.at[wid], idx_v)

        @pl.loop(0, n_chunks)
        def _(j):
            for k in range(top_k):
                pltpu.async_copy(ys_hbm.at[idx_v.at[k, j]], buf, sem).wait()
                pltpu.sync_copy(buf, out_hbm.at[k, pl.ds(wid * tpw + j * ch, ch)])

    return run(dest_w, ys)


def _expert_kernel(be_ref, valid_ref, nused_ref, xs_ref, wg_ref, wu_ref, wd_ref, ys_ref, *, nsub):
    i = pl.program_id(0)
    blk = xs_ref.shape[0] // nsub
    l = xs_ref.shape[1]

    @pl.when(i < nused_ref[0])
    def _compute():
        live = lax.broadcasted_iota(jnp.int32, (blk, l), 0) < valid_ref[i]
        xb = jnp.concatenate(
            [jnp.where(live, xs_ref[pl.ds(j, blk, stride=nsub), :], 0.0).astype(jnp.bfloat16)
             for j in range(nsub)], axis=-1)
        a = _dot(xb, wg_ref[...])
        u = _dot(xb, wu_ref[...])
        y = _dot((_silu(a) * u).astype(jnp.bfloat16), wd_ref[...])
        for j in range(nsub):
            ys_ref[pl.ds(j, blk, stride=nsub), :] = y[:, j * l:(j + 1) * l]

    @pl.when(i >= nused_ref[0])
    def _unused_block():
        ys_ref[...] = jnp.zeros(ys_ref.shape, ys_ref.dtype)


def _expert_call(block_expert, block_valid, n_used, xs, wg, wu, wd, nsub):
    rows, l = xs.shape
    n_blocks = rows // (EXPERT_BLOCK * nsub)
    _, d, de = wg.shape

    def row_map(i, be, bv, nu):
        return (i, 0)

    def w_map(i, be, bv, nu):
        return (be[i], 0, 0)

    return pl.pallas_call(
        functools.partial(_expert_kernel, nsub=nsub),
        grid_spec=pltpu.PrefetchScalarGridSpec(
            num_scalar_prefetch=3,
            grid=(n_blocks,),
            in_specs=[
                pl.BlockSpec((EXPERT_BLOCK * nsub, l), row_map),
                pl.BlockSpec((None, d, de), w_map),
                pl.BlockSpec((None, d, de), w_map),
                pl.BlockSpec((None, de, d), w_map),
            ],
            out_specs=pl.BlockSpec((EXPERT_BLOCK * nsub, l), row_map),
        ),
        out_shape=jax.ShapeDtypeStruct((rows, l), jnp.float32),
        compiler_params=pltpu.CompilerParams(
            dimension_semantics=("arbitrary",), vmem_limit_bytes=VMEM_LIMIT),
        name="experts",
    )(block_expert, block_valid, n_used, xs, wg, wu, wd)


def _combine_kernel(x1_ref, gc_ref, gfin_ref, y_ref, out_ref, *, nsub):
    tc, d = x1_ref.shape
    g1 = gc_ref[:, 0:1]
    g2 = gc_ref[:, 1:2]
    pieces = []
    ssq = jnp.zeros((tc, 1), jnp.float32)
    for j in range(nsub):
        y1 = y_ref[0, pl.ds(j, tc, stride=nsub), :]
        y2 = y_ref[1, pl.ds(j, tc, stride=nsub), :]
        z = x1_ref[:, j * LANES:(j + 1) * LANES] + (g1 * y1 + g2 * y2)
        pieces.append(z)
        ssq = ssq + jnp.sum(z * z, axis=-1, keepdims=True)
    scale = lax.rsqrt(ssq / d + EPS)
    for j, z in enumerate(pieces):
        out_ref[:, j * LANES:(j + 1) * LANES] = z * scale * gfin_ref[:, j * LANES:(j + 1) * LANES]


def _combine_call(x1, gates_col, gfin, y_tok, nsub, out_prev, tok0, n_tok_total):
    n_tok, d = x1.shape
    tc = min(COMBINE_TILE, n_tok)
    assert n_tok % tc == 0 and tok0 % tc == 0
    l = y_tok.shape[-1]
    in_specs = [
        pl.BlockSpec((tc, d), lambda i: (i, 0)),
        pl.BlockSpec((tc, LANES), lambda i: (i, 0)),
        pl.BlockSpec((1, d), lambda i: (0, 0)),
        pl.BlockSpec((TOP_K, tc * nsub, l), lambda i: (0, i, 0)),
    ]
    args = [x1, gates_col, gfin, y_tok]
    aliases = {}
    kern = functools.partial(_combine_kernel, nsub=nsub)
    if out_prev is not None:
        in_specs.append(pl.BlockSpec(memory_space=pl.ANY))
        args.append(out_prev)
        aliases = {len(args) - 1: 0}
        kern = lambda x1_ref, gc_ref, gfin_ref, y_ref, prev_ref, out_ref: _combine_kernel(
            x1_ref, gc_ref, gfin_ref, y_ref, out_ref, nsub=nsub)
    return pl.pallas_call(
        kern,
        grid=(n_tok // tc,),
        in_specs=in_specs,
        out_specs=pl.BlockSpec((tc, d), lambda i: (i + tok0 // tc, 0)),
        out_shape=jax.ShapeDtypeStruct((n_tok_total, d), jnp.float32),
        input_output_aliases=aliases,
        compiler_params=pltpu.CompilerParams(
            dimension_semantics=("arbitrary",), vmem_limit_bytes=VMEM_LIMIT),
        name="combine",
    )(*args)


def _prepare_layer(l, norm_mix_g, w_in, b_in, conv_w, conv_b, conv_ln_g, conv_ln_b, qk_conv_w,
                   qk_conv_b, mlstm_norm_g, w_out, norm_ffn_g, router_group_w, router_group_b,
                   router_expert_w, router_expert_b, ts):
    f32, bf16 = jnp.float32, jnp.bfloat16
    d = w_in.shape[1]
    dc = conv_w.shape[-1]
    dm = mlstm_norm_g.shape[-1]
    n_heads = (w_in.shape[-1] - 2 * dc - 4 * dm) // 2
    n_groups = router_group_w.shape[-1]
    n_experts = router_expert_w.shape[-1]
    wi, bi = w_in[l], b_in[l]
    o_qk, o_vo, o_if = 2 * dc, 2 * dc + 2 * dm, 2 * dc + 4 * dm
    w_if = wi[:, o_if:]
    b_if = bi[o_if:]
    wr_t = jnp.zeros((SUBLANES + n_experts, d), f32)
    wr_t = wr_t.at[:n_groups].set(router_group_w[l].T).at[SUBLANES:].set(router_expert_w[l].T)
    br = jnp.zeros((SUBLANES + n_experts, 1), f32)
    br = br.at[:n_groups, 0].set(router_group_b[l]).at[SUBLANES:, 0].set(router_expert_b[l])
    wr_hi = wr_t.astype(bf16)
    wr_lo = (wr_t - wr_hi.astype(f32)).astype(bf16)
    pad_rows = lambda a, n: jnp.pad(a, ((0, n - a.shape[0]), (0, 0)))
    idx = jnp.arange(ts)
    cidx = jnp.arange(CHUNK)
    return dict(
        gmix=norm_mix_g[l][None, :],
        wcv=wi[:, :o_qk].astype(bf16), bcv=bi[None, :o_qk],
        wqk=wi[:, o_qk:o_vo].astype(bf16), bqk=bi[None, o_qk:o_vo],
        wvo=wi[:, o_vo:o_if].astype(bf16), bvo=bi[None, o_vo:o_if],
        wif=jnp.pad(w_if, ((0, 0), (0, LANES - 2 * n_heads))).astype(bf16),
        bif=jnp.pad(b_if, (0, LANES - 2 * n_heads))[None, :],
        wift=pad_rows(w_if.T, SUBLANES).astype(bf16),
        bift=pad_rows(b_if[:, None], SUBLANES),
        cw=jnp.repeat(conv_w[l], SUBLANES, axis=0), cb=conv_b[l][None, :],
        lng=conv_ln_g[l][None, :], lnb=conv_ln_b[l][None, :],
        qkw=jnp.repeat(qk_conv_w[l], SUBLANES, axis=0), qkb=qk_conv_b[l][None, :],
        mng=mlstm_norm_g[l][None, :],
        wout=w_out[l].astype(bf16), gffn=norm_ffn_g[l][None, :],
        wrh=wr_hi, wrl=wr_lo, br=br,
        su=(idx[:, None] < idx[None, :]).astype(bf16),
        tril=(cidx[:, None] >= cidx[None, :]).astype(bf16),
    ), dict(n_heads=n_heads, conv_width=conv_w.shape[1], qk_width=qk_conv_w.shape[1],
            n_groups=n_groups, epg=n_experts // n_groups)


def kernel(x, norm_mix_g, w_in, b_in, conv_w, conv_b, conv_ln_g, conv_ln_b, qk_conv_w, qk_conv_b,
           mlstm_norm_g, w_out, norm_ffn_g, router_group_w, router_group_b, router_expert_w,
           router_expert_b, expert_w_gate, expert_w_up, expert_w_down, final_norm_g):
    bsz, seq, d = x.shape
    n_tok = bsz * seq
    depth = w_in.shape[0]
    assert depth == 1, "the combine kernel fuses the final norm, so exactly one layer is supported"
    n_experts = router_expert_w.shape[-1]
    n_parts = N_PARTS if bsz % N_PARTS == 0 else 1
    pb = bsz // n_parts
    n_tok = pb * seq
    n_assign = n_tok * TOP_K
    n_blocks = -(-n_assign // EXPERT_BLOCK) + n_experts
    n_slots = n_blocks * EXPERT_BLOCK
    ts = min(SEQ_TILE, seq)
    nsub = d // LANES
    l = 0
    params, dims = _prepare_layer(
        l, norm_mix_g, w_in, b_in, conv_w, conv_b, conv_ln_g, conv_ln_b, qk_conv_w, qk_conv_b,
        mlstm_norm_g, w_out, norm_ffn_g, router_group_w, router_group_b, router_expert_w,
        router_expert_b, ts)
    wg, wu, wd = (w[l].astype(jnp.bfloat16) for w in (expert_w_gate, expert_w_up, expert_w_down))
    out = None
    for part in range(n_parts):
        x1, h2_rows, route_i, gates_col, counts = _mixer_call(x, params, part * pb, pb, **dims)

        cnt = counts[:, 0].astype(jnp.int32)
        padded = (cnt + EXPERT_BLOCK - 1) // EXPERT_BLOCK * EXPERT_BLOCK
        padded_ends = jnp.cumsum(padded)
        padded_starts = padded_ends - padded
        route = jnp.transpose(route_i, (2, 0, 1, 3)).reshape(SUBLANES, n_tok)
        dest = jnp.stack([padded_starts[route[0]] + route[2],
                          padded_starts[route[1]] + route[3]], axis=0)
        block_start = jnp.arange(n_blocks, dtype=jnp.int32) * EXPERT_BLOCK
        block_expert = jnp.minimum(
            jnp.sum(block_start[:, None] >= padded_ends[None, :], axis=1), n_experts - 1).astype(jnp.int32)
        block_valid = jnp.clip((padded_starts + cnt)[block_expert] - block_start,
                               0, EXPERT_BLOCK).astype(jnp.int32)
        n_used = (padded_ends[-1:] // EXPERT_BLOCK).astype(jnp.int32)

        dest_w = _worker_index_table(dest, _sc_workers()[1])
        xs = _sc_dispatch(dest_w, h2_rows.reshape(n_tok, nsub, LANES), n_slots)
        ys = _expert_call(block_expert, block_valid, n_used, xs.reshape(n_slots * nsub, LANES),
                          wg, wu, wd, nsub)
        y_tok = _sc_gather(dest_w, ys.reshape(n_slots, nsub, LANES), n_tok)
        out = _combine_call(x1.reshape(n_tok, d), gates_col.reshape(n_tok, LANES),
                            final_norm_g[None, :], y_tok.reshape(TOP_K, n_tok * nsub, LANES), nsub,
                            out, part * n_tok, bsz * seq)
    return out.reshape(bsz, seq, d)
```

```python
import functools

import jax
import jax.numpy as jnp
from jax import lax
from jax.experimental import pallas as pl
from jax.experimental.pallas import tpu as pltpu
from jax.experimental.pallas import tpu_sc as plsc

EPS = 1e-6
LANES = 128
SUBLANES = 8
CHUNK = 128
TOP_K = 2
SEQ_TILE = 512
CONV_ROWS = 32
CONV_HALO = 32
QK_HALO = 8
EXPERT_BLOCK = 256
SC_CHUNK = 32
COMBINE_TILE = 512
N_PARTS = 2
VMEM_LIMIT = 56 * 1024 * 1024

_NT = (((1,), (1,)), ((), ()))
_TN = (((0,), (0,)), ((), ()))


def _dot(a, b):
    return jnp.dot(a, b, preferred_element_type=jnp.float32)


def _silu(x):
    return x * jax.nn.sigmoid(x)


def _log_sigmoid(x):
    return jnp.minimum(x, 0.0) - jnp.log1p(jnp.exp(-jnp.abs(x)))


def _split_bf16(x, parts):
    out = []
    for _ in range(parts):
        p = x.astype(jnp.bfloat16)
        out.append(p)
        x = x - p.astype(jnp.float32)
    return out


def _mixer_kernel(b0_ref, x_ref, gmix_ref, wcv_ref, bcv_ref, wqk_ref, bqk_ref, wvo_ref, bvo_ref,
                  wif_ref, bif_ref, wift_ref, bift_ref, cw_ref, cb_ref, lng_ref, lnb_ref,
                  qkw_ref, qkb_ref, mng_ref, wout_ref, gffn_ref, wrh_ref, wrl_ref, br_ref,
                  su_ref, tril_ref,
                  x1_ref, h2_ref, ri_ref, gc_ref, cnt_ref,
                  uext, qkext, q_s, k_s, v_s, o_s, cn_s, m_s, y_s, cnt_s,
                  *, n_heads, conv_width, qk_width, n_groups, epg):
    ts, d = x_ref.shape
    dc = cw_ref.shape[1]
    dm = mng_ref.shape[1]
    dh = dm // n_heads
    n_chunks = ts // CHUNK
    n_experts = n_groups * epg
    b_idx = pl.program_id(0)
    t_idx = pl.program_id(1)

    @pl.when(t_idx == 0)
    def _reset_sequence_state():
        uext[0:CONV_HALO, :] = jnp.zeros((CONV_HALO, dc), jnp.float32)
        uext[CONV_HALO + ts:, :] = jnp.zeros((SUBLANES, dc), jnp.float32)
        qkext[0:QK_HALO, :] = jnp.zeros((QK_HALO, 2 * dm), jnp.float32)
        cn_s[...] = jnp.zeros(cn_s.shape, jnp.float32)
        m_s[...] = jnp.zeros(m_s.shape, jnp.float32)

    @pl.when((t_idx == 0) & (b_idx == 0))
    def _reset_counts():
        cnt_s[...] = jnp.zeros(cnt_s.shape, jnp.float32)

    x = x_ref[...]
    h = x * lax.rsqrt(jnp.mean(x * x, axis=-1, keepdims=True) + EPS) * gmix_ref[...]
    hb = h.astype(jnp.bfloat16)

    zc = _dot(hb, wcv_ref[...]) + bcv_ref[...]
    uext[CONV_HALO:CONV_HALO + ts, :] = zc[:, :dc] * jax.nn.sigmoid(zc[:, dc:])
    first = CONV_HALO - (conv_width - 1)
    sub = lax.broadcasted_iota(jnp.int32, (SUBLANES, dc), 0)
    n_grp = CONV_ROWS // SUBLANES
    for rb in range(ts // CONV_ROWS):
        r0 = rb * CONV_ROWS
        acc = jnp.zeros((CONV_ROWS, dc), jnp.float32) + cb_ref[...]
        for r in range(SUBLANES):
            part = None
            for j in range(conv_width):
                if (first + j) % SUBLANES != r:
                    continue
                a0 = r0 + (first + j) - r
                w8 = cw_ref[j * SUBLANES:(j + 1) * SUBLANES, :]
                term = jnp.concatenate([w8] * (n_grp + 1), axis=0) * uext[a0:a0 + CONV_ROWS + SUBLANES, :]
                part = term if part is None else part + term
            if part is None:
                continue
            if r == 0:
                acc = acc + part[:CONV_ROWS, :]
                continue
            rolled = [pltpu.roll(part[g * SUBLANES:(g + 1) * SUBLANES, :], SUBLANES - r, axis=0)
                      for g in range(n_grp + 1)]
            acc = acc + jnp.concatenate(
                [jnp.where(sub < SUBLANES - r, rolled[g], rolled[g + 1]) for g in range(n_grp)], axis=0)
        mu = jnp.mean(acc, axis=-1, keepdims=True)
        cen = acc - mu
        var = jnp.mean(cen * cen, axis=-1, keepdims=True)
        yn = cen * lax.rsqrt(var + EPS) * lng_ref[...] + lnb_ref[...]
        y_s[r0:r0 + CONV_ROWS, 0:dc] = _silu(yn).astype(jnp.bfloat16)
    uext[0:CONV_HALO, :] = uext[ts:ts + CONV_HALO, :]

    qkext[QK_HALO:QK_HALO + ts, :] = _dot(hb, wqk_ref[...]) + bqk_ref[...]
    qfirst = QK_HALO - (qk_width - 1)
    k_scale = dh ** -0.5
    for rb in range(ts // CONV_ROWS):
        r0 = rb * CONV_ROWS
        acc = jnp.zeros((CONV_ROWS, 2 * dm), jnp.float32) + qkb_ref[...]
        for j in range(qk_width):
            w8 = qkw_ref[j * SUBLANES:(j + 1) * SUBLANES, :]
            acc = acc + (jnp.concatenate([w8] * (CONV_ROWS // SUBLANES), axis=0)
                         * qkext[r0 + qfirst + j:r0 + qfirst + j + CONV_ROWS, :])
        act = _silu(acc)
        q_s[r0:r0 + CONV_ROWS, :] = act[:, :dm].astype(jnp.bfloat16)
        k_s[r0:r0 + CONV_ROWS, :] = (act[:, dm:] * k_scale).astype(jnp.bfloat16)
    qkext[0:QK_HALO, :] = qkext[ts:ts + QK_HALO, :]

    zvo = _dot(hb, wvo_ref[...]) + bvo_ref[...]
    v_s[...] = zvo[:, :dm].astype(jnp.bfloat16)
    o_s[...] = jax.nn.sigmoid(zvo[:, dm:])

    zif_col = _dot(hb, wif_ref[...]) + bif_ref[...]
    zif_row = lax.dot_general(wift_ref[...], hb, _NT,
                              preferred_element_type=jnp.float32) + bift_ref[...]
    logf_col = _log_sigmoid(zif_col)
    logf_row = _log_sigmoid(zif_row)
    lane = lax.broadcasted_iota(jnp.int32, (SUBLANES, CHUNK), 1)
    causal = (lax.broadcasted_iota(jnp.int32, (CHUNK, CHUNK), 0)
              >= lax.broadcasted_iota(jnp.int32, (CHUNK, CHUNK), 1))
    ones_blk = jnp.ones((CHUNK, dh), jnp.bfloat16)
    tril = tril_ref[...]

    for c in range(n_chunks):
        r0 = c * CHUNK
        b_row = logf_row[:, r0:r0 + CHUNK]
        shift = 1
        while shift < CHUNK:
            b_row = b_row + jnp.where(lane >= shift, pltpu.roll(b_row, shift, axis=1), 0.0)
            shift *= 2
        b_col = sum(_dot(tril, p) for p in _split_bf16(logf_col[r0:r0 + CHUNK, :], 3))
        i_col_all = zif_col[r0:r0 + CHUNK, :]
        i_row_all = zif_row[:, r0:r0 + CHUNK]
        for hd in range(n_heads):
            c0 = hd * dh
            b_c = b_col[:, n_heads + hd:n_heads + hd + 1]
            i_c = i_col_all[:, hd:hd + 1]
            b_r = b_row[n_heads + hd:n_heads + hd + 1, :]
            i_r = i_row_all[hd:hd + 1, :]
            m_prev = m_s[hd, 0:1, 0:1]
            q = q_s[r0:r0 + CHUNK, c0:c0 + dh]
            k = k_s[r0:r0 + CHUNK, c0:c0 + dh]
            vaug = jnp.concatenate([v_s[r0:r0 + CHUNK, c0:c0 + dh], ones_blk], axis=-1)
            cn = cn_s[hd]

            dmat = jnp.where(causal, b_c + (i_r - b_r), -jnp.inf)
            g = b_c + m_prev
            m_t = jnp.maximum(g, jnp.max(dmat, axis=-1, keepdims=True))
            w_intra = jnp.exp(dmat - m_t)
            w_inter = jnp.exp(g - m_t)
            s = lax.dot_general(q, k, _NT, preferred_element_type=jnp.float32) * w_intra
            nd = _dot(s.astype(jnp.bfloat16), vaug) + w_inter * _dot(q, cn.astype(jnp.bfloat16))
            hval = nd[:, :dh] / jnp.maximum(jnp.abs(nd[:, dh:]), jnp.exp(-m_t))

            b_last = b_r[:, CHUNK - 1:CHUNK]
            logw_c = b_last - b_c + i_c
            m_new = jnp.maximum(b_last + m_prev, jnp.max(logw_c, axis=0, keepdims=True))
            decay = jnp.exp(b_last + m_prev - m_new)
            kw = (k.astype(jnp.float32) * jnp.exp(logw_c - m_new)).astype(jnp.bfloat16)
            cn_s[hd] = decay * cn + lax.dot_general(kw, vaug, _TN,
                                                    preferred_element_type=jnp.float32)
            m_s[hd] = jnp.broadcast_to(m_new, m_s.shape[1:])

            hm = o_s[r0:r0 + CHUNK, c0:c0 + dh] * hval
            mu = jnp.mean(hm, axis=-1, keepdims=True)
            cen = hm - mu
            var = jnp.mean(cen * cen, axis=-1, keepdims=True)
            hn = cen * lax.rsqrt(var + EPS) * mng_ref[:, c0:c0 + dh]
            y_s[r0:r0 + CHUNK, dc + c0:dc + c0 + dh] = hn.astype(jnp.bfloat16)

    x1 = x + _dot(y_s[...], wout_ref[...])
    x1_ref[...] = x1
    h2 = x1 * lax.rsqrt(jnp.mean(x1 * x1, axis=-1, keepdims=True) + EPS) * gffn_ref[...]
    nsub = d // LANES
    for sblk in range(nsub):
        h2_ref[pl.ds(sblk, ts, stride=nsub), :] = h2[:, sblk * LANES:(sblk + 1) * LANES]

    h2_hi, h2_lo = _split_bf16(h2, 2)
    lg = (lax.dot_general(wrh_ref[...], h2_hi, _NT, preferred_element_type=jnp.float32)
          + lax.dot_general(wrl_ref[...], h2_hi, _NT, preferred_element_type=jnp.float32)
          + lax.dot_general(wrh_ref[...], h2_lo, _NT, preferred_element_type=jnp.float32)
          + br_ref[...])
    gl = lg[0:n_groups, :]
    gidx = lax.broadcasted_iota(jnp.int32, (n_groups, ts), 0)
    gmax = jnp.max(gl, axis=0, keepdims=True)
    grp = jnp.min(jnp.where(gl == gmax, gidx, n_groups), axis=0, keepdims=True)
    p_grp = 1.0 / jnp.sum(jnp.exp(gl - gmax), axis=0, keepdims=True)
    in_group = jnp.zeros((epg, ts), jnp.float32)
    for gi in range(n_groups):
        in_group = jnp.where(grp == gi, lg[SUBLANES + gi * epg:SUBLANES + (gi + 1) * epg, :], in_group)
    eidx = lax.broadcasted_iota(jnp.int32, (epg, ts), 0)
    v1 = jnp.max(in_group, axis=0, keepdims=True)
    i1 = jnp.min(jnp.where(in_group == v1, eidx, epg), axis=0, keepdims=True)
    rest = jnp.where(eidx == i1, -jnp.inf, in_group)
    v2 = jnp.max(rest, axis=0, keepdims=True)
    i2 = jnp.min(jnp.where(rest == v2, eidx, epg), axis=0, keepdims=True)
    e21 = jnp.exp(v2 - v1)
    g1 = p_grp / (1.0 + e21)
    g2 = g1 * e21
    e1 = grp * epg + i1
    e2 = grp * epg + i2

    xidx = lax.broadcasted_iota(jnp.int32, (n_experts, ts), 0)
    sel1 = xidx == e1
    sel2 = xidx == e2
    onehot = jnp.where(sel1 | sel2, 1.0, 0.0).astype(jnp.bfloat16)
    before = cnt_s[:, 0:1] + _dot(onehot, su_ref[...])
    rank1 = jnp.sum(jnp.where(sel1, before, 0.0), axis=0, keepdims=True)
    rank2 = jnp.sum(jnp.where(sel2, before, 0.0), axis=0, keepdims=True)
    cnt_new = cnt_s[...] + jnp.sum(onehot.astype(jnp.float32), axis=1, keepdims=True)
    cnt_s[...] = cnt_new
    cnt_ref[...] = cnt_new

    zero_i = jnp.zeros((SUBLANES - 4, ts), jnp.int32)
    ri_ref[...] = jnp.concatenate(
        [e1, e2, rank1.astype(jnp.int32), rank2.astype(jnp.int32), zero_i], axis=0)
    gates_row = jnp.concatenate([g1, g2, jnp.zeros((LANES - 2, ts), jnp.float32)], axis=0)
    gc_ref[...] = gates_row.T


def _mixer_call(x, p, b0, bsz, *, n_heads, conv_width, qk_width, n_groups, epg):
    _, seq, d = x.shape
    ts = min(SEQ_TILE, seq)
    nt = seq // ts
    dc = p["cw"].shape[1]
    dm = p["mng"].shape[1]
    dh = dm // n_heads
    n_experts = n_groups * epg
    assert seq % ts == 0 and ts % CHUNK == 0 and dh == LANES and d % LANES == 0
    assert epg == SUBLANES and n_groups <= SUBLANES and conv_width - 1 <= CONV_HALO

    consts = [p[n] for n in ("gmix", "wcv", "bcv", "wqk", "bqk", "wvo", "bvo", "wif", "bif",
                             "wift", "bift", "cw", "cb", "lng", "lnb", "qkw", "qkb", "mng",
                             "wout", "gffn", "wrh", "wrl", "br", "su", "tril")]

    def const_spec(a):
        return pl.BlockSpec(a.shape, lambda b, t, b0r: (0,) * a.ndim)

    kern = functools.partial(_mixer_kernel, n_heads=n_heads, conv_width=conv_width,
                             qk_width=qk_width, n_groups=n_groups, epg=epg)
    grid_spec = pltpu.PrefetchScalarGridSpec(
        num_scalar_prefetch=1,
        grid=(bsz, nt),
        in_specs=([pl.BlockSpec((None, ts, d), lambda b, t, b0r: (b + b0r[0], t, 0))]
                  + [const_spec(a) for a in consts]),
        out_specs=[
            pl.BlockSpec((None, ts, d), lambda b, t, b0r: (b, t, 0)),
            pl.BlockSpec((None, ts * (d // LANES), LANES), lambda b, t, b0r: (b, t, 0)),
            pl.BlockSpec((None, None, SUBLANES, ts), lambda b, t, b0r: (b, t, 0, 0)),
            pl.BlockSpec((None, ts, LANES), lambda b, t, b0r: (b, t, 0)),
            pl.BlockSpec((n_experts, LANES), lambda b, t, b0r: (0, 0)),
        ],
        scratch_shapes=[
            pltpu.VMEM((CONV_HALO + ts + SUBLANES, dc), jnp.float32),
            pltpu.VMEM((QK_HALO + ts, 2 * dm), jnp.float32),
            pltpu.VMEM((ts, dm), jnp.bfloat16),
            pltpu.VMEM((ts, dm), jnp.bfloat16),
            pltpu.VMEM((ts, dm), jnp.bfloat16),
            pltpu.VMEM((ts, dm), jnp.float32),
            pltpu.VMEM((n_heads, dh, 2 * dh), jnp.float32),
            pltpu.VMEM((n_heads, SUBLANES, LANES), jnp.float32),
            pltpu.VMEM((ts, d), jnp.bfloat16),
            pltpu.VMEM((n_experts, LANES), jnp.float32),
        ],
    )
    return pl.pallas_call(
        kern,
        grid_spec=grid_spec,
        out_shape=[
            jax.ShapeDtypeStruct((bsz, seq, d), jnp.float32),
            jax.ShapeDtypeStruct((bsz, seq * (d // LANES), LANES), jnp.float32),
            jax.ShapeDtypeStruct((bsz, nt, SUBLANES, ts), jnp.int32),
            jax.ShapeDtypeStruct((bsz, seq, LANES), jnp.float32),
            jax.ShapeDtypeStruct((n_experts, LANES), jnp.float32),
        ],
        compiler_params=pltpu.CompilerParams(
            dimension_semantics=("arbitrary", "arbitrary"), vmem_limit_bytes=VMEM_LIMIT),
        name="mixer_router",
    )(b0, x, *consts)


def _sc_workers():
    info = plsc.get_sparse_core_info()
    return info.num_cores, info.num_cores * info.num_subcores


def _worker_index_table(dest, n_workers):
    top_k, n_tok = dest.shape
    assert n_tok % (n_workers * SC_CHUNK) == 0
    return jnp.transpose(dest.reshape(top_k, n_workers, n_tok // (n_workers * SC_CHUNK), SC_CHUNK),
                         (1, 0, 2, 3))


def _sc_dispatch(dest_w, rows, n_slots):
    n_tok, nsub, l = rows.shape
    nw, top_k, n_chunks, ch = dest_w.shape
    tpw = n_tok // nw
    n_cores, n_workers = _sc_workers()
    assert nw == n_workers and tpw == n_chunks * ch

    @functools.partial(
        pl.kernel,
        mesh=plsc.VectorSubcoreMesh(core_axis_name="c", subcore_axis_name="s"),
        out_type=jax.ShapeDtypeStruct((n_slots, nsub, l), rows.dtype),
        scratch_types=[pltpu.VMEM((top_k, n_chunks, ch), jnp.int32),
                       pltpu.VMEM((ch, nsub, l), rows.dtype),
                       pltpu.SemaphoreType.DMA],
        name="sc_dispatch",
    )
    def run(dest_hbm, rows_hbm, out_hbm, idx_v, buf, sem):
        wid = lax.axis_index("s") * n_cores + lax.axis_index("c")
        pltpu.sync_copy(dest_hbm.at[wid], idx_v)

        @pl.loop(0, n_chunks)
        def _(j):
            pltpu.sync_copy(rows_hbm.at[pl.ds(wid * tpw + j * ch, ch)], buf)
            for k in range(top_k):
                pltpu.async_copy(buf, out_hbm.at[idx_v.at[k, j]], sem).wait()

    return run(dest_w, rows)


def _sc_gather(dest_w, ys, n_tok):
    _, nsub, l = ys.shape
    nw, top_k, n_chunks, ch = dest_w.shape
    tpw = n_tok // nw
    n_cores, n_workers = _sc_workers()
    assert nw == n_workers and tpw == n_chunks * ch

    @functools.partial(
        pl.kernel,
        mesh=plsc.VectorSubcoreMesh(core_axis_name="c", subcore_axis_name="s"),
        out_type=jax.ShapeDtypeStruct((top_k, n_tok, nsub, l), ys.dtype),
        scratch_types=[pltpu.VMEM((top_k, n_chunks, ch), jnp.int32),
                       pltpu.VMEM((ch, nsub, l), ys.dtype),
                       pltpu.SemaphoreType.DMA],
        name="sc_gather",
    )
    def run(dest_hbm, ys_hbm, out_hbm, idx_v, buf, sem):
        wid = lax.axis_index("s") * n_cores + lax.axis_index("c")
        pltpu.sync_copy(dest_hbm.at[wid], idx_v)

        @pl.loop(0, n_chunks)
        def _(j):
            for k in range(top_k):
                pltpu.async_copy(ys_hbm.at[idx_v.at[k, j]], buf, sem).wait()
                pltpu.sync_copy(buf, out_hbm.at[k, pl.ds(wid * tpw + j * ch, ch)])

    return run(dest_w, ys)


def _expert_kernel(be_ref, valid_ref, nused_ref, xs_ref, wg_ref, wu_ref, wd_ref, ys_ref,
                   wg_b, wu_b, wd_b, *, nsub):
    i = pl.program_id(0)
    blk = xs_ref.shape[0] // nsub
    l = xs_ref.shape[1]

    @pl.when((i == 0) | (be_ref[i] != be_ref[jnp.maximum(i - 1, 0)]))
    def _new_expert_weights():
        wg_b[...] = wg_ref[...].astype(jnp.bfloat16)
        wu_b[...] = wu_ref[...].astype(jnp.bfloat16)
        wd_b[...] = wd_ref[...].astype(jnp.bfloat16)

    @pl.when(i < nused_ref[0])
    def _compute():
        live = lax.broadcasted_iota(jnp.int32, (blk, l), 0) < valid_ref[i]
        xb = jnp.concatenate(
            [jnp.where(live, xs_ref[pl.ds(j, blk, stride=nsub), :], 0.0).astype(jnp.bfloat16)
             for j in range(nsub)], axis=-1)
        a = _dot(xb, wg_b[...])
        u = _dot(xb, wu_b[...])
        y = _dot((_silu(a) * u).astype(jnp.bfloat16), wd_b[...])
        for j in range(nsub):
            ys_ref[pl.ds(j, blk, stride=nsub), :] = y[:, j * l:(j + 1) * l]

    @pl.when(i >= nused_ref[0])
    def _unused_block():
        ys_ref[...] = jnp.zeros(ys_ref.shape, ys_ref.dtype)


def _expert_call(block_expert, block_valid, n_used, xs, wg, wu, wd, nsub):
    rows, l = xs.shape
    n_blocks = rows // (EXPERT_BLOCK * nsub)
    _, d, de = wg.shape

    def row_map(i, be, bv, nu):
        return (i, 0)

    def w_map(i, be, bv, nu):
        return (be[i], 0, 0)

    return pl.pallas_call(
        functools.partial(_expert_kernel, nsub=nsub),
        grid_spec=pltpu.PrefetchScalarGridSpec(
            num_scalar_prefetch=3,
            grid=(n_blocks,),
            in_specs=[
                pl.BlockSpec((EXPERT_BLOCK * nsub, l), row_map),
                pl.BlockSpec((None, d, de), w_map),
                pl.BlockSpec((None, d, de), w_map),
                pl.BlockSpec((None, de, d), w_map),
            ],
            out_specs=pl.BlockSpec((EXPERT_BLOCK * nsub, l), row_map),
            scratch_shapes=[pltpu.VMEM((d, de), jnp.bfloat16), pltpu.VMEM((d, de), jnp.bfloat16),
                            pltpu.VMEM((de, d), jnp.bfloat16)],
        ),
        out_shape=jax.ShapeDtypeStruct((rows, l), jnp.float32),
        compiler_params=pltpu.CompilerParams(
            dimension_semantics=("arbitrary",), vmem_limit_bytes=VMEM_LIMIT),
        name="experts",
    )(block_expert, block_valid, n_used, xs, wg, wu, wd)


def _combine_kernel(x1_ref, gc_ref, gfin_ref, y_ref, out_ref, *, nsub):
    tc, d = x1_ref.shape
    g1 = gc_ref[:, 0:1]
    g2 = gc_ref[:, 1:2]
    pieces = []
    ssq = jnp.zeros((tc, 1), jnp.float32)
    for j in range(nsub):
        y1 = y_ref[0, pl.ds(j, tc, stride=nsub), :]
        y2 = y_ref[1, pl.ds(j, tc, stride=nsub), :]
        z = x1_ref[:, j * LANES:(j + 1) * LANES] + (g1 * y1 + g2 * y2)
        pieces.append(z)
        ssq = ssq + jnp.sum(z * z, axis=-1, keepdims=True)
    scale = lax.rsqrt(ssq / d + EPS)
    for j, z in enumerate(pieces):
        out_ref[:, j * LANES:(j + 1) * LANES] = z * scale * gfin_ref[:, j * LANES:(j + 1) * LANES]


def _combine_call(x1, gates_col, gfin, y_tok, nsub, out_prev, tok0, n_tok_total):
    n_tok, d = x1.shape
    tc = min(COMBINE_TILE, n_tok)
    assert n_tok % tc == 0 and tok0 % tc == 0
    l = y_tok.shape[-1]
    in_specs = [
        pl.BlockSpec((tc, d), lambda i: (i, 0)),
        pl.BlockSpec((tc, LANES), lambda i: (i, 0)),
        pl.BlockSpec((1, d), lambda i: (0, 0)),
        pl.BlockSpec((TOP_K, tc * nsub, l), lambda i: (0, i, 0)),
    ]
    args = [x1, gates_col, gfin, y_tok]
    aliases = {}
    kern = functools.partial(_combine_kernel, nsub=nsub)
    if out_prev is not None:
        in_specs.append(pl.BlockSpec(memory_space=pl.ANY))
        args.append(out_prev)
        aliases = {len(args) - 1: 0}
        kern = lambda x1_ref, gc_ref, gfin_ref, y_ref, prev_ref, out_ref: _combine_kernel(
            x1_ref, gc_ref, gfin_ref, y_ref, out_ref, nsub=nsub)
    return pl.pallas_call(
        kern,
        grid=(n_tok // tc,),
        in_specs=in_specs,
        out_specs=pl.BlockSpec((tc, d), lambda i: (i + tok0 // tc, 0)),
        out_shape=jax.ShapeDtypeStruct((n_tok_total, d), jnp.float32),
        input_output_aliases=aliases,
        compiler_params=pltpu.CompilerParams(
            dimension_semantics=("arbitrary",), vmem_limit_bytes=VMEM_LIMIT),
        name="combine",
    )(*args)


def _prepare_layer(l, norm_mix_g, w_in, b_in, conv_w, conv_b, conv_ln_g, conv_ln_b, qk_conv_w,
                   qk_conv_b, mlstm_norm_g, w_out, norm_ffn_g, router_group_w, router_group_b,
                   router_expert_w, router_expert_b, ts):
    f32, bf16 = jnp.float32, jnp.bfloat16
    d = w_in.shape[1]
    dc = conv_w.shape[-1]
    dm = mlstm_norm_g.shape[-1]
    n_heads = (w_in.shape[-1] - 2 * dc - 4 * dm) // 2
    n_groups = router_group_w.shape[-1]
    n_experts = router_expert_w.shape[-1]
    wi, bi = w_in[l], b_in[l]
    o_qk, o_vo, o_if = 2 * dc, 2 * dc + 2 * dm, 2 * dc + 4 * dm
    w_if = wi[:, o_if:]
    b_if = bi[o_if:]
    wr_t = jnp.zeros((SUBLANES + n_experts, d), f32)
    wr_t = wr_t.at[:n_groups].set(router_group_w[l].T).at[SUBLANES:].set(router_expert_w[l].T)
    br = jnp.zeros((SUBLANES + n_experts, 1), f32)
    br = br.at[:n_groups, 0].set(router_group_b[l]).at[SUBLANES:, 0].set(router_expert_b[l])
    wr_hi = wr_t.astype(bf16)
    wr_lo = (wr_t - wr_hi.astype(f32)).astype(bf16)
    pad_rows = lambda a, n: jnp.pad(a, ((0, n - a.shape[0]), (0, 0)))
    idx = jnp.arange(ts)
    cidx = jnp.arange(CHUNK)
    return dict(
        gmix=norm_mix_g[l][None, :],
        wcv=wi[:, :o_qk].astype(bf16), bcv=bi[None, :o_qk],
        wqk=wi[:, o_qk:o_vo].astype(bf16), bqk=bi[None, o_qk:o_vo],
        wvo=wi[:, o_vo:o_if].astype(bf16), bvo=bi[None, o_vo:o_if],
        wif=jnp.pad(w_if, ((0, 0), (0, LANES - 2 * n_heads))).astype(bf16),
        bif=jnp.pad(b_if, (0, LANES - 2 * n_heads))[None, :],
        wift=pad_rows(w_if.T, SUBLANES).astype(bf16),
        bift=pad_rows(b_if[:, None], SUBLANES),
        cw=jnp.repeat(conv_w[l], SUBLANES, axis=0), cb=conv_b[l][None, :],
        lng=conv_ln_g[l][None, :], lnb=conv_ln_b[l][None, :],
        qkw=jnp.repeat(qk_conv_w[l], SUBLANES, axis=0), qkb=qk_conv_b[l][None, :],
        mng=mlstm_norm_g[l][None, :],
        wout=w_out[l].astype(bf16), gffn=norm_ffn_g[l][None, :],
        wrh=wr_hi, wrl=wr_lo, br=br,
        su=(idx[:, None] < idx[None, :]).astype(bf16),
        tril=(cidx[:, None] >= cidx[None, :]).astype(bf16),
    ), dict(n_heads=n_heads, conv_width=conv_w.shape[1], qk_width=qk_conv_w.shape[1],
            n_groups=n_groups, epg=n_experts // n_groups)


def kernel(x, norm_mix_g, w_in, b_in, conv_w, conv_b, conv_ln_g, conv_ln_b, qk_conv_w, qk_conv_b,
           mlstm_norm_g, w_out, norm_ffn_g, router_group_w, router_group_b, router_expert_w,
           router_expert_b, expert_w_gate, expert_w_up, expert_w_down, final_norm_g):
    bsz, seq, d = x.shape
    n_tok = bsz * seq
    depth = w_in.shape[0]
    assert depth == 1, "the combine kernel fuses the final norm, so exactly one layer is supported"
    n_experts = router_expert_w.shape[-1]
    n_parts = N_PARTS if bsz % N_PARTS == 0 else 1
    pb = bsz // n_parts
    n_tok = pb * seq
    n_assign = n_tok * TOP_K
    n_blocks = -(-n_assign // EXPERT_BLOCK) + n_experts
    n_slots = n_blocks * EXPERT_BLOCK
    ts = min(SEQ_TILE, seq)
    nsub = d // LANES
    l = 0
    params, dims = _prepare_layer(
        l, norm_mix_g, w_in, b_in, conv_w, conv_b, conv_ln_g, conv_ln_b, qk_conv_w, qk_conv_b,
        mlstm_norm_g, w_out, norm_ffn_g, router_group_w, router_group_b, router_expert_w,
        router_expert_b, ts)
    wg, wu, wd = expert_w_gate[l], expert_w_up[l], expert_w_down[l]
    out = None
    prev_dest = None
    for part in range(n_parts):
        b0 = jnp.full((1,), part * pb, jnp.int32)
        if prev_dest is not None:
            b0 = b0 + jnp.minimum(prev_dest[:1, 0], 0)
        x1, h2_rows, route_i, gates_col, counts = _mixer_call(x, params, b0, pb, **dims)

        cnt = counts[:, 0].astype(jnp.int32)
        padded = (cnt + EXPERT_BLOCK - 1) // EXPERT_BLOCK * EXPERT_BLOCK
        padded_ends = jnp.cumsum(padded)
        padded_starts = padded_ends - padded
        route = jnp.transpose(route_i, (2, 0, 1, 3)).reshape(SUBLANES, n_tok)
        dest = jnp.stack([padded_starts[route[0]] + route[2],
                          padded_starts[route[1]] + route[3]], axis=0)
        block_start = jnp.arange(n_blocks, dtype=jnp.int32) * EXPERT_BLOCK
        block_expert = jnp.minimum(
            jnp.sum(block_start[:, None] >= padded_ends[None, :], axis=1), n_experts - 1).astype(jnp.int32)
        block_valid = jnp.clip((padded_starts + cnt)[block_expert] - block_start,
                               0, EXPERT_BLOCK).astype(jnp.int32)
        n_used = (padded_ends[-1:] // EXPERT_BLOCK).astype(jnp.int32)

        prev_dest = dest
        dest_w = _worker_index_table(dest, _sc_workers()[1])
        xs = _sc_dispatch(dest_w, h2_rows.reshape(n_tok, nsub, LANES), n_slots)
        ys = _expert_call(block_expert, block_valid, n_used, xs.reshape(n_slots * nsub, LANES),
                          wg, wu, wd, nsub)
        y_tok = _sc_gather(dest_w, ys.reshape(n_slots, nsub, LANES), n_tok)
        out = _combine_call(x1.reshape(n_tok, d), gates_col.reshape(n_tok, LANES),
                            final_norm_g[None, :], y_tok.reshape(TOP_K, n_tok * nsub, LANES), nsub,
                            out, part * n_tok, bsz * seq)
    return out.reshape(bsz, seq, d)
```

```python
import functools

import jax
import jax.numpy as jnp
from jax import lax
from jax.experimental import pallas as pl
from jax.experimental.pallas import tpu as pltpu
from jax.experimental.pallas import tpu_sc as plsc

EPS = 1e-6
LANES = 128
SUBLANES = 8
CHUNK = 128
TOP_K = 2
SEQ_TILE = 512
CONV_ROWS = 32
CONV_HALO = 32
QK_HALO = 8
EXPERT_BLOCK = 256
SC_CHUNK = 32
COMBINE_TILE = 512
N_PARTS = 2
PACK_COLS = 2 * LANES
VMEM_LIMIT = 56 * 1024 * 1024

_NT = (((1,), (1,)), ((), ()))
_TN = (((0,), (0,)), ((), ()))
_HI16 = 0xFFFF0000


def _dot(a, b):
    return jnp.dot(a, b, preferred_element_type=jnp.float32)


def _silu(x):
    return x * jax.nn.sigmoid(x)


def _log_sigmoid(x):
    return jnp.minimum(x, 0.0) - jnp.log1p(jnp.exp(-jnp.abs(x)))


def _split_bf16(x, parts):
    out = []
    for _ in range(parts):
        p = x.astype(jnp.bfloat16)
        out.append(p)
        x = x - p.astype(jnp.float32)
    return out


def _store_packed_rows(ref, v):
    n, d = v.shape
    half = d // 2
    nsub = half // LANES
    as_bits = lambda a: pltpu.bitcast(a.astype(jnp.bfloat16).astype(jnp.float32), jnp.uint32)
    word = (as_bits(v[:, :half]) >> 16) | (as_bits(v[:, half:]) & jnp.uint32(_HI16))
    for s in range(nsub):
        ref[pl.ds(s, n, stride=nsub), :] = word[:, s * LANES:(s + 1) * LANES]


def _load_packed_rows(ref, n, nsub, live=None):
    lo, hi = [], []
    for s in range(nsub):
        w = ref[pl.ds(s, n, stride=nsub), :]
        if live is not None:
            w = jnp.where(live, w, jnp.uint32(0))
        lo.append(pltpu.bitcast(w << 16, jnp.float32))
        hi.append(pltpu.bitcast(w & jnp.uint32(_HI16), jnp.float32))
    return lo + hi


def _mixer_kernel(b0_ref, x_ref, gmix_ref, wcv_ref, bcv_ref, wqk_ref, bqk_ref, wvo_ref, bvo_ref,
                  wif_ref, bif_ref, wift_ref, bift_ref, cw_ref, cb_ref, lng_ref, lnb_ref,
                  qkw_ref, qkb_ref, mng_ref, wout_ref, gffn_ref, wrh_ref, wrl_ref, br_ref,
                  su_ref, tril_ref,
                  x1_ref, h2_ref, ri_ref, gc_ref, cnt_ref,
                  uext, qkext, q_s, k_s, v_s, o_s, cn_s, m_s, y_s, cnt_s,
                  *, n_heads, conv_width, qk_width, n_groups, epg):
    ts, d = x_ref.shape
    dc = cw_ref.shape[1]
    dm = mng_ref.shape[1]
    dh = dm // n_heads
    n_chunks = ts // CHUNK
    n_experts = n_groups * epg
    b_idx = pl.program_id(0)
    t_idx = pl.program_id(1)

    @pl.when(t_idx == 0)
    def _reset_sequence_state():
        uext[0:CONV_HALO, :] = jnp.zeros((CONV_HALO, dc), jnp.float32)
        uext[CONV_HALO + ts:, :] = jnp.zeros((SUBLANES, dc), jnp.float32)
        qkext[0:QK_HALO, :] = jnp.zeros((QK_HALO, 2 * dm), jnp.float32)
        cn_s[...] = jnp.zeros(cn_s.shape, jnp.float32)
        m_s[...] = jnp.zeros(m_s.shape, jnp.float32)

    @pl.when((t_idx == 0) & (b_idx == 0))
    def _reset_counts():
        cnt_s[...] = jnp.zeros(cnt_s.shape, jnp.float32)

    x = x_ref[...]
    h = x * lax.rsqrt(jnp.mean(x * x, axis=-1, keepdims=True) + EPS) * gmix_ref[...]
    hb = h.astype(jnp.bfloat16)

    zc = _dot(hb, wcv_ref[...]) + bcv_ref[...]
    uext[CONV_HALO:CONV_HALO + ts, :] = zc[:, :dc] * jax.nn.sigmoid(zc[:, dc:])
    first = CONV_HALO - (conv_width - 1)
    sub = lax.broadcasted_iota(jnp.int32, (SUBLANES, dc), 0)
    n_grp = CONV_ROWS // SUBLANES
    for rb in range(ts // CONV_ROWS):
        r0 = rb * CONV_ROWS
        acc = jnp.zeros((CONV_ROWS, dc), jnp.float32) + cb_ref[...]
        for r in range(SUBLANES):
            part = None
            for j in range(conv_width):
                if (first + j) % SUBLANES != r:
                    continue
                a0 = r0 + (first + j) - r
                w8 = cw_ref[j * SUBLANES:(j + 1) * SUBLANES, :]
                term = jnp.concatenate([w8] * (n_grp + 1), axis=0) * uext[a0:a0 + CONV_ROWS + SUBLANES, :]
                part = term if part is None else part + term
            if part is None:
                continue
            if r == 0:
                acc = acc + part[:CONV_ROWS, :]
                continue
            rolled = [pltpu.roll(part[g * SUBLANES:(g + 1) * SUBLANES, :], SUBLANES - r, axis=0)
                      for g in range(n_grp + 1)]
            acc = acc + jnp.concatenate(
                [jnp.where(sub < SUBLANES - r, rolled[g], rolled[g + 1]) for g in range(n_grp)], axis=0)
        mu = jnp.mean(acc, axis=-1, keepdims=True)
        cen = acc - mu
        var = jnp.mean(cen * cen, axis=-1, keepdims=True)
        yn = cen * lax.rsqrt(var + EPS) * lng_ref[...] + lnb_ref[...]
        y_s[r0:r0 + CONV_ROWS, 0:dc] = _silu(yn).astype(jnp.bfloat16)
    uext[0:CONV_HALO, :] = uext[ts:ts + CONV_HALO, :]

    qkext[QK_HALO:QK_HALO + ts, :] = _dot(hb, wqk_ref[...]) + bqk_ref[...]
    qfirst = QK_HALO - (qk_width - 1)
    k_scale = dh ** -0.5
    for rb in range(ts // CONV_ROWS):
        r0 = rb * CONV_ROWS
        acc = jnp.zeros((CONV_ROWS, 2 * dm), jnp.float32) + qkb_ref[...]
        for j in range(qk_width):
            w8 = qkw_ref[j * SUBLANES:(j + 1) * SUBLANES, :]
            acc = acc + (jnp.concatenate([w8] * (CONV_ROWS // SUBLANES), axis=0)
                         * qkext[r0 + qfirst + j:r0 + qfirst + j + CONV_ROWS, :])
        act = _silu(acc)
        q_s[r0:r0 + CONV_ROWS, :] = act[:, :dm].astype(jnp.bfloat16)
        k_s[r0:r0 + CONV_ROWS, :] = (act[:, dm:] * k_scale).astype(jnp.bfloat16)
    qkext[0:QK_HALO, :] = qkext[ts:ts + QK_HALO, :]

    zvo = _dot(hb, wvo_ref[...]) + bvo_ref[...]
    v_s[...] = zvo[:, :dm].astype(jnp.bfloat16)
    o_s[...] = jax.nn.sigmoid(zvo[:, dm:])

    zif_col = _dot(hb, wif_ref[...]) + bif_ref[...]
    zif_row = lax.dot_general(wift_ref[...], hb, _NT,
                              preferred_element_type=jnp.float32) + bift_ref[...]
    logf_col = _log_sigmoid(zif_col)
    logf_row = _log_sigmoid(zif_row)
    lane = lax.broadcasted_iota(jnp.int32, (SUBLANES, CHUNK), 1)
    causal = (lax.broadcasted_iota(jnp.int32, (CHUNK, CHUNK), 0)
              >= lax.broadcasted_iota(jnp.int32, (CHUNK, CHUNK), 1))
    ones_blk = jnp.ones((CHUNK, dh), jnp.bfloat16)
    tril = tril_ref[...]

    for c in range(n_chunks):
        r0 = c * CHUNK
        b_row = logf_row[:, r0:r0 + CHUNK]
        shift = 1
        while shift < CHUNK:
            b_row = b_row + jnp.where(lane >= shift, pltpu.roll(b_row, shift, axis=1), 0.0)
            shift *= 2
        b_col = sum(_dot(tril, p) for p in _split_bf16(logf_col[r0:r0 + CHUNK, :], 3))
        i_col_all = zif_col[r0:r0 + CHUNK, :]
        i_row_all = zif_row[:, r0:r0 + CHUNK]
        for hd in range(n_heads):
            c0 = hd * dh
            b_c = b_col[:, n_heads + hd:n_heads + hd + 1]
            i_c = i_col_all[:, hd:hd + 1]
            b_r = b_row[n_heads + hd:n_heads + hd + 1, :]
            i_r = i_row_all[hd:hd + 1, :]
            m_prev = m_s[hd, 0:1, 0:1]
            q = q_s[r0:r0 + CHUNK, c0:c0 + dh]
            k = k_s[r0:r0 + CHUNK, c0:c0 + dh]
            vaug = jnp.concatenate([v_s[r0:r0 + CHUNK, c0:c0 + dh], ones_blk], axis=-1)
            cn = cn_s[hd]

            dmat = jnp.where(causal, b_c + (i_r - b_r), -jnp.inf)
            g = b_c + m_prev
            m_t = jnp.maximum(g, jnp.max(dmat, axis=-1, keepdims=True))
            w_intra = jnp.exp(dmat - m_t)
            w_inter = jnp.exp(g - m_t)
            s = lax.dot_general(q, k, _NT, preferred_element_type=jnp.float32) * w_intra
            nd = _dot(s.astype(jnp.bfloat16), vaug) + w_inter * _dot(q, cn.astype(jnp.bfloat16))
            hval = nd[:, :dh] / jnp.maximum(jnp.abs(nd[:, dh:]), jnp.exp(-m_t))

            b_last = b_r[:, CHUNK - 1:CHUNK]
            logw_c = b_last - b_c + i_c
            m_new = jnp.maximum(b_last + m_prev, jnp.max(logw_c, axis=0, keepdims=True))
            decay = jnp.exp(b_last + m_prev - m_new)
            kw = (k.astype(jnp.float32) * jnp.exp(logw_c - m_new)).astype(jnp.bfloat16)
            cn_s[hd] = decay * cn + lax.dot_general(kw, vaug, _TN,
                                                    preferred_element_type=jnp.float32)
            m_s[hd] = jnp.broadcast_to(m_new, m_s.shape[1:])

            hm = o_s[r0:r0 + CHUNK, c0:c0 + dh] * hval
            mu = jnp.mean(hm, axis=-1, keepdims=True)
            cen = hm - mu
            var = jnp.mean(cen * cen, axis=-1, keepdims=True)
            hn = cen * lax.rsqrt(var + EPS) * mng_ref[:, c0:c0 + dh]
            y_s[r0:r0 + CHUNK, dc + c0:dc + c0 + dh] = hn.astype(jnp.bfloat16)

    x1 = x + _dot(y_s[...], wout_ref[...])
    x1_ref[...] = x1
    h2 = x1 * lax.rsqrt(jnp.mean(x1 * x1, axis=-1, keepdims=True) + EPS) * gffn_ref[...]
    _store_packed_rows(h2_ref, h2)

    h2_hi, h2_lo = _split_bf16(h2, 2)
    lg = (lax.dot_general(wrh_ref[...], h2_hi, _NT, preferred_element_type=jnp.float32)
          + lax.dot_general(wrl_ref[...], h2_hi, _NT, preferred_element_type=jnp.float32)
          + lax.dot_general(wrh_ref[...], h2_lo, _NT, preferred_element_type=jnp.float32)
          + br_ref[...])
    gl = lg[0:n_groups, :]
    gidx = lax.broadcasted_iota(jnp.int32, (n_groups, ts), 0)
    gmax = jnp.max(gl, axis=0, keepdims=True)
    grp = jnp.min(jnp.where(gl == gmax, gidx, n_groups), axis=0, keepdims=True)
    p_grp = 1.0 / jnp.sum(jnp.exp(gl - gmax), axis=0, keepdims=True)
    in_group = jnp.zeros((epg, ts), jnp.float32)
    for gi in range(n_groups):
        in_group = jnp.where(grp == gi, lg[SUBLANES + gi * epg:SUBLANES + (gi + 1) * epg, :], in_group)
    eidx = lax.broadcasted_iota(jnp.int32, (epg, ts), 0)
    v1 = jnp.max(in_group, axis=0, keepdims=True)
    i1 = jnp.min(jnp.where(in_group == v1, eidx, epg), axis=0, keepdims=True)
    rest = jnp.where(eidx == i1, -jnp.inf, in_group)
    v2 = jnp.max(rest, axis=0, keepdims=True)
    i2 = jnp.min(jnp.where(rest == v2, eidx, epg), axis=0, keepdims=True)
    e21 = jnp.exp(v2 - v1)
    g1 = p_grp / (1.0 + e21)
    g2 = g1 * e21
    e1 = grp * epg + i1
    e2 = grp * epg + i2

    xidx = lax.broadcasted_iota(jnp.int32, (n_experts, ts), 0)
    sel1 = xidx == e1
    sel2 = xidx == e2
    onehot = jnp.where(sel1 | sel2, 1.0, 0.0).astype(jnp.bfloat16)
    before = cnt_s[:, 0:1] + _dot(onehot, su_ref[...])
    rank1 = jnp.sum(jnp.where(sel1, before, 0.0), axis=0, keepdims=True)
    rank2 = jnp.sum(jnp.where(sel2, before, 0.0), axis=0, keepdims=True)
    cnt_new = cnt_s[...] + jnp.sum(onehot.astype(jnp.float32), axis=1, keepdims=True)
    cnt_s[...] = cnt_new
    cnt_ref[...] = cnt_new

    zero_i = jnp.zeros((SUBLANES - 4, ts), jnp.int32)
    ri_ref[...] = jnp.concatenate(
        [e1, e2, rank1.astype(jnp.int32), rank2.astype(jnp.int32), zero_i], axis=0)
    gates_row = jnp.concatenate([g1, g2, jnp.zeros((LANES - 2, ts), jnp.float32)], axis=0)
    gc_ref[...] = gates_row.T


def _mixer_call(x, p, b0, bsz, *, n_heads, conv_width, qk_width, n_groups, epg):
    _, seq, d = x.shape
    ts = min(SEQ_TILE, seq)
    nt = seq // ts
    dc = p["cw"].shape[1]
    dm = p["mng"].shape[1]
    dh = dm // n_heads
    n_experts = n_groups * epg
    assert seq % ts == 0 and ts % CHUNK == 0 and dh == LANES and d % PACK_COLS == 0
    assert epg == SUBLANES and n_groups <= SUBLANES and conv_width - 1 <= CONV_HALO

    consts = [p[n] for n in ("gmix", "wcv", "bcv", "wqk", "bqk", "wvo", "bvo", "wif", "bif",
                             "wift", "bift", "cw", "cb", "lng", "lnb", "qkw", "qkb", "mng",
                             "wout", "gffn", "wrh", "wrl", "br", "su", "tril")]

    def const_spec(a):
        return pl.BlockSpec(a.shape, lambda b, t, b0r: (0,) * a.ndim)

    kern = functools.partial(_mixer_kernel, n_heads=n_heads, conv_width=conv_width,
                             qk_width=qk_width, n_groups=n_groups, epg=epg)
    grid_spec = pltpu.PrefetchScalarGridSpec(
        num_scalar_prefetch=1,
        grid=(bsz, nt),
        in_specs=([pl.BlockSpec((None, ts, d), lambda b, t, b0r: (b + b0r[0], t, 0))]
                  + [const_spec(a) for a in consts]),
        out_specs=[
            pl.BlockSpec((None, ts, d), lambda b, t, b0r: (b, t, 0)),
            pl.BlockSpec((None, ts * (d // PACK_COLS), LANES), lambda b, t, b0r: (b, t, 0)),
            pl.BlockSpec((None, None, SUBLANES, ts), lambda b, t, b0r: (b, t, 0, 0)),
            pl.BlockSpec((None, ts, LANES), lambda b, t, b0r: (b, t, 0)),
            pl.BlockSpec((n_experts, LANES), lambda b, t, b0r: (0, 0)),
        ],
        scratch_shapes=[
            pltpu.VMEM((CONV_HALO + ts + SUBLANES, dc), jnp.float32),
            pltpu.VMEM((QK_HALO + ts, 2 * dm), jnp.float32),
            pltpu.VMEM((ts, dm), jnp.bfloat16),
            pltpu.VMEM((ts, dm), jnp.bfloat16),
            pltpu.VMEM((ts, dm), jnp.bfloat16),
            pltpu.VMEM((ts, dm), jnp.float32),
            pltpu.VMEM((n_heads, dh, 2 * dh), jnp.float32),
            pltpu.VMEM((n_heads, SUBLANES, LANES), jnp.float32),
            pltpu.VMEM((ts, d), jnp.bfloat16),
            pltpu.VMEM((n_experts, LANES), jnp.float32),
        ],
    )
    return pl.pallas_call(
        kern,
        grid_spec=grid_spec,
        out_shape=[
            jax.ShapeDtypeStruct((bsz, seq, d), jnp.float32),
            jax.ShapeDtypeStruct((bsz, seq * (d // PACK_COLS), LANES), jnp.uint32),
            jax.ShapeDtypeStruct((bsz, nt, SUBLANES, ts), jnp.int32),
            jax.ShapeDtypeStruct((bsz, seq, LANES), jnp.float32),
            jax.ShapeDtypeStruct((n_experts, LANES), jnp.float32),
        ],
        compiler_params=pltpu.CompilerParams(
            dimension_semantics=("arbitrary", "arbitrary"), vmem_limit_bytes=VMEM_LIMIT),
        name="mixer_router",
    )(b0, x, *consts)


def _sc_workers():
    info = plsc.get_sparse_core_info()
    return info.num_cores, info.num_cores * info.num_subcores


def _worker_index_table(dest, n_workers):
    top_k, n_tok = dest.shape
    assert n_tok % (n_workers * SC_CHUNK) == 0
    return jnp.transpose(dest.reshape(top_k, n_workers, n_tok // (n_workers * SC_CHUNK), SC_CHUNK),
                         (1, 0, 2, 3))


def _sc_dispatch(dest_w, rows, n_slots):
    n_tok, nsub, l = rows.shape
    nw, top_k, n_chunks, ch = dest_w.shape
    tpw = n_tok // nw
    n_cores, n_workers = _sc_workers()
    assert nw == n_workers and tpw == n_chunks * ch

    @functools.partial(
        pl.kernel,
        mesh=plsc.VectorSubcoreMesh(core_axis_name="c", subcore_axis_name="s"),
        out_type=jax.ShapeDtypeStruct((n_slots, nsub, l), rows.dtype),
        scratch_types=[pltpu.VMEM((top_k, n_chunks, ch), jnp.int32),
                       pltpu.VMEM((ch, nsub, l), rows.dtype),
                       pltpu.SemaphoreType.DMA],
        name="sc_dispatch",
    )
    def run(dest_hbm, rows_hbm, out_hbm, idx_v, buf, sem):
        wid = lax.axis_index("s") * n_cores + lax.axis_index("c")
        pltpu.sync_copy(dest_hbm.at[wid], idx_v)

        @pl.loop(0, n_chunks)
        def _(j):
            pltpu.sync_copy(rows_hbm.at[pl.ds(wid * tpw + j * ch, ch)], buf)
            for k in range(top_k):
                pltpu.async_copy(buf, out_hbm.at[idx_v.at[k, j]], sem).wait()

    return run(dest_w, rows)


def _sc_gather(dest_w, ys, n_tok):
    _, nsub, l = ys.shape
    nw, top_k, n_chunks, ch = dest_w.shape
    tpw = n_tok // nw
    n_cores, n_workers = _sc_workers()
    assert nw == n_workers and tpw == n_chunks * ch

    @functools.partial(
        pl.kernel,
        mesh=plsc.VectorSubcoreMesh(core_axis_name="c", subcore_axis_name="s"),
        out_type=jax.ShapeDtypeStruct((top_k, n_tok, nsub, l), ys.dtype),
        scratch_types=[pltpu.VMEM((top_k, n_chunks, ch), jnp.int32),
                       pltpu.VMEM((ch, nsub, l), ys.dtype),
                       pltpu.SemaphoreType.DMA],
        name="sc_gather",
    )
    def run(dest_hbm, ys_hbm, out_hbm, idx_v, buf, sem):
        wid = lax.axis_index("s") * n_cores + lax.axis_index("c")
        pltpu.sync_copy(dest_hbm.at[wid], idx_v)

        @pl.loop(0, n_chunks)
        def _(j):
            for k in range(top_k):
                pltpu.async_copy(ys_hbm.at[idx_v.at[k, j]], buf, sem).wait()
                pltpu.sync_copy(buf, out_hbm.at[k, pl.ds(wid * tpw + j * ch, ch)])

    return run(dest_w, ys)


def _expert_kernel(be_ref, valid_ref, nused_ref, xs_ref, wg_ref, wu_ref, wd_ref, ys_ref,
                   wg_b, wu_b, wd_b, *, nsub):
    i = pl.program_id(0)
    blk = xs_ref.shape[0] // nsub
    l = xs_ref.shape[1]

    @pl.when((i == 0) | (be_ref[i] != be_ref[jnp.maximum(i - 1, 0)]))
    def _new_expert_weights():
        wg_b[...] = wg_ref[...].astype(jnp.bfloat16)
        wu_b[...] = wu_ref[...].astype(jnp.bfloat16)
        wd_b[...] = wd_ref[...].astype(jnp.bfloat16)

    @pl.when(i < nused_ref[0])
    def _compute():
        live = lax.broadcasted_iota(jnp.int32, (blk, l), 0) < valid_ref[i]
        xb = jnp.concatenate(
            [c.astype(jnp.bfloat16) for c in _load_packed_rows(xs_ref, blk, nsub, live)], axis=-1)
        a = _dot(xb, wg_b[...])
        u = _dot(xb, wu_b[...])
        _store_packed_rows(ys_ref, _dot((_silu(a) * u).astype(jnp.bfloat16), wd_b[...]))

    @pl.when(i >= nused_ref[0])
    def _unused_block():
        ys_ref[...] = jnp.zeros(ys_ref.shape, ys_ref.dtype)


def _expert_call(block_expert, block_valid, n_used, xs, wg, wu, wd, nsub):
    rows, l = xs.shape
    n_blocks = rows // (EXPERT_BLOCK * nsub)
    _, d, de = wg.shape

    def row_map(i, be, bv, nu):
        return (i, 0)

    def w_map(i, be, bv, nu):
        return (be[i], 0, 0)

    return pl.pallas_call(
        functools.partial(_expert_kernel, nsub=nsub),
        grid_spec=pltpu.PrefetchScalarGridSpec(
            num_scalar_prefetch=3,
            grid=(n_blocks,),
            in_specs=[
                pl.BlockSpec((EXPERT_BLOCK * nsub, l), row_map),
                pl.BlockSpec((None, d, de), w_map),
                pl.BlockSpec((None, d, de), w_map),
                pl.BlockSpec((None, de, d), w_map),
            ],
            out_specs=pl.BlockSpec((EXPERT_BLOCK * nsub, l), row_map),
            scratch_shapes=[pltpu.VMEM((d, de), jnp.bfloat16), pltpu.VMEM((d, de), jnp.bfloat16),
                            pltpu.VMEM((de, d), jnp.bfloat16)],
        ),
        out_shape=jax.ShapeDtypeStruct((rows, l), xs.dtype),
        compiler_params=pltpu.CompilerParams(
            dimension_semantics=("arbitrary",), vmem_limit_bytes=VMEM_LIMIT),
        name="experts",
    )(block_expert, block_valid, n_used, xs, wg, wu, wd)


def _combine_kernel(x1_ref, gc_ref, gfin_ref, y_ref, out_ref, *, nsub):
    tc, d = x1_ref.shape
    g1 = gc_ref[:, 0:1]
    g2 = gc_ref[:, 1:2]
    pieces = []
    ssq = jnp.zeros((tc, 1), jnp.float32)
    y1_cols = _load_packed_rows(y_ref.at[0], tc, nsub)
    y2_cols = _load_packed_rows(y_ref.at[1], tc, nsub)
    for j, (y1, y2) in enumerate(zip(y1_cols, y2_cols)):
        z = x1_ref[:, j * LANES:(j + 1) * LANES] + (g1 * y1 + g2 * y2)
        pieces.append(z)
        ssq = ssq + jnp.sum(z * z, axis=-1, keepdims=True)
    scale = lax.rsqrt(ssq / d + EPS)
    for j, z in enumerate(pieces):
        out_ref[:, j * LANES:(j + 1) * LANES] = z * scale * gfin_ref[:, j * LANES:(j + 1) * LANES]


def _combine_call(x1, gates_col, gfin, y_tok, nsub, out_prev, tok0, n_tok_total):
    n_tok, d = x1.shape
    tc = min(COMBINE_TILE, n_tok)
    assert n_tok % tc == 0 and tok0 % tc == 0
    l = y_tok.shape[-1]
    in_specs = [
        pl.BlockSpec((tc, d), lambda i: (i, 0)),
        pl.BlockSpec((tc, LANES), lambda i: (i, 0)),
        pl.BlockSpec((1, d), lambda i: (0, 0)),
        pl.BlockSpec((TOP_K, tc * nsub, l), lambda i: (0, i, 0)),
    ]
    args = [x1, gates_col, gfin, y_tok]
    aliases = {}
    kern = functools.partial(_combine_kernel, nsub=nsub)
    if out_prev is not None:
        in_specs.append(pl.BlockSpec(memory_space=pl.ANY))
        args.append(out_prev)
        aliases = {len(args) - 1: 0}
        kern = lambda x1_ref, gc_ref, gfin_ref, y_ref, prev_ref, out_ref: _combine_kernel(
            x1_ref, gc_ref, gfin_ref, y_ref, out_ref, nsub=nsub)
    return pl.pallas_call(
        kern,
        grid=(n_tok // tc,),
        in_specs=in_specs,
        out_specs=pl.BlockSpec((tc, d), lambda i: (i + tok0 // tc, 0)),
        out_shape=jax.ShapeDtypeStruct((n_tok_total, d), jnp.float32),
        input_output_aliases=aliases,
        compiler_params=pltpu.CompilerParams(
            dimension_semantics=("arbitrary",), vmem_limit_bytes=VMEM_LIMIT),
        name="combine",
    )(*args)


def _prepare_layer(l, norm_mix_g, w_in, b_in, conv_w, conv_b, conv_ln_g, conv_ln_b, qk_conv_w,
                   qk_conv_b, mlstm_norm_g, w_out, norm_ffn_g, router_group_w, router_group_b,
                   router_expert_w, router_expert_b, ts):
    f32, bf16 = jnp.float32, jnp.bfloat16
    d = w_in.shape[1]
    dc = conv_w.shape[-1]
    dm = mlstm_norm_g.shape[-1]
    n_heads = (w_in.shape[-1] - 2 * dc - 4 * dm) // 2
    n_groups = router_group_w.shape[-1]
    n_experts = router_expert_w.shape[-1]
    wi, bi = w_in[l], b_in[l]
    o_qk, o_vo, o_if = 2 * dc, 2 * dc + 2 * dm, 2 * dc + 4 * dm
    w_if = wi[:, o_if:]
    b_if = bi[o_if:]
    wr_t = jnp.zeros((SUBLANES + n_experts, d), f32)
    wr_t = wr_t.at[:n_groups].set(router_group_w[l].T).at[SUBLANES:].set(router_expert_w[l].T)
    br = jnp.zeros((SUBLANES + n_experts, 1), f32)
    br = br.at[:n_groups, 0].set(router_group_b[l]).at[SUBLANES:, 0].set(router_expert_b[l])
    wr_hi = wr_t.astype(bf16)
    wr_lo = (wr_t - wr_hi.astype(f32)).astype(bf16)
    pad_rows = lambda a, n: jnp.pad(a, ((0, n - a.shape[0]), (0, 0)))
    idx = jnp.arange(ts)
    cidx = jnp.arange(CHUNK)
    return dict(
        gmix=norm_mix_g[l][None, :],
        wcv=wi[:, :o_qk].astype(bf16), bcv=bi[None, :o_qk],
        wqk=wi[:, o_qk:o_vo].astype(bf16), bqk=bi[None, o_qk:o_vo],
        wvo=wi[:, o_vo:o_if].astype(bf16), bvo=bi[None, o_vo:o_if],
        wif=jnp.pad(w_if, ((0, 0), (0, LANES - 2 * n_heads))).astype(bf16),
        bif=jnp.pad(b_if, (0, LANES - 2 * n_heads))[None, :],
        wift=pad_rows(w_if.T, SUBLANES).astype(bf16),
        bift=pad_rows(b_if[:, None], SUBLANES),
        cw=jnp.repeat(conv_w[l], SUBLANES, axis=0), cb=conv_b[l][None, :],
        lng=conv_ln_g[l][None, :], lnb=conv_ln_b[l][None, :],
        qkw=jnp.repeat(qk_conv_w[l], SUBLANES, axis=0), qkb=qk_conv_b[l][None, :],
        mng=mlstm_norm_g[l][None, :],
        wout=w_out[l].astype(bf16), gffn=norm_ffn_g[l][None, :],
        wrh=wr_hi, wrl=wr_lo, br=br,
        su=(idx[:, None] < idx[None, :]).astype(bf16),
        tril=(cidx[:, None] >= cidx[None, :]).astype(bf16),
    ), dict(n_heads=n_heads, conv_width=conv_w.shape[1], qk_width=qk_conv_w.shape[1],
            n_groups=n_groups, epg=n_experts // n_groups)


def kernel(x, norm_mix_g, w_in, b_in, conv_w, conv_b, conv_ln_g, conv_ln_b, qk_conv_w, qk_conv_b,
           mlstm_norm_g, w_out, norm_ffn_g, router_group_w, router_group_b, router_expert_w,
           router_expert_b, expert_w_gate, expert_w_up, expert_w_down, final_norm_g):
    bsz, seq, d = x.shape
    depth = w_in.shape[0]
    assert depth == 1, "the combine kernel fuses the final norm, so exactly one layer is supported"
    n_experts = router_expert_w.shape[-1]
    n_parts = N_PARTS if bsz % N_PARTS == 0 else 1
    pb = bsz // n_parts
    n_tok = pb * seq
    n_assign = n_tok * TOP_K
    n_blocks = -(-n_assign // EXPERT_BLOCK) + n_experts
    n_slots = n_blocks * EXPERT_BLOCK
    ts = min(SEQ_TILE, seq)
    nsub = d // PACK_COLS
    l = 0
    params, dims = _prepare_layer(
        l, norm_mix_g, w_in, b_in, conv_w, conv_b, conv_ln_g, conv_ln_b, qk_conv_w, qk_conv_b,
        mlstm_norm_g, w_out, norm_ffn_g, router_group_w, router_group_b, router_expert_w,
        router_expert_b, ts)
    wg, wu, wd = expert_w_gate[l], expert_w_up[l], expert_w_down[l]
    out = None
    prev_dest = None
    for part in range(n_parts):
        b0 = jnp.full((1,), part * pb, jnp.int32)
        if prev_dest is not None:
            b0 = b0 + jnp.minimum(prev_dest[:1, 0], 0)
        x1, h2_rows, route_i, gates_col, counts = _mixer_call(x, params, b0, pb, **dims)

        cnt = counts[:, 0].astype(jnp.int32)
        padded = (cnt + EXPERT_BLOCK - 1) // EXPERT_BLOCK * EXPERT_BLOCK
        padded_ends = jnp.cumsum(padded)
        padded_starts = padded_ends - padded
        route = jnp.transpose(route_i, (2, 0, 1, 3)).reshape(SUBLANES, n_tok)
        dest = jnp.stack([padded_starts[route[0]] + route[2],
                          padded_starts[route[1]] + route[3]], axis=0)
        block_start = jnp.arange(n_blocks, dtype=jnp.int32) * EXPERT_BLOCK
        block_expert = jnp.minimum(
            jnp.sum(block_start[:, None] >= padded_ends[None, :], axis=1), n_experts - 1).astype(jnp.int32)
        block_valid = jnp.clip((padded_starts + cnt)[block_expert] - block_start,
                               0, EXPERT_BLOCK).astype(jnp.int32)
        n_used = (padded_ends[-1:] // EXPERT_BLOCK).astype(jnp.int32)

        prev_dest = dest
        dest_w = _worker_index_table(dest, _sc_workers()[1])
        xs = _sc_dispatch(dest_w, h2_rows.reshape(n_tok, nsub, LANES), n_slots)
        ys = _expert_call(block_expert, block_valid, n_used, xs.reshape(n_slots * nsub, LANES),
                          wg, wu, wd, nsub)
        y_tok = _sc_gather(dest_w, ys.reshape(n_slots, nsub, LANES), n_tok)
        out = _combine_call(x1.reshape(n_tok, d), gates_col.reshape(n_tok, LANES),
                            final_norm_g[None, :], y_tok.reshape(TOP_K, n_tok * nsub, LANES), nsub,
                            out, part * n_tok, bsz * seq)
    return out.reshape(bsz, seq, d)
```

```python
import functools

import jax
import jax.numpy as jnp
from jax import lax
from jax.experimental import pallas as pl
from jax.experimental.pallas import tpu as pltpu
from jax.experimental.pallas import tpu_sc as plsc

EPS = 1e-6
LANES = 128
SUBLANES = 8
CHUNK = 128
TOP_K = 2
SEQ_TILE = 512
CONV_ROWS = 32
CONV_HALO = 32
QK_HALO = 8
EXPERT_BLOCK = 256
SC_CHUNK = 32
COMBINE_TILE = 512
N_PARTS = 2
PACK_COLS = 2 * LANES
VMEM_LIMIT = 56 * 1024 * 1024

_NT = (((1,), (1,)), ((), ()))
_TN = (((0,), (0,)), ((), ()))
_HI16 = 0xFFFF0000


def _dot(a, b):
    return jnp.dot(a, b, preferred_element_type=jnp.float32)


def _silu(x):
    return x * jax.nn.sigmoid(x)


def _log_sigmoid(x):
    return jnp.minimum(x, 0.0) - jnp.log1p(jnp.exp(-jnp.abs(x)))


def _split_bf16(x, parts):
    out = []
    for _ in range(parts):
        p = x.astype(jnp.bfloat16)
        out.append(p)
        x = x - p.astype(jnp.float32)
    return out


def _store_packed_rows(ref, v):
    n, d = v.shape
    half = d // 2
    nsub = half // LANES
    as_bits = lambda a: pltpu.bitcast(a.astype(jnp.bfloat16).astype(jnp.float32), jnp.uint32)
    word = (as_bits(v[:, :half]) >> 16) | (as_bits(v[:, half:]) & jnp.uint32(_HI16))
    for s in range(nsub):
        ref[pl.ds(s, n, stride=nsub), :] = word[:, s * LANES:(s + 1) * LANES]


def _load_packed_rows(ref, n, nsub, live=None):
    lo, hi = [], []
    for s in range(nsub):
        w = ref[pl.ds(s, n, stride=nsub), :]
        if live is not None:
            w = jnp.where(live, w, jnp.uint32(0))
        lo.append(pltpu.bitcast(w << 16, jnp.float32))
        hi.append(pltpu.bitcast(w & jnp.uint32(_HI16), jnp.float32))
    return lo + hi


def _mixer_kernel(b0_ref, x_ref, gmix_ref, wcv_ref, bcv_ref, wqk_ref, bqk_ref, wvo_ref, bvo_ref,
                  wif_ref, bif_ref, wift_ref, bift_ref, cw_ref, cb_ref, lng_ref, lnb_ref,
                  qkw_ref, qkb_ref, mng_ref, wout_ref, gffn_ref, wrh_ref, wrl_ref, br_ref,
                  su_ref, tril_ref,
                  x1_ref, h2_ref, ri_ref, gc_ref, cnt_ref,
                  uext, qkext, q_s, k_s, v_s, o_s, cn_s, m_s, y_s, cnt_s,
                  *, n_heads, conv_width, qk_width, n_groups, epg):
    ts, d = x_ref.shape
    dc = cw_ref.shape[1]
    dm = mng_ref.shape[1]
    dh = dm // n_heads
    n_chunks = ts // CHUNK
    n_experts = n_groups * epg
    b_idx = pl.program_id(0)
    t_idx = pl.program_id(1)

    @pl.when(t_idx == 0)
    def _reset_sequence_state():
        uext[0:CONV_HALO, :] = jnp.zeros((CONV_HALO, dc), jnp.float32)
        uext[CONV_HALO + ts:, :] = jnp.zeros((SUBLANES, dc), jnp.float32)
        qkext[0:QK_HALO, :] = jnp.zeros((QK_HALO, 2 * dm), jnp.float32)
        cn_s[...] = jnp.zeros(cn_s.shape, jnp.float32)
        m_s[...] = jnp.zeros(m_s.shape, jnp.float32)

    @pl.when((t_idx == 0) & (b_idx == 0))
    def _reset_counts():
        cnt_s[...] = jnp.zeros(cnt_s.shape, jnp.float32)

    x = x_ref[...]
    h = x * lax.rsqrt(jnp.mean(x * x, axis=-1, keepdims=True) + EPS) * gmix_ref[...]
    hb = h.astype(jnp.bfloat16)

    zc = _dot(hb, wcv_ref[...]) + bcv_ref[...]
    uext[CONV_HALO:CONV_HALO + ts, :] = zc[:, :dc] * jax.nn.sigmoid(zc[:, dc:])
    first = CONV_HALO - (conv_width - 1)
    sub = lax.broadcasted_iota(jnp.int32, (SUBLANES, dc), 0)
    n_grp = CONV_ROWS // SUBLANES
    for rb in range(ts // CONV_ROWS):
        r0 = rb * CONV_ROWS
        acc = jnp.zeros((CONV_ROWS, dc), jnp.float32) + cb_ref[...]
        for r in range(SUBLANES):
            part = None
            for j in range(conv_width):
                if (first + j) % SUBLANES != r:
                    continue
                a0 = r0 + (first + j) - r
                w8 = cw_ref[j * SUBLANES:(j + 1) * SUBLANES, :]
                term = jnp.concatenate([w8] * (n_grp + 1), axis=0) * uext[a0:a0 + CONV_ROWS + SUBLANES, :]
                part = term if part is None else part + term
            if part is None:
                continue
            if r == 0:
                acc = acc + part[:CONV_ROWS, :]
                continue
            rolled = [pltpu.roll(part[g * SUBLANES:(g + 1) * SUBLANES, :], SUBLANES - r, axis=0)
                      for g in range(n_grp + 1)]
            acc = acc + jnp.concatenate(
                [jnp.where(sub < SUBLANES - r, rolled[g], rolled[g + 1]) for g in range(n_grp)], axis=0)
        mu = jnp.mean(acc, axis=-1, keepdims=True)
        cen = acc - mu
        var = jnp.mean(cen * cen, axis=-1, keepdims=True)
        yn = cen * lax.rsqrt(var + EPS) * lng_ref[...] + lnb_ref[...]
        y_s[r0:r0 + CONV_ROWS, 0:dc] = _silu(yn).astype(jnp.bfloat16)
    uext[0:CONV_HALO, :] = uext[ts:ts + CONV_HALO, :]

    qkext[QK_HALO:QK_HALO + ts, :] = _dot(hb, wqk_ref[...]) + bqk_ref[...]
    qfirst = QK_HALO - (qk_width - 1)
    k_scale = dh ** -0.5
    for rb in range(ts // CONV_ROWS):
        r0 = rb * CONV_ROWS
        acc = jnp.zeros((CONV_ROWS, 2 * dm), jnp.float32) + qkb_ref[...]
        for j in range(qk_width):
            w8 = qkw_ref[j * SUBLANES:(j + 1) * SUBLANES, :]
            acc = acc + (jnp.concatenate([w8] * (CONV_ROWS // SUBLANES), axis=0)
                         * qkext[r0 + qfirst + j:r0 + qfirst + j + CONV_ROWS, :])
        act = _silu(acc)
        q_s[r0:r0 + CONV_ROWS, :] = act[:, :dm].astype(jnp.bfloat16)
        k_s[r0:r0 + CONV_ROWS, :] = (act[:, dm:] * k_scale).astype(jnp.bfloat16)
    qkext[0:QK_HALO, :] = qkext[ts:ts + QK_HALO, :]

    zvo = _dot(hb, wvo_ref[...]) + bvo_ref[...]
    v_s[...] = zvo[:, :dm].astype(jnp.bfloat16)
    o_s[...] = jax.nn.sigmoid(zvo[:, dm:])

    zif_col = _dot(hb, wif_ref[...]) + bif_ref[...]
    zif_row = lax.dot_general(wift_ref[...], hb, _NT,
                              preferred_element_type=jnp.float32) + bift_ref[...]
    logf_col = _log_sigmoid(zif_col)
    logf_row = _log_sigmoid(zif_row)
    lane = lax.broadcasted_iota(jnp.int32, (SUBLANES, CHUNK), 1)
    causal = (lax.broadcasted_iota(jnp.int32, (CHUNK, CHUNK), 0)
              >= lax.broadcasted_iota(jnp.int32, (CHUNK, CHUNK), 1))
    ones_blk = jnp.ones((CHUNK, dh), jnp.bfloat16)
    tril = tril_ref[...]

    for c in range(n_chunks):
        r0 = c * CHUNK
        b_row = logf_row[:, r0:r0 + CHUNK]
        shift = 1
        while shift < CHUNK:
            b_row = b_row + jnp.where(lane >= shift, pltpu.roll(b_row, shift, axis=1), 0.0)
            shift *= 2
        b_col = sum(_dot(tril, p) for p in _split_bf16(logf_col[r0:r0 + CHUNK, :], 3))
        i_col_all = zif_col[r0:r0 + CHUNK, :]
        i_row_all = zif_row[:, r0:r0 + CHUNK]
        for hd in range(n_heads):
            c0 = hd * dh
            b_c = b_col[:, n_heads + hd:n_heads + hd + 1]
            i_c = i_col_all[:, hd:hd + 1]
            b_r = b_row[n_heads + hd:n_heads + hd + 1, :]
            i_r = i_row_all[hd:hd + 1, :]
            m_prev = m_s[hd, 0:1, 0:1]
            q = q_s[r0:r0 + CHUNK, c0:c0 + dh]
            k = k_s[r0:r0 + CHUNK, c0:c0 + dh]
            vaug = jnp.concatenate([v_s[r0:r0 + CHUNK, c0:c0 + dh], ones_blk], axis=-1)
            cn = cn_s[hd]

            dmat = jnp.where(causal, b_c + (i_r - b_r), -jnp.inf)
            g = b_c + m_prev
            m_t = jnp.maximum(g, jnp.max(dmat, axis=-1, keepdims=True))
            w_intra = jnp.exp(dmat - m_t)
            w_inter = jnp.exp(g - m_t)
            s = lax.dot_general(q, k, _NT, preferred_element_type=jnp.float32) * w_intra
            nd = _dot(s.astype(jnp.bfloat16), vaug) + w_inter * _dot(q, cn.astype(jnp.bfloat16))
            hval = nd[:, :dh] / jnp.maximum(jnp.abs(nd[:, dh:]), jnp.exp(-m_t))

            b_last = b_r[:, CHUNK - 1:CHUNK]
            logw_c = b_last - b_c + i_c
            m_new = jnp.maximum(b_last + m_prev, jnp.max(logw_c, axis=0, keepdims=True))
            decay = jnp.exp(b_last + m_prev - m_new)
            kw = (k.astype(jnp.float32) * jnp.exp(logw_c - m_new)).astype(jnp.bfloat16)
            cn_s[hd] = decay * cn + lax.dot_general(kw, vaug, _TN,
                                                    preferred_element_type=jnp.float32)
            m_s[hd] = jnp.broadcast_to(m_new, m_s.shape[1:])

            hm = o_s[r0:r0 + CHUNK, c0:c0 + dh] * hval
            mu = jnp.mean(hm, axis=-1, keepdims=True)
            cen = hm - mu
            var = jnp.mean(cen * cen, axis=-1, keepdims=True)
            hn = cen * lax.rsqrt(var + EPS) * mng_ref[:, c0:c0 + dh]
            y_s[r0:r0 + CHUNK, dc + c0:dc + c0 + dh] = hn.astype(jnp.bfloat16)

    x1 = x + _dot(y_s[...], wout_ref[...])
    x1_ref[...] = x1
    h2 = x1 * lax.rsqrt(jnp.mean(x1 * x1, axis=-1, keepdims=True) + EPS) * gffn_ref[...]
    _store_packed_rows(h2_ref, h2)

    h2_hi, h2_lo = _split_bf16(h2, 2)
    lg = (lax.dot_general(wrh_ref[...], h2_hi, _NT, preferred_element_type=jnp.float32)
          + lax.dot_general(wrl_ref[...], h2_hi, _NT, preferred_element_type=jnp.float32)
          + lax.dot_general(wrh_ref[...], h2_lo, _NT, preferred_element_type=jnp.float32)
          + br_ref[...])
    gl = lg[0:n_groups, :]
    gidx = lax.broadcasted_iota(jnp.int32, (n_groups, ts), 0)
    gmax = jnp.max(gl, axis=0, keepdims=True)
    grp = jnp.min(jnp.where(gl == gmax, gidx, n_groups), axis=0, keepdims=True)
    p_grp = 1.0 / jnp.sum(jnp.exp(gl - gmax), axis=0, keepdims=True)
    in_group = jnp.zeros((epg, ts), jnp.float32)
    for gi in range(n_groups):
        in_group = jnp.where(grp == gi, lg[SUBLANES + gi * epg:SUBLANES + (gi + 1) * epg, :], in_group)
    eidx = lax.broadcasted_iota(jnp.int32, (epg, ts), 0)
    v1 = jnp.max(in_group, axis=0, keepdims=True)
    i1 = jnp.min(jnp.where(in_group == v1, eidx, epg), axis=0, keepdims=True)
    rest = jnp.where(eidx == i1, -jnp.inf, in_group)
    v2 = jnp.max(rest, axis=0, keepdims=True)
    i2 = jnp.min(jnp.where(rest == v2, eidx, epg), axis=0, keepdims=True)
    e21 = jnp.exp(v2 - v1)
    g1 = p_grp / (1.0 + e21)
    g2 = g1 * e21
    e1 = grp * epg + i1
    e2 = grp * epg + i2

    xidx = lax.broadcasted_iota(jnp.int32, (n_experts, ts), 0)
    sel1 = xidx == e1
    sel2 = xidx == e2
    onehot = jnp.where(sel1 | sel2, 1.0, 0.0).astype(jnp.bfloat16)
    before = cnt_s[:, 0:1] + _dot(onehot, su_ref[...])
    rank1 = jnp.sum(jnp.where(sel1, before, 0.0), axis=0, keepdims=True)
    rank2 = jnp.sum(jnp.where(sel2, before, 0.0), axis=0, keepdims=True)
    cnt_new = cnt_s[...] + jnp.sum(onehot.astype(jnp.float32), axis=1, keepdims=True)
    cnt_s[...] = cnt_new
    cnt_ref[...] = cnt_new

    zero_i = jnp.zeros((SUBLANES - 4, ts), jnp.int32)
    ri_ref[...] = jnp.concatenate(
        [e1, e2, rank1.astype(jnp.int32), rank2.astype(jnp.int32), zero_i], axis=0)
    gates_row = jnp.concatenate([g1, g2, jnp.zeros((LANES - 2, ts), jnp.float32)], axis=0)
    gc_ref[...] = gates_row.T


def _mixer_call(x, p, b0, bsz, *, n_heads, conv_width, qk_width, n_groups, epg):
    _, seq, d = x.shape
    ts = min(SEQ_TILE, seq)
    nt = seq // ts
    dc = p["cw"].shape[1]
    dm = p["mng"].shape[1]
    dh = dm // n_heads
    n_experts = n_groups * epg
    assert seq % ts == 0 and ts % CHUNK == 0 and dh == LANES and d % PACK_COLS == 0
    assert epg == SUBLANES and n_groups <= SUBLANES and conv_width - 1 <= CONV_HALO

    consts = [p[n] for n in ("gmix", "wcv", "bcv", "wqk", "bqk", "wvo", "bvo", "wif", "bif",
                             "wift", "bift", "cw", "cb", "lng", "lnb", "qkw", "qkb", "mng",
                             "wout", "gffn", "wrh", "wrl", "br", "su", "tril")]

    def const_spec(a):
        return pl.BlockSpec(a.shape, lambda b, t, b0r: (0,) * a.ndim)

    kern = functools.partial(_mixer_kernel, n_heads=n_heads, conv_width=conv_width,
                             qk_width=qk_width, n_groups=n_groups, epg=epg)
    grid_spec = pltpu.PrefetchScalarGridSpec(
        num_scalar_prefetch=1,
        grid=(bsz, nt),
        in_specs=([pl.BlockSpec((None, ts, d), lambda b, t, b0r: (b + b0r[0], t, 0))]
                  + [const_spec(a) for a in consts]),
        out_specs=[
            pl.BlockSpec((None, ts, d), lambda b, t, b0r: (b, t, 0)),
            pl.BlockSpec((None, ts * (d // PACK_COLS), LANES), lambda b, t, b0r: (b, t, 0)),
            pl.BlockSpec((None, None, SUBLANES, ts), lambda b, t, b0r: (b, t, 0, 0)),
            pl.BlockSpec((None, ts, LANES), lambda b, t, b0r: (b, t, 0)),
            pl.BlockSpec((n_experts, LANES), lambda b, t, b0r: (0, 0)),
        ],
        scratch_shapes=[
            pltpu.VMEM((CONV_HALO + ts + SUBLANES, dc), jnp.float32),
            pltpu.VMEM((QK_HALO + ts, 2 * dm), jnp.float32),
            pltpu.VMEM((ts, dm), jnp.bfloat16),
            pltpu.VMEM((ts, dm), jnp.bfloat16),
            pltpu.VMEM((ts, dm), jnp.bfloat16),
            pltpu.VMEM((ts, dm), jnp.float32),
            pltpu.VMEM((n_heads, dh, 2 * dh), jnp.float32),
            pltpu.VMEM((n_heads, SUBLANES, LANES), jnp.float32),
            pltpu.VMEM((ts, d), jnp.bfloat16),
            pltpu.VMEM((n_experts, LANES), jnp.float32),
        ],
    )
    return pl.pallas_call(
        kern,
        grid_spec=grid_spec,
        out_shape=[
            jax.ShapeDtypeStruct((bsz, seq, d), jnp.float32),
            jax.ShapeDtypeStruct((bsz, seq * (d // PACK_COLS), LANES), jnp.uint32),
            jax.ShapeDtypeStruct((bsz, nt, SUBLANES, ts), jnp.int32),
            jax.ShapeDtypeStruct((bsz, seq, LANES), jnp.float32),
            jax.ShapeDtypeStruct((n_experts, LANES), jnp.float32),
        ],
        compiler_params=pltpu.CompilerParams(
            dimension_semantics=("arbitrary", "arbitrary"), vmem_limit_bytes=VMEM_LIMIT),
        name="mixer_router",
    )(b0, x, *consts)


def _sc_workers():
    info = plsc.get_sparse_core_info()
    return info.num_cores, info.num_cores * info.num_subcores


def _worker_index_table(dest, n_workers):
    top_k, n_tok = dest.shape
    assert n_tok % (n_workers * SC_CHUNK) == 0
    return jnp.transpose(dest.reshape(top_k, n_workers, n_tok // (n_workers * SC_CHUNK), SC_CHUNK),
                         (1, 0, 2, 3))


def _sc_dispatch(dest_w, rows, n_slots):
    n_tok, nsub, l = rows.shape
    nw, top_k, n_chunks, ch = dest_w.shape
    tpw = n_tok // nw
    n_cores, n_workers = _sc_workers()
    assert nw == n_workers and tpw == n_chunks * ch

    @functools.partial(
        pl.kernel,
        mesh=plsc.VectorSubcoreMesh(core_axis_name="c", subcore_axis_name="s"),
        out_type=jax.ShapeDtypeStruct((n_slots, nsub, l), rows.dtype),
        scratch_types=[pltpu.VMEM((top_k, n_chunks, ch), jnp.int32),
                       pltpu.VMEM((ch, nsub, l), rows.dtype),
                       pltpu.SemaphoreType.DMA],
        name="sc_dispatch",
    )
    def run(dest_hbm, rows_hbm, out_hbm, idx_v, buf, sem):
        wid = lax.axis_index("s") * n_cores + lax.axis_index("c")
        pltpu.sync_copy(dest_hbm.at[wid], idx_v)

        @pl.loop(0, n_chunks)
        def _(j):
            pltpu.sync_copy(rows_hbm.at[pl.ds(wid * tpw + j * ch, ch)], buf)
            for k in range(top_k):
                pltpu.async_copy(buf, out_hbm.at[idx_v.at[k, j]], sem).wait()

    return run(dest_w, rows)


def _sc_gather(dest_w, ys, n_tok):
    _, nsub, l = ys.shape
    nw, top_k, n_chunks, ch = dest_w.shape
    tpw = n_tok // nw
    n_cores, n_workers = _sc_workers()
    assert nw == n_workers and tpw == n_chunks * ch

    @functools.partial(
        pl.kernel,
        mesh=plsc.VectorSubcoreMesh(core_axis_name="c", subcore_axis_name="s"),
        out_type=jax.ShapeDtypeStruct((top_k, n_tok, nsub, l), ys.dtype),
        scratch_types=[pltpu.VMEM((top_k, n_chunks, ch), jnp.int32),
                       pltpu.VMEM((ch, nsub, l), ys.dtype),
                       pltpu.SemaphoreType.DMA],
        name="sc_gather",
    )
    def run(dest_hbm, ys_hbm, out_hbm, idx_v, buf, sem):
        wid = lax.axis_index("s") * n_cores + lax.axis_index("c")
        pltpu.sync_copy(dest_hbm.at[wid], idx_v)

        @pl.loop(0, n_chunks)
        def _(j):
            for k in range(top_k):
                pltpu.async_copy(ys_hbm.at[idx_v.at[k, j]], buf, sem).wait()
                pltpu.sync_copy(buf, out_hbm.at[k, pl.ds(wid * tpw + j * ch, ch)])

    return run(dest_w, ys)


def _expert_kernel(be_ref, valid_ref, nused_ref, next_ref, run_ref,
                   xs_ref, wg_hbm, wu_hbm, wd_hbm, ys_ref,
                   wg_f, wu_f, wd_f, wg_b, wu_b, wd_b, sems, *, nsub):
    i = pl.program_id(0)
    blk = xs_ref.shape[0] // nsub
    l = xs_ref.shape[1]
    slot = run_ref[i] % 2

    def weight_copies(e, s):
        return [pltpu.make_async_copy(src.at[e], dst.at[s], sems.at[s])
                for src, dst in ((wg_hbm, wg_f), (wu_hbm, wu_f), (wd_hbm, wd_f))]

    @pl.when(i == 0)
    def _fetch_first_weights():
        for c in weight_copies(be_ref[0], 0):
            c.start()

    @pl.when((i == 0) | (be_ref[i] != be_ref[jnp.maximum(i - 1, 0)]))
    def _new_expert_weights():
        for c in weight_copies(be_ref[i], slot):
            c.wait()
        wg_b[...] = wg_f[slot].astype(jnp.bfloat16)
        wu_b[...] = wu_f[slot].astype(jnp.bfloat16)
        wd_b[...] = wd_f[slot].astype(jnp.bfloat16)

        @pl.when(next_ref[i] >= 0)
        def _fetch_next_run_weights():
            for c in weight_copies(next_ref[i], 1 - slot):
                c.start()

    @pl.when(i < nused_ref[0])
    def _compute():
        live = lax.broadcasted_iota(jnp.int32, (blk, l), 0) < valid_ref[i]
        xb = jnp.concatenate(
            [c.astype(jnp.bfloat16) for c in _load_packed_rows(xs_ref, blk, nsub, live)], axis=-1)
        a = _dot(xb, wg_b[...])
        u = _dot(xb, wu_b[...])
        _store_packed_rows(ys_ref, _dot((_silu(a) * u).astype(jnp.bfloat16), wd_b[...]))

    @pl.when(i >= nused_ref[0])
    def _unused_block():
        ys_ref[...] = jnp.zeros(ys_ref.shape, ys_ref.dtype)


def _expert_call(block_expert, block_valid, n_used, xs, wg, wu, wd, nsub):
    rows, l = xs.shape
    n_blocks = rows // (EXPERT_BLOCK * nsub)
    n_experts, d, de = wg.shape

    ids = jnp.arange(n_experts, dtype=jnp.int32)
    present = jnp.any(block_expert[:, None] == ids[None, :], axis=0)
    later = (ids[None, :] > ids[:, None]) & present[None, :]
    next_of = jnp.min(jnp.where(later, ids[None, :], n_experts), axis=1)
    next_of = jnp.where(next_of == n_experts, -1, next_of).astype(jnp.int32)
    run_of = (jnp.cumsum(present.astype(jnp.int32)) - 1).astype(jnp.int32)
    block_next = next_of[block_expert]
    block_run = run_of[block_expert]

    def row_map(i, be, bv, nu, nx, rn):
        return (i, 0)

    hbm = pl.BlockSpec(memory_space=pl.ANY)
    return pl.pallas_call(
        functools.partial(_expert_kernel, nsub=nsub),
        grid_spec=pltpu.PrefetchScalarGridSpec(
            num_scalar_prefetch=5,
            grid=(n_blocks,),
            in_specs=[pl.BlockSpec((EXPERT_BLOCK * nsub, l), row_map), hbm, hbm, hbm],
            out_specs=pl.BlockSpec((EXPERT_BLOCK * nsub, l), row_map),
            scratch_shapes=[pltpu.VMEM((2, d, de), wg.dtype), pltpu.VMEM((2, d, de), wu.dtype),
                            pltpu.VMEM((2, de, d), wd.dtype),
                            pltpu.VMEM((d, de), jnp.bfloat16), pltpu.VMEM((d, de), jnp.bfloat16),
                            pltpu.VMEM((de, d), jnp.bfloat16),
                            pltpu.SemaphoreType.DMA((2,))],
        ),
        out_shape=jax.ShapeDtypeStruct((rows, l), xs.dtype),
        compiler_params=pltpu.CompilerParams(
            dimension_semantics=("arbitrary",), vmem_limit_bytes=VMEM_LIMIT),
        name="experts",
    )(block_expert, block_valid, n_used, block_next, block_run, xs, wg, wu, wd)


def _combine_kernel(x1_ref, gc_ref, gfin_ref, y_ref, out_ref, *, nsub):
    tc, d = x1_ref.shape
    g1 = gc_ref[:, 0:1]
    g2 = gc_ref[:, 1:2]
    pieces = []
    ssq = jnp.zeros((tc, 1), jnp.float32)
    y1_cols = _load_packed_rows(y_ref.at[0], tc, nsub)
    y2_cols = _load_packed_rows(y_ref.at[1], tc, nsub)
    for j, (y1, y2) in enumerate(zip(y1_cols, y2_cols)):
        z = x1_ref[:, j * LANES:(j + 1) * LANES] + (g1 * y1 + g2 * y2)
        pieces.append(z)
        ssq = ssq + jnp.sum(z * z, axis=-1, keepdims=True)
    scale = lax.rsqrt(ssq / d + EPS)
    for j, z in enumerate(pieces):
        out_ref[:, j * LANES:(j + 1) * LANES] = z * scale * gfin_ref[:, j * LANES:(j + 1) * LANES]


def _combine_call(x1, gates_col, gfin, y_tok, nsub, out_prev, tok0, n_tok_total):
    n_tok, d = x1.shape
    tc = min(COMBINE_TILE, n_tok)
    assert n_tok % tc == 0 and tok0 % tc == 0
    l = y_tok.shape[-1]
    in_specs = [
        pl.BlockSpec((tc, d), lambda i: (i, 0)),
        pl.BlockSpec((tc, LANES), lambda i: (i, 0)),
        pl.BlockSpec((1, d), lambda i: (0, 0)),
        pl.BlockSpec((TOP_K, tc * nsub, l), lambda i: (0, i, 0)),
    ]
    args = [x1, gates_col, gfin, y_tok]
    aliases = {}
    kern = functools.partial(_combine_kernel, nsub=nsub)
    if out_prev is not None:
        in_specs.append(pl.BlockSpec(memory_space=pl.ANY))
        args.append(out_prev)
        aliases = {len(args) - 1: 0}
        kern = lambda x1_ref, gc_ref, gfin_ref, y_ref, prev_ref, out_ref: _combine_kernel(
            x1_ref, gc_ref, gfin_ref, y_ref, out_ref, nsub=nsub)
    return pl.pallas_call(
        kern,
        grid=(n_tok // tc,),
        in_specs=in_specs,
        out_specs=pl.BlockSpec((tc, d), lambda i: (i + tok0 // tc, 0)),
        out_shape=jax.ShapeDtypeStruct((n_tok_total, d), jnp.float32),
        input_output_aliases=aliases,
        compiler_params=pltpu.CompilerParams(
            dimension_semantics=("arbitrary",), vmem_limit_bytes=VMEM_LIMIT),
        name="combine",
    )(*args)


def _prepare_layer(l, norm_mix_g, w_in, b_in, conv_w, conv_b, conv_ln_g, conv_ln_b, qk_conv_w,
                   qk_conv_b, mlstm_norm_g, w_out, norm_ffn_g, router_group_w, router_group_b,
                   router_expert_w, router_expert_b, ts):
    f32, bf16 = jnp.float32, jnp.bfloat16
    d = w_in.shape[1]
    dc = conv_w.shape[-1]
    dm = mlstm_norm_g.shape[-1]
    n_heads = (w_in.shape[-1] - 2 * dc - 4 * dm) // 2
    n_groups = router_group_w.shape[-1]
    n_experts = router_expert_w.shape[-1]
    wi, bi = w_in[l], b_in[l]
    o_qk, o_vo, o_if = 2 * dc, 2 * dc + 2 * dm, 2 * dc + 4 * dm
    w_if = wi[:, o_if:]
    b_if = bi[o_if:]
    wr_t = jnp.zeros((SUBLANES + n_experts, d), f32)
    wr_t = wr_t.at[:n_groups].set(router_group_w[l].T).at[SUBLANES:].set(router_expert_w[l].T)
    br = jnp.zeros((SUBLANES + n_experts, 1), f32)
    br = br.at[:n_groups, 0].set(router_group_b[l]).at[SUBLANES:, 0].set(router_expert_b[l])
    wr_hi = wr_t.astype(bf16)
    wr_lo = (wr_t - wr_hi.astype(f32)).astype(bf16)
    pad_rows = lambda a, n: jnp.pad(a, ((0, n - a.shape[0]), (0, 0)))
    idx = jnp.arange(ts)
    cidx = jnp.arange(CHUNK)
    return dict(
        gmix=norm_mix_g[l][None, :],
        wcv=wi[:, :o_qk].astype(bf16), bcv=bi[None, :o_qk],
        wqk=wi[:, o_qk:o_vo].astype(bf16), bqk=bi[None, o_qk:o_vo],
        wvo=wi[:, o_vo:o_if].astype(bf16), bvo=bi[None, o_vo:o_if],
        wif=jnp.pad(w_if, ((0, 0), (0, LANES - 2 * n_heads))).astype(bf16),
        bif=jnp.pad(b_if, (0, LANES - 2 * n_heads))[None, :],
        wift=pad_rows(w_if.T, SUBLANES).astype(bf16),
        bift=pad_rows(b_if[:, None], SUBLANES),
        cw=jnp.repeat(conv_w[l], SUBLANES, axis=0), cb=conv_b[l][None, :],
        lng=conv_ln_g[l][None, :], lnb=conv_ln_b[l][None, :],
        qkw=jnp.repeat(qk_conv_w[l], SUBLANES, axis=0), qkb=qk_conv_b[l][None, :],
        mng=mlstm_norm_g[l][None, :],
        wout=w_out[l].astype(bf16), gffn=norm_ffn_g[l][None, :],
        wrh=wr_hi, wrl=wr_lo, br=br,
        su=(idx[:, None] < idx[None, :]).astype(bf16),
        tril=(cidx[:, None] >= cidx[None, :]).astype(bf16),
    ), dict(n_heads=n_heads, conv_width=conv_w.shape[1], qk_width=qk_conv_w.shape[1],
            n_groups=n_groups, epg=n_experts // n_groups)


def kernel(x, norm_mix_g, w_in, b_in, conv_w, conv_b, conv_ln_g, conv_ln_b, qk_conv_w, qk_conv_b,
           mlstm_norm_g, w_out, norm_ffn_g, router_group_w, router_group_b, router_expert_w,
           router_expert_b, expert_w_gate, expert_w_up, expert_w_down, final_norm_g):
    bsz, seq, d = x.shape
    depth = w_in.shape[0]
    assert depth == 1, "the combine kernel fuses the final norm, so exactly one layer is supported"
    n_experts = router_expert_w.shape[-1]
    n_parts = N_PARTS if bsz % N_PARTS == 0 else 1
    pb = bsz // n_parts
    n_tok = pb * seq
    n_assign = n_tok * TOP_K
    n_blocks = -(-n_assign // EXPERT_BLOCK) + n_experts
    n_slots = n_blocks * EXPERT_BLOCK
    ts = min(SEQ_TILE, seq)
    nsub = d // PACK_COLS
    l = 0
    params, dims = _prepare_layer(
        l, norm_mix_g, w_in, b_in, conv_w, conv_b, conv_ln_g, conv_ln_b, qk_conv_w, qk_conv_b,
        mlstm_norm_g, w_out, norm_ffn_g, router_group_w, router_group_b, router_expert_w,
        router_expert_b, ts)
    wg, wu, wd = expert_w_gate[l], expert_w_up[l], expert_w_down[l]
    out = None
    prev_dest = None
    for part in range(n_parts):
        b0 = jnp.full((1,), part * pb, jnp.int32)
        if prev_dest is not None:
            b0 = b0 + jnp.minimum(prev_dest[:1, 0], 0)
        x1, h2_rows, route_i, gates_col, counts = _mixer_call(x, params, b0, pb, **dims)

        cnt = counts[:, 0].astype(jnp.int32)
        padded = (cnt + EXPERT_BLOCK - 1) // EXPERT_BLOCK * EXPERT_BLOCK
        padded_ends = jnp.cumsum(padded)
        padded_starts = padded_ends - padded
        route = jnp.transpose(route_i, (2, 0, 1, 3)).reshape(SUBLANES, n_tok)
        dest = jnp.stack([padded_starts[route[0]] + route[2],
                          padded_starts[route[1]] + route[3]], axis=0)
        block_start = jnp.arange(n_blocks, dtype=jnp.int32) * EXPERT_BLOCK
        block_expert = jnp.minimum(
            jnp.sum(block_start[:, None] >= padded_ends[None, :], axis=1), n_experts - 1).astype(jnp.int32)
        block_valid = jnp.clip((padded_starts + cnt)[block_expert] - block_start,
                               0, EXPERT_BLOCK).astype(jnp.int32)
        n_used = (padded_ends[-1:] // EXPERT_BLOCK).astype(jnp.int32)

        prev_dest = dest
        dest_w = _worker_index_table(dest, _sc_workers()[1])
        xs = _sc_dispatch(dest_w, h2_rows.reshape(n_tok, nsub, LANES), n_slots)
        ys = _expert_call(block_expert, block_valid, n_used, xs.reshape(n_slots * nsub, LANES),
                          wg, wu, wd, nsub)
        y_tok = _sc_gather(dest_w, ys.reshape(n_slots, nsub, LANES), n_tok)
        out = _combine_call(x1.reshape(n_tok, d), gates_col.reshape(n_tok, LANES),
                            final_norm_g[None, :], y_tok.reshape(TOP_K, n_tok * nsub, LANES), nsub,
                            out, part * n_tok, bsz * seq)
    return out.reshape(bsz, seq, d)
```

```python
import functools

import jax
import jax.numpy as jnp
from jax import lax
from jax.experimental import pallas as pl
from jax.experimental.pallas import tpu as pltpu
from jax.experimental.pallas import tpu_sc as plsc

EPS = 1e-6
LANES = 128
SUBLANES = 8
CHUNK = 128
TOP_K = 2
SEQ_TILE = 512
CONV_ROWS = 32
CONV_HALO = 32
QK_HALO = 8
EXPERT_BLOCK = 256
SC_CHUNK = 128
COMBINE_TILE = 512
N_PARTS = 2
PACK_COLS = 2 * LANES
VMEM_LIMIT = 56 * 1024 * 1024

_NT = (((1,), (1,)), ((), ()))
_TN = (((0,), (0,)), ((), ()))
_HI16 = 0xFFFF0000


def _dot(a, b):
    return jnp.dot(a, b, preferred_element_type=jnp.float32)


def _silu(x):
    return x * jax.nn.sigmoid(x)


def _log_sigmoid(x):
    return jnp.minimum(x, 0.0) - jnp.log1p(jnp.exp(-jnp.abs(x)))


def _split_bf16(x, parts):
    out = []
    for _ in range(parts):
        p = x.astype(jnp.bfloat16)
        out.append(p)
        x = x - p.astype(jnp.float32)
    return out


def _store_packed_rows(ref, v):
    n, d = v.shape
    half = d // 2
    nsub = half // LANES
    as_bits = lambda a: pltpu.bitcast(a.astype(jnp.bfloat16).astype(jnp.float32), jnp.uint32)
    word = (as_bits(v[:, :half]) >> 16) | (as_bits(v[:, half:]) & jnp.uint32(_HI16))
    for s in range(nsub):
        ref[pl.ds(s, n, stride=nsub), :] = word[:, s * LANES:(s + 1) * LANES]


def _load_packed_rows(ref, n, nsub, live=None):
    lo, hi = [], []
    for s in range(nsub):
        w = ref[pl.ds(s, n, stride=nsub), :]
        if live is not None:
            w = jnp.where(live, w, jnp.uint32(0))
        lo.append(pltpu.bitcast(w << 16, jnp.float32))
        hi.append(pltpu.bitcast(w & jnp.uint32(_HI16), jnp.float32))
    return lo + hi


def _mixer_kernel(b0_ref, x_ref, gmix_ref, wcv_ref, bcv_ref, wqk_ref, bqk_ref, wvo_ref, bvo_ref,
                  wif_ref, bif_ref, wift_ref, bift_ref, cw_ref, cb_ref, lng_ref, lnb_ref,
                  qkw_ref, qkb_ref, mng_ref, wout_ref, gffn_ref, wrh_ref, wrl_ref, br_ref,
                  su_ref, tril_ref,
                  x1_ref, h2_ref, ri_ref, gc_ref, cnt_ref,
                  uext, qkext, q_s, k_s, v_s, o_s, cn_s, m_s, y_s, cnt_s,
                  *, n_heads, conv_width, qk_width, n_groups, epg):
    ts, d = x_ref.shape
    dc = cw_ref.shape[1]
    dm = mng_ref.shape[1]
    dh = dm // n_heads
    n_chunks = ts // CHUNK
    n_experts = n_groups * epg
    b_idx = pl.program_id(0)
    t_idx = pl.program_id(1)

    @pl.when(t_idx == 0)
    def _reset_sequence_state():
        uext[0:CONV_HALO, :] = jnp.zeros((CONV_HALO, dc), jnp.float32)
        uext[CONV_HALO + ts:, :] = jnp.zeros((SUBLANES, dc), jnp.float32)
        qkext[0:QK_HALO, :] = jnp.zeros((QK_HALO, 2 * dm), jnp.float32)
        cn_s[...] = jnp.zeros(cn_s.shape, jnp.float32)
        m_s[...] = jnp.zeros(m_s.shape, jnp.float32)

    @pl.when((t_idx == 0) & (b_idx == 0))
    def _reset_counts():
        cnt_s[...] = jnp.zeros(cnt_s.shape, jnp.float32)

    x = x_ref[...]
    h = x * lax.rsqrt(jnp.mean(x * x, axis=-1, keepdims=True) + EPS) * gmix_ref[...]
    hb = h.astype(jnp.bfloat16)

    zc = _dot(hb, wcv_ref[...]) + bcv_ref[...]
    uext[CONV_HALO:CONV_HALO + ts, :] = zc[:, :dc] * jax.nn.sigmoid(zc[:, dc:])
    first = CONV_HALO - (conv_width - 1)
    sub = lax.broadcasted_iota(jnp.int32, (SUBLANES, dc), 0)
    n_grp = CONV_ROWS // SUBLANES
    for rb in range(ts // CONV_ROWS):
        r0 = rb * CONV_ROWS
        acc = jnp.zeros((CONV_ROWS, dc), jnp.float32) + cb_ref[...]
        for r in range(SUBLANES):
            part = None
            for j in range(conv_width):
                if (first + j) % SUBLANES != r:
                    continue
                a0 = r0 + (first + j) - r
                w8 = cw_ref[j * SUBLANES:(j + 1) * SUBLANES, :]
                term = jnp.concatenate([w8] * (n_grp + 1), axis=0) * uext[a0:a0 + CONV_ROWS + SUBLANES, :]
                part = term if part is None else part + term
            if part is None:
                continue
            if r == 0:
                acc = acc + part[:CONV_ROWS, :]
                continue
            rolled = [pltpu.roll(part[g * SUBLANES:(g + 1) * SUBLANES, :], SUBLANES - r, axis=0)
                      for g in range(n_grp + 1)]
            acc = acc + jnp.concatenate(
                [jnp.where(sub < SUBLANES - r, rolled[g], rolled[g + 1]) for g in range(n_grp)], axis=0)
        mu = jnp.mean(acc, axis=-1, keepdims=True)
        cen = acc - mu
        var = jnp.mean(cen * cen, axis=-1, keepdims=True)
        yn = cen * lax.rsqrt(var + EPS) * lng_ref[...] + lnb_ref[...]
        y_s[r0:r0 + CONV_ROWS, 0:dc] = _silu(yn).astype(jnp.bfloat16)
    uext[0:CONV_HALO, :] = uext[ts:ts + CONV_HALO, :]

    qkext[QK_HALO:QK_HALO + ts, :] = _dot(hb, wqk_ref[...]) + bqk_ref[...]
    qfirst = QK_HALO - (qk_width - 1)
    k_scale = dh ** -0.5
    for rb in range(ts // CONV_ROWS):
        r0 = rb * CONV_ROWS
        acc = jnp.zeros((CONV_ROWS, 2 * dm), jnp.float32) + qkb_ref[...]
        for j in range(qk_width):
            w8 = qkw_ref[j * SUBLANES:(j + 1) * SUBLANES, :]
            acc = acc + (jnp.concatenate([w8] * (CONV_ROWS // SUBLANES), axis=0)
                         * qkext[r0 + qfirst + j:r0 + qfirst + j + CONV_ROWS, :])
        act = _silu(acc)
        q_s[r0:r0 + CONV_ROWS, :] = act[:, :dm].astype(jnp.bfloat16)
        k_s[r0:r0 + CONV_ROWS, :] = (act[:, dm:] * k_scale).astype(jnp.bfloat16)
    qkext[0:QK_HALO, :] = qkext[ts:ts + QK_HALO, :]

    zvo = _dot(hb, wvo_ref[...]) + bvo_ref[...]
    v_s[...] = zvo[:, :dm].astype(jnp.bfloat16)
    o_s[...] = jax.nn.sigmoid(zvo[:, dm:])

    zif_col = _dot(hb, wif_ref[...]) + bif_ref[...]
    zif_row = lax.dot_general(wift_ref[...], hb, _NT,
                              preferred_element_type=jnp.float32) + bift_ref[...]
    logf_col = _log_sigmoid(zif_col)
    logf_row = _log_sigmoid(zif_row)
    lane = lax.broadcasted_iota(jnp.int32, (SUBLANES, CHUNK), 1)
    causal = (lax.broadcasted_iota(jnp.int32, (CHUNK, CHUNK), 0)
              >= lax.broadcasted_iota(jnp.int32, (CHUNK, CHUNK), 1))
    ones_blk = jnp.ones((CHUNK, dh), jnp.bfloat16)
    tril = tril_ref[...]

    for c in range(n_chunks):
        r0 = c * CHUNK
        b_row = logf_row[:, r0:r0 + CHUNK]
        shift = 1
        while shift < CHUNK:
            b_row = b_row + jnp.where(lane >= shift, pltpu.roll(b_row, shift, axis=1), 0.0)
            shift *= 2
        b_col = sum(_dot(tril, p) for p in _split_bf16(logf_col[r0:r0 + CHUNK, :], 3))
        i_col_all = zif_col[r0:r0 + CHUNK, :]
        i_row_all = zif_row[:, r0:r0 + CHUNK]
        for hd in range(n_heads):
            c0 = hd * dh
            b_c = b_col[:, n_heads + hd:n_heads + hd + 1]
            i_c = i_col_all[:, hd:hd + 1]
            b_r = b_row[n_heads + hd:n_heads + hd + 1, :]
            i_r = i_row_all[hd:hd + 1, :]
            m_prev = m_s[hd, 0:1, 0:1]
            q = q_s[r0:r0 + CHUNK, c0:c0 + dh]
            k = k_s[r0:r0 + CHUNK, c0:c0 + dh]
            vaug = jnp.concatenate([v_s[r0:r0 + CHUNK, c0:c0 + dh], ones_blk], axis=-1)
            cn = cn_s[hd]

            dmat = jnp.where(causal, b_c + (i_r - b_r), -jnp.inf)
            g = b_c + m_prev
            m_t = jnp.maximum(g, jnp.max(dmat, axis=-1, keepdims=True))
            w_intra = jnp.exp(dmat - m_t)
            w_inter = jnp.exp(g - m_t)
            s = lax.dot_general(q, k, _NT, preferred_element_type=jnp.float32) * w_intra
            nd = _dot(s.astype(jnp.bfloat16), vaug) + w_inter * _dot(q, cn.astype(jnp.bfloat16))
            hval = nd[:, :dh] / jnp.maximum(jnp.abs(nd[:, dh:]), jnp.exp(-m_t))

            b_last = b_r[:, CHUNK - 1:CHUNK]
            logw_c = b_last - b_c + i_c
            m_new = jnp.maximum(b_last + m_prev, jnp.max(logw_c, axis=0, keepdims=True))
            decay = jnp.exp(b_last + m_prev - m_new)
            kw = (k.astype(jnp.float32) * jnp.exp(logw_c - m_new)).astype(jnp.bfloat16)
            cn_s[hd] = decay * cn + lax.dot_general(kw, vaug, _TN,
                                                    preferred_element_type=jnp.float32)
            m_s[hd] = jnp.broadcast_to(m_new, m_s.shape[1:])

            hm = o_s[r0:r0 + CHUNK, c0:c0 + dh] * hval
            mu = jnp.mean(hm, axis=-1, keepdims=True)
            cen = hm - mu
            var = jnp.mean(cen * cen, axis=-1, keepdims=True)
            hn = cen * lax.rsqrt(var + EPS) * mng_ref[:, c0:c0 + dh]
            y_s[r0:r0 + CHUNK, dc + c0:dc + c0 + dh] = hn.astype(jnp.bfloat16)

    x1 = x + _dot(y_s[...], wout_ref[...])
    x1_ref[...] = x1
    h2 = x1 * lax.rsqrt(jnp.mean(x1 * x1, axis=-1, keepdims=True) + EPS) * gffn_ref[...]
    _store_packed_rows(h2_ref, h2)

    h2_hi, h2_lo = _split_bf16(h2, 2)
    lg = (lax.dot_general(wrh_ref[...], h2_hi, _NT, preferred_element_type=jnp.float32)
          + lax.dot_general(wrl_ref[...], h2_hi, _NT, preferred_element_type=jnp.float32)
          + lax.dot_general(wrh_ref[...], h2_lo, _NT, preferred_element_type=jnp.float32)
          + br_ref[...])
    gl = lg[0:n_groups, :]
    gidx = lax.broadcasted_iota(jnp.int32, (n_groups, ts), 0)
    gmax = jnp.max(gl, axis=0, keepdims=True)
    grp = jnp.min(jnp.where(gl == gmax, gidx, n_groups), axis=0, keepdims=True)
    p_grp = 1.0 / jnp.sum(jnp.exp(gl - gmax), axis=0, keepdims=True)
    in_group = jnp.zeros((epg, ts), jnp.float32)
    for gi in range(n_groups):
        in_group = jnp.where(grp == gi, lg[SUBLANES + gi * epg:SUBLANES + (gi + 1) * epg, :], in_group)
    eidx = lax.broadcasted_iota(jnp.int32, (epg, ts), 0)
    v1 = jnp.max(in_group, axis=0, keepdims=True)
    i1 = jnp.min(jnp.where(in_group == v1, eidx, epg), axis=0, keepdims=True)
    rest = jnp.where(eidx == i1, -jnp.inf, in_group)
    v2 = jnp.max(rest, axis=0, keepdims=True)
    i2 = jnp.min(jnp.where(rest == v2, eidx, epg), axis=0, keepdims=True)
    e21 = jnp.exp(v2 - v1)
    g1 = p_grp / (1.0 + e21)
    g2 = g1 * e21
    e1 = grp * epg + i1
    e2 = grp * epg + i2

    xidx = lax.broadcasted_iota(jnp.int32, (n_experts, ts), 0)
    sel1 = xidx == e1
    sel2 = xidx == e2
    onehot = jnp.where(sel1 | sel2, 1.0, 0.0).astype(jnp.bfloat16)
    before = cnt_s[:, 0:1] + _dot(onehot, su_ref[...])
    rank1 = jnp.sum(jnp.where(sel1, before, 0.0), axis=0, keepdims=True)
    rank2 = jnp.sum(jnp.where(sel2, before, 0.0), axis=0, keepdims=True)
    cnt_new = cnt_s[...] + jnp.sum(onehot.astype(jnp.float32), axis=1, keepdims=True)
    cnt_s[...] = cnt_new
    cnt_ref[...] = cnt_new

    zero_i = jnp.zeros((SUBLANES - 4, ts), jnp.int32)
    ri_ref[...] = jnp.concatenate(
        [e1, e2, rank1.astype(jnp.int32), rank2.astype(jnp.int32), zero_i], axis=0)
    gates_row = jnp.concatenate([g1, g2, jnp.zeros((LANES - 2, ts), jnp.float32)], axis=0)
    gc_ref[...] = gates_row.T


def _mixer_call(x, p, b0, bsz, *, n_heads, conv_width, qk_width, n_groups, epg):
    _, seq, d = x.shape
    ts = min(SEQ_TILE, seq)
    nt = seq // ts
    dc = p["cw"].shape[1]
    dm = p["mng"].shape[1]
    dh = dm // n_heads
    n_experts = n_groups * epg
    assert seq % ts == 0 and ts % CHUNK == 0 and dh == LANES and d % PACK_COLS == 0
    assert epg == SUBLANES and n_groups <= SUBLANES and conv_width - 1 <= CONV_HALO

    consts = [p[n] for n in ("gmix", "wcv", "bcv", "wqk", "bqk", "wvo", "bvo", "wif", "bif",
                             "wift", "bift", "cw", "cb", "lng", "lnb", "qkw", "qkb", "mng",
                             "wout", "gffn", "wrh", "wrl", "br", "su", "tril")]

    def const_spec(a):
        return pl.BlockSpec(a.shape, lambda b, t, b0r: (0,) * a.ndim)

    kern = functools.partial(_mixer_kernel, n_heads=n_heads, conv_width=conv_width,
                             qk_width=qk_width, n_groups=n_groups, epg=epg)
    grid_spec = pltpu.PrefetchScalarGridSpec(
        num_scalar_prefetch=1,
        grid=(bsz, nt),
        in_specs=([pl.BlockSpec((None, ts, d), lambda b, t, b0r: (b + b0r[0], t, 0))]
                  + [const_spec(a) for a in consts]),
        out_specs=[
            pl.BlockSpec((None, ts, d), lambda b, t, b0r: (b, t, 0)),
            pl.BlockSpec((None, ts * (d // PACK_COLS), LANES), lambda b, t, b0r: (b, t, 0)),
            pl.BlockSpec((None, None, SUBLANES, ts), lambda b, t, b0r: (b, t, 0, 0)),
            pl.BlockSpec((None, ts, LANES), lambda b, t, b0r: (b, t, 0)),
            pl.BlockSpec((n_experts, LANES), lambda b, t, b0r: (0, 0)),
        ],
        scratch_shapes=[
            pltpu.VMEM((CONV_HALO + ts + SUBLANES, dc), jnp.float32),
            pltpu.VMEM((QK_HALO + ts, 2 * dm), jnp.float32),
            pltpu.VMEM((ts, dm), jnp.bfloat16),
            pltpu.VMEM((ts, dm), jnp.bfloat16),
            pltpu.VMEM((ts, dm), jnp.bfloat16),
            pltpu.VMEM((ts, dm), jnp.float32),
            pltpu.VMEM((n_heads, dh, 2 * dh), jnp.float32),
            pltpu.VMEM((n_heads, SUBLANES, LANES), jnp.float32),
            pltpu.VMEM((ts, d), jnp.bfloat16),
            pltpu.VMEM((n_experts, LANES), jnp.float32),
        ],
    )
    return pl.pallas_call(
        kern,
        grid_spec=grid_spec,
        out_shape=[
            jax.ShapeDtypeStruct((bsz, seq, d), jnp.float32),
            jax.ShapeDtypeStruct((bsz, seq * (d // PACK_COLS), LANES), jnp.uint32),
            jax.ShapeDtypeStruct((bsz, nt, SUBLANES, ts), jnp.int32),
            jax.ShapeDtypeStruct((bsz, seq, LANES), jnp.float32),
            jax.ShapeDtypeStruct((n_experts, LANES), jnp.float32),
        ],
        compiler_params=pltpu.CompilerParams(
            dimension_semantics=("arbitrary", "arbitrary"), vmem_limit_bytes=VMEM_LIMIT),
        name="mixer_router",
    )(b0, x, *consts)


def _sc_workers():
    info = plsc.get_sparse_core_info()
    return info.num_cores, info.num_cores * info.num_subcores


def _dest_kernel(pstart_ref, route_ref, dest_ref, *, n_experts):
    tiles, _, ts = route_ref.shape
    per = ts // SC_CHUNK
    for k in range(TOP_K):
        e = route_ref[:, k, :]
        rank = route_ref[:, TOP_K + k, :]
        start = jnp.zeros((tiles, ts), jnp.int32)
        for ex in range(n_experts):
            start = jnp.where(e == ex, pstart_ref[ex], start)
        slot = start + rank
        for j in range(per):
            dest_ref[k, pl.ds(j, tiles, stride=per), :] = slot[:, j * SC_CHUNK:(j + 1) * SC_CHUNK]


def _dest_call(padded_starts, route):
    tiles, _, ts = route.shape
    assert ts % SC_CHUNK == 0 and SC_CHUNK == LANES
    n_rows = tiles * ts // SC_CHUNK
    return pl.pallas_call(
        functools.partial(_dest_kernel, n_experts=padded_starts.shape[0]),
        grid_spec=pltpu.PrefetchScalarGridSpec(
            num_scalar_prefetch=1,
            grid=(1,),
            in_specs=[pl.BlockSpec(route.shape, lambda i, ps: (0, 0, 0))],
            out_specs=pl.BlockSpec((TOP_K, n_rows, SC_CHUNK), lambda i, ps: (0, 0, 0)),
        ),
        out_shape=jax.ShapeDtypeStruct((TOP_K, n_rows, SC_CHUNK), jnp.int32),
        compiler_params=pltpu.CompilerParams(dimension_semantics=("arbitrary",)),
        name="slot_ids",
    )(padded_starts, route)


def _sc_dispatch(dest, rows, n_slots):
    n_tok, nsub, l = rows.shape
    top_k, n_rows, ch = dest.shape
    n_cores, n_workers = _sc_workers()
    assert n_rows * ch == n_tok and n_rows % n_workers == 0
    n_chunks = n_rows // n_workers
    tpw = n_chunks * ch

    @functools.partial(
        pl.kernel,
        mesh=plsc.VectorSubcoreMesh(core_axis_name="c", subcore_axis_name="s"),
        out_type=jax.ShapeDtypeStruct((n_slots, nsub, l), rows.dtype),
        scratch_types=[pltpu.VMEM((top_k, n_chunks, ch), jnp.int32),
                       pltpu.VMEM((ch, nsub, l), rows.dtype),
                       pltpu.SemaphoreType.DMA],
        name="sc_dispatch",
    )
    def run(dest_hbm, rows_hbm, out_hbm, idx_v, buf, sem):
        wid = lax.axis_index("s") * n_cores + lax.axis_index("c")
        for k in range(top_k):
            pltpu.sync_copy(dest_hbm.at[k, pl.ds(wid * n_chunks, n_chunks)], idx_v.at[k])

        @pl.loop(0, n_chunks)
        def _(j):
            pltpu.sync_copy(rows_hbm.at[pl.ds(wid * tpw + j * ch, ch)], buf)
            for k in range(top_k):
                pltpu.async_copy(buf, out_hbm.at[idx_v.at[k, j]], sem).wait()

    return run(dest, rows)


def _sc_gather(dest, ys, n_tok):
    _, nsub, l = ys.shape
    top_k, n_rows, ch = dest.shape
    n_cores, n_workers = _sc_workers()
    assert n_rows * ch == n_tok and n_rows % n_workers == 0
    n_chunks = n_rows // n_workers
    tpw = n_chunks * ch

    @functools.partial(
        pl.kernel,
        mesh=plsc.VectorSubcoreMesh(core_axis_name="c", subcore_axis_name="s"),
        out_type=jax.ShapeDtypeStruct((top_k, n_tok, nsub, l), ys.dtype),
        scratch_types=[pltpu.VMEM((top_k, n_chunks, ch), jnp.int32),
                       pltpu.VMEM((ch, nsub, l), ys.dtype),
                       pltpu.SemaphoreType.DMA],
        name="sc_gather",
    )
    def run(dest_hbm, ys_hbm, out_hbm, idx_v, buf, sem):
        wid = lax.axis_index("s") * n_cores + lax.axis_index("c")
        for k in range(top_k):
            pltpu.sync_copy(dest_hbm.at[k, pl.ds(wid * n_chunks, n_chunks)], idx_v.at[k])

        @pl.loop(0, n_chunks)
        def _(j):
            for k in range(top_k):
                pltpu.async_copy(ys_hbm.at[idx_v.at[k, j]], buf, sem).wait()
                pltpu.sync_copy(buf, out_hbm.at[k, pl.ds(wid * tpw + j * ch, ch)])

    return run(dest, ys)


_T_EXPERT, _T_VALID, _T_NEXT, _T_RUN, _T_USED = range(5)


def _expert_kernel(ecol_ref, xs_ref, wg_hbm, wu_hbm, wd_hbm, ys_ref,
                   wg_f, wu_f, wd_f, wg_b, wu_b, wd_b, tbl_v, tbl_s, sems, tsem, *, nsub, n_blocks):
    i = pl.program_id(0)
    blk = xs_ref.shape[0] // nsub
    l = xs_ref.shape[1]
    n_experts = ecol_ref.shape[0]

    @pl.when(i == 0)
    def _build_block_table():
        width = tbl_v.shape[1]
        blk_end = ecol_ref[:, 0:1]
        slot_end = ecol_ref[:, 1:2]
        b = lax.broadcasted_iota(jnp.int32, (n_experts, width), 1).astype(jnp.float32)
        e_id = lax.broadcasted_iota(jnp.int32, (n_experts, width), 0).astype(jnp.float32)
        be = jnp.minimum(jnp.sum(jnp.where(b >= blk_end, 1.0, 0.0), axis=0, keepdims=True),
                         n_experts - 1.0)
        mine = e_id == be
        valid = jnp.clip(jnp.sum(jnp.where(mine, slot_end, 0.0), axis=0, keepdims=True)
                         - b[0:1, :] * blk, 0.0, float(blk))
        present = jnp.max(jnp.where(mine & (b < n_blocks), 1.0, 0.0), axis=1, keepdims=True) > 0.0
        run = jnp.sum(jnp.where((e_id < be) & present, 1.0, 0.0), axis=0, keepdims=True)
        nxt = jnp.min(jnp.where((e_id > be) & present, e_id, float(n_experts)), axis=0, keepdims=True)
        nxt = jnp.where(nxt == n_experts, -1.0, nxt)
        used = jnp.broadcast_to(jnp.max(blk_end, axis=0, keepdims=True), (1, width))
        rows = [be, valid, nxt, run, used] + [jnp.zeros((1, width), jnp.float32)] * (SUBLANES - 5)
        tbl_v[...] = jnp.concatenate(rows, axis=0).astype(jnp.int32)
        to_smem = pltpu.make_async_copy(tbl_v, tbl_s, tsem)
        to_smem.start()
        to_smem.wait()

    expert = tbl_s[_T_EXPERT, i]
    slot = tbl_s[_T_RUN, i] % 2

    def weight_copies(e, s):
        return [pltpu.make_async_copy(src.at[e], dst.at[s], sems.at[s])
                for src, dst in ((wg_hbm, wg_f), (wu_hbm, wu_f), (wd_hbm, wd_f))]

    @pl.when(i == 0)
    def _fetch_first_weights():
        for c in weight_copies(expert, 0):
            c.start()

    @pl.when((i == 0) | (expert != tbl_s[_T_EXPERT, jnp.maximum(i - 1, 0)]))
    def _new_expert_weights():
        for c in weight_copies(expert, slot):
            c.wait()
        wg_b[...] = wg_f[slot].astype(jnp.bfloat16)
        wu_b[...] = wu_f[slot].astype(jnp.bfloat16)
        wd_b[...] = wd_f[slot].astype(jnp.bfloat16)

        @pl.when(tbl_s[_T_NEXT, i] >= 0)
        def _fetch_next_run_weights():
            for c in weight_copies(tbl_s[_T_NEXT, i], 1 - slot):
                c.start()

    @pl.when(i < tbl_s[_T_USED, i])
    def _compute():
        live = lax.broadcasted_iota(jnp.int32, (blk, l), 0) < tbl_s[_T_VALID, i]
        xb = jnp.concatenate(
            [c.astype(jnp.bfloat16) for c in _load_packed_rows(xs_ref, blk, nsub, live)], axis=-1)
        a = _dot(xb, wg_b[...])
        u = _dot(xb, wu_b[...])
        _store_packed_rows(ys_ref, _dot((_silu(a) * u).astype(jnp.bfloat16), wd_b[...]))

    @pl.when(i >= tbl_s[_T_USED, i])
    def _unused_block():
        ys_ref[...] = jnp.zeros(ys_ref.shape, ys_ref.dtype)


def _expert_call(ecol, xs, wg, wu, wd, nsub):
    rows, l = xs.shape
    n_blocks = rows // (EXPERT_BLOCK * nsub)
    _, d, de = wg.shape
    width = -(-n_blocks // LANES) * LANES
    hbm = pl.BlockSpec(memory_space=pl.ANY)
    return pl.pallas_call(
        functools.partial(_expert_kernel, nsub=nsub, n_blocks=n_blocks),
        grid=(n_blocks,),
        in_specs=[pl.BlockSpec(ecol.shape, lambda i: (0, 0)),
                  pl.BlockSpec((EXPERT_BLOCK * nsub, l), lambda i: (i, 0)), hbm, hbm, hbm],
        out_specs=pl.BlockSpec((EXPERT_BLOCK * nsub, l), lambda i: (i, 0)),
        out_shape=jax.ShapeDtypeStruct((rows, l), xs.dtype),
        scratch_shapes=[pltpu.VMEM((2, d, de), wg.dtype), pltpu.VMEM((2, d, de), wu.dtype),
                        pltpu.VMEM((2, de, d), wd.dtype),
                        pltpu.VMEM((d, de), jnp.bfloat16), pltpu.VMEM((d, de), jnp.bfloat16),
                        pltpu.VMEM((de, d), jnp.bfloat16),
                        pltpu.VMEM((SUBLANES, width), jnp.int32),
                        pltpu.SMEM((SUBLANES, width), jnp.int32),
                        pltpu.SemaphoreType.DMA((2,)), pltpu.SemaphoreType.DMA(())],
        compiler_params=pltpu.CompilerParams(
            dimension_semantics=("arbitrary",), vmem_limit_bytes=VMEM_LIMIT),
        name="experts",
    )(ecol, xs, wg, wu, wd)


def _combine_kernel(x1_ref, gc_ref, gfin_ref, y_ref, out_ref, *, nsub):
    tc, d = x1_ref.shape
    g1 = gc_ref[:, 0:1]
    g2 = gc_ref[:, 1:2]
    pieces = []
    ssq = jnp.zeros((tc, 1), jnp.float32)
    y1_cols = _load_packed_rows(y_ref.at[0], tc, nsub)
    y2_cols = _load_packed_rows(y_ref.at[1], tc, nsub)
    for j, (y1, y2) in enumerate(zip(y1_cols, y2_cols)):
        z = x1_ref[:, j * LANES:(j + 1) * LANES] + (g1 * y1 + g2 * y2)
        pieces.append(z)
        ssq = ssq + jnp.sum(z * z, axis=-1, keepdims=True)
    scale = lax.rsqrt(ssq / d + EPS)
    for j, z in enumerate(pieces):
        out_ref[:, j * LANES:(j + 1) * LANES] = z * scale * gfin_ref[:, j * LANES:(j + 1) * LANES]


def _combine_call(x1, gates_col, gfin, y_tok, nsub, out_prev, tok0, n_tok_total):
    n_tok, d = x1.shape
    tc = min(COMBINE_TILE, n_tok)
    assert n_tok % tc == 0 and tok0 % tc == 0
    l = y_tok.shape[-1]
    in_specs = [
        pl.BlockSpec((tc, d), lambda i: (i, 0)),
        pl.BlockSpec((tc, LANES), lambda i: (i, 0)),
        pl.BlockSpec((1, d), lambda i: (0, 0)),
        pl.BlockSpec((TOP_K, tc * nsub, l), lambda i: (0, i, 0)),
    ]
    args = [x1, gates_col, gfin, y_tok]
    aliases = {}
    kern = functools.partial(_combine_kernel, nsub=nsub)
    if out_prev is not None:
        in_specs.append(pl.BlockSpec(memory_space=pl.ANY))
        args.append(out_prev)
        aliases = {len(args) - 1: 0}
        kern = lambda x1_ref, gc_ref, gfin_ref, y_ref, prev_ref, out_ref: _combine_kernel(
            x1_ref, gc_ref, gfin_ref, y_ref, out_ref, nsub=nsub)
    return pl.pallas_call(
        kern,
        grid=(n_tok // tc,),
        in_specs=in_specs,
        out_specs=pl.BlockSpec((tc, d), lambda i: (i + tok0 // tc, 0)),
        out_shape=jax.ShapeDtypeStruct((n_tok_total, d), jnp.float32),
        input_output_aliases=aliases,
        compiler_params=pltpu.CompilerParams(
            dimension_semantics=("arbitrary",), vmem_limit_bytes=VMEM_LIMIT),
        name="combine",
    )(*args)


def _prepare_layer(l, norm_mix_g, w_in, b_in, conv_w, conv_b, conv_ln_g, conv_ln_b, qk_conv_w,
                   qk_conv_b, mlstm_norm_g, w_out, norm_ffn_g, router_group_w, router_group_b,
                   router_expert_w, router_expert_b, ts):
    f32, bf16 = jnp.float32, jnp.bfloat16
    d = w_in.shape[1]
    dc = conv_w.shape[-1]
    dm = mlstm_norm_g.shape[-1]
    n_heads = (w_in.shape[-1] - 2 * dc - 4 * dm) // 2
    n_groups = router_group_w.shape[-1]
    n_experts = router_expert_w.shape[-1]
    wi, bi = w_in[l], b_in[l]
    o_qk, o_vo, o_if = 2 * dc, 2 * dc + 2 * dm, 2 * dc + 4 * dm
    w_if = wi[:, o_if:]
    b_if = bi[o_if:]
    wr_t = jnp.zeros((SUBLANES + n_experts, d), f32)
    wr_t = wr_t.at[:n_groups].set(router_group_w[l].T).at[SUBLANES:].set(router_expert_w[l].T)
    br = jnp.zeros((SUBLANES + n_experts, 1), f32)
    br = br.at[:n_groups, 0].set(router_group_b[l]).at[SUBLANES:, 0].set(router_expert_b[l])
    wr_hi = wr_t.astype(bf16)
    wr_lo = (wr_t - wr_hi.astype(f32)).astype(bf16)
    pad_rows = lambda a, n: jnp.pad(a, ((0, n - a.shape[0]), (0, 0)))
    idx = jnp.arange(ts)
    cidx = jnp.arange(CHUNK)
    return dict(
        gmix=norm_mix_g[l][None, :],
        wcv=wi[:, :o_qk].astype(bf16), bcv=bi[None, :o_qk],
        wqk=wi[:, o_qk:o_vo].astype(bf16), bqk=bi[None, o_qk:o_vo],
        wvo=wi[:, o_vo:o_if].astype(bf16), bvo=bi[None, o_vo:o_if],
        wif=jnp.pad(w_if, ((0, 0), (0, LANES - 2 * n_heads))).astype(bf16),
        bif=jnp.pad(b_if, (0, LANES - 2 * n_heads))[None, :],
        wift=pad_rows(w_if.T, SUBLANES).astype(bf16),
        bift=pad_rows(b_if[:, None], SUBLANES),
        cw=jnp.repeat(conv_w[l], SUBLANES, axis=0), cb=conv_b[l][None, :],
        lng=conv_ln_g[l][None, :], lnb=conv_ln_b[l][None, :],
        qkw=jnp.repeat(qk_conv_w[l], SUBLANES, axis=0), qkb=qk_conv_b[l][None, :],
        mng=mlstm_norm_g[l][None, :],
        wout=w_out[l].astype(bf16), gffn=norm_ffn_g[l][None, :],
        wrh=wr_hi, wrl=wr_lo, br=br,
        su=(idx[:, None] < idx[None, :]).astype(bf16),
        tril=(cidx[:, None] >= cidx[None, :]).astype(bf16),
    ), dict(n_heads=n_heads, conv_width=conv_w.shape[1], qk_width=qk_conv_w.shape[1],
            n_groups=n_groups, epg=n_experts // n_groups)


def kernel(x, norm_mix_g, w_in, b_in, conv_w, conv_b, conv_ln_g, conv_ln_b, qk_conv_w, qk_conv_b,
           mlstm_norm_g, w_out, norm_ffn_g, router_group_w, router_group_b, router_expert_w,
           router_expert_b, expert_w_gate, expert_w_up, expert_w_down, final_norm_g):
    bsz, seq, d = x.shape
    depth = w_in.shape[0]
    assert depth == 1, "the combine kernel fuses the final norm, so exactly one layer is supported"
    n_experts = router_expert_w.shape[-1]
    n_parts = N_PARTS if bsz % N_PARTS == 0 else 1
    pb = bsz // n_parts
    n_tok = pb * seq
    n_assign = n_tok * TOP_K
    n_blocks = -(-n_assign // EXPERT_BLOCK) + n_experts
    n_slots = n_blocks * EXPERT_BLOCK
    ts = min(SEQ_TILE, seq)
    nsub = d // PACK_COLS
    l = 0
    params, dims = _prepare_layer(
        l, norm_mix_g, w_in, b_in, conv_w, conv_b, conv_ln_g, conv_ln_b, qk_conv_w, qk_conv_b,
        mlstm_norm_g, w_out, norm_ffn_g, router_group_w, router_group_b, router_expert_w,
        router_expert_b, ts)
    wg, wu, wd = expert_w_gate[l], expert_w_up[l], expert_w_down[l]
    out = None
    prev_dest = None
    for part in range(n_parts):
        b0 = jnp.full((1,), part * pb, jnp.int32)
        if prev_dest is not None:
            b0 = b0 + jnp.minimum(prev_dest[0, 0, :1], 0)
        x1, h2_rows, route_i, gates_col, counts = _mixer_call(x, params, b0, pb, **dims)

        cnt = counts[:, 0].astype(jnp.int32)
        padded = (cnt + EXPERT_BLOCK - 1) // EXPERT_BLOCK * EXPERT_BLOCK
        padded_ends = jnp.cumsum(padded)
        padded_starts = padded_ends - padded
        ecol = jnp.pad(jnp.stack([padded_ends // EXPERT_BLOCK, padded_starts + cnt], axis=1),
                       ((0, 0), (0, LANES - 2))).astype(jnp.float32)

        dest = _dest_call(padded_starts, route_i.reshape(pb * (seq // ts), SUBLANES, ts))
        prev_dest = dest
        xs = _sc_dispatch(dest, h2_rows.reshape(n_tok, nsub, LANES), n_slots)
        ys = _expert_call(ecol, xs.reshape(n_slots * nsub, LANES), wg, wu, wd, nsub)
        y_tok = _sc_gather(dest, ys.reshape(n_slots, nsub, LANES), n_tok)
        out = _combine_call(x1.reshape(n_tok, d), gates_col.reshape(n_tok, LANES),
                            final_norm_g[None, :], y_tok.reshape(TOP_K, n_tok * nsub, LANES), nsub,
                            out, part * n_tok, bsz * seq)
    return out.reshape(bsz, seq, d)
```

```python
import functools

import jax
import jax.numpy as jnp
from jax import lax
from jax.experimental import pallas as pl
from jax.experimental.pallas import tpu as pltpu
from jax.experimental.pallas import tpu_sc as plsc

EPS = 1e-6
LANES = 128
SUBLANES = 8
CHUNK = 128
TOP_K = 2
SEQ_TILE = 512
CONV_ROWS = 32
CONV_HALO = 32
QK_HALO = 8
EXPERT_BLOCK = 512
SC_CHUNK = 128
COMBINE_TILE = 1024
N_PARTS = 2
PACK_COLS = 2 * LANES
VMEM_LIMIT = 56 * 1024 * 1024

_NT = (((1,), (1,)), ((), ()))
_TN = (((0,), (0,)), ((), ()))
_HI16 = 0xFFFF0000


def _dot(a, b):
    return jnp.dot(a, b, preferred_element_type=jnp.float32)


def _silu(x):
    return x * jax.nn.sigmoid(x)


def _log_sigmoid(x):
    return jnp.minimum(x, 0.0) - jnp.log1p(jnp.exp(-jnp.abs(x)))


def _split_bf16(x, parts):
    out = []
    for _ in range(parts):
        p = x.astype(jnp.bfloat16)
        out.append(p)
        x = x - p.astype(jnp.float32)
    return out


def _store_packed_rows(ref, v):
    n, d = v.shape
    half = d // 2
    nsub = half // LANES
    as_bits = lambda a: pltpu.bitcast(a.astype(jnp.bfloat16).astype(jnp.float32), jnp.uint32)
    word = (as_bits(v[:, :half]) >> 16) | (as_bits(v[:, half:]) & jnp.uint32(_HI16))
    for s in range(nsub):
        ref[pl.ds(s, n, stride=nsub), :] = word[:, s * LANES:(s + 1) * LANES]


def _load_packed_rows(ref, n, nsub, live=None):
    lo, hi = [], []
    for s in range(nsub):
        w = ref[pl.ds(s, n, stride=nsub), :]
        if live is not None:
            w = jnp.where(live, w, jnp.uint32(0))
        lo.append(pltpu.bitcast(w << 16, jnp.float32))
        hi.append(pltpu.bitcast(w & jnp.uint32(_HI16), jnp.float32))
    return lo + hi


def _mixer_kernel(b0_ref, x_ref, gmix_ref, wcv_ref, bcv_ref, wqk_ref, bqk_ref, wvo_ref, bvo_ref,
                  wif_ref, bif_ref, wift_ref, bift_ref, cw_ref, cb_ref, lng_ref, lnb_ref,
                  qkw_ref, qkb_ref, mng_ref, wout_ref, gffn_ref, wrh_ref, wrl_ref, br_ref,
                  su_ref, tril_ref,
                  x1_ref, h2_ref, ri_ref, gc_ref, cnt_ref,
                  uext, qkext, q_s, k_s, v_s, o_s, cn_s, m_s, y_s, cnt_s,
                  *, n_heads, conv_width, qk_width, n_groups, epg):
    ts, d = x_ref.shape
    dc = cw_ref.shape[1]
    dm = mng_ref.shape[1]
    dh = dm // n_heads
    n_chunks = ts // CHUNK
    n_experts = n_groups * epg
    b_idx = pl.program_id(0)
    t_idx = pl.program_id(1)

    @pl.when(t_idx == 0)
    def _reset_sequence_state():
        uext[0:CONV_HALO, :] = jnp.zeros((CONV_HALO, dc), jnp.float32)
        uext[CONV_HALO + ts:, :] = jnp.zeros((SUBLANES, dc), jnp.float32)
        qkext[0:QK_HALO, :] = jnp.zeros((QK_HALO, 2 * dm), jnp.float32)
        cn_s[...] = jnp.zeros(cn_s.shape, jnp.float32)
        m_s[...] = jnp.zeros(m_s.shape, jnp.float32)

    @pl.when((t_idx == 0) & (b_idx == 0))
    def _reset_counts():
        cnt_s[...] = jnp.zeros(cnt_s.shape, jnp.float32)

    x = x_ref[...]
    h = x * lax.rsqrt(jnp.mean(x * x, axis=-1, keepdims=True) + EPS) * gmix_ref[...]
    hb = h.astype(jnp.bfloat16)

    zc = _dot(hb, wcv_ref[...]) + bcv_ref[...]
    uext[CONV_HALO:CONV_HALO + ts, :] = zc[:, :dc] * jax.nn.sigmoid(zc[:, dc:])
    first = CONV_HALO - (conv_width - 1)
    sub = lax.broadcasted_iota(jnp.int32, (SUBLANES, dc), 0)
    n_grp = CONV_ROWS // SUBLANES
    for rb in range(ts // CONV_ROWS):
        r0 = rb * CONV_ROWS
        acc = jnp.zeros((CONV_ROWS, dc), jnp.float32) + cb_ref[...]
        for r in range(SUBLANES):
            part = None
            for j in range(conv_width):
                if (first + j) % SUBLANES != r:
                    continue
                a0 = r0 + (first + j) - r
                w8 = cw_ref[j * SUBLANES:(j + 1) * SUBLANES, :]
                term = jnp.concatenate([w8] * (n_grp + 1), axis=0) * uext[a0:a0 + CONV_ROWS + SUBLANES, :]
                part = term if part is None else part + term
            if part is None:
                continue
            if r == 0:
                acc = acc + part[:CONV_ROWS, :]
                continue
            rolled = [pltpu.roll(part[g * SUBLANES:(g + 1) * SUBLANES, :], SUBLANES - r, axis=0)
                      for g in range(n_grp + 1)]
            acc = acc + jnp.concatenate(
                [jnp.where(sub < SUBLANES - r, rolled[g], rolled[g + 1]) for g in range(n_grp)], axis=0)
        mu = jnp.mean(acc, axis=-1, keepdims=True)
        cen = acc - mu
        var = jnp.mean(cen * cen, axis=-1, keepdims=True)
        yn = cen * lax.rsqrt(var + EPS) * lng_ref[...] + lnb_ref[...]
        y_s[r0:r0 + CONV_ROWS, 0:dc] = _silu(yn).astype(jnp.bfloat16)
    uext[0:CONV_HALO, :] = uext[ts:ts + CONV_HALO, :]

    qkext[QK_HALO:QK_HALO + ts, :] = _dot(hb, wqk_ref[...]) + bqk_ref[...]
    qfirst = QK_HALO - (qk_width - 1)
    k_scale = dh ** -0.5
    for rb in range(ts // CONV_ROWS):
        r0 = rb * CONV_ROWS
        acc = jnp.zeros((CONV_ROWS, 2 * dm), jnp.float32) + qkb_ref[...]
        for j in range(qk_width):
            w8 = qkw_ref[j * SUBLANES:(j + 1) * SUBLANES, :]
            acc = acc + (jnp.concatenate([w8] * (CONV_ROWS // SUBLANES), axis=0)
                         * qkext[r0 + qfirst + j:r0 + qfirst + j + CONV_ROWS, :])
        act = _silu(acc)
        q_s[r0:r0 + CONV_ROWS, :] = act[:, :dm].astype(jnp.bfloat16)
        k_s[r0:r0 + CONV_ROWS, :] = (act[:, dm:] * k_scale).astype(jnp.bfloat16)
    qkext[0:QK_HALO, :] = qkext[ts:ts + QK_HALO, :]

    zvo = _dot(hb, wvo_ref[...]) + bvo_ref[...]
    v_s[...] = zvo[:, :dm].astype(jnp.bfloat16)
    o_s[...] = jax.nn.sigmoid(zvo[:, dm:])

    zif_col = _dot(hb, wif_ref[...]) + bif_ref[...]
    zif_row = lax.dot_general(wift_ref[...], hb, _NT,
                              preferred_element_type=jnp.float32) + bift_ref[...]
    logf_col = _log_sigmoid(zif_col)
    logf_row = _log_sigmoid(zif_row)
    lane = lax.broadcasted_iota(jnp.int32, (SUBLANES, CHUNK), 1)
    causal = (lax.broadcasted_iota(jnp.int32, (CHUNK, CHUNK), 0)
              >= lax.broadcasted_iota(jnp.int32, (CHUNK, CHUNK), 1))
    ones_blk = jnp.ones((CHUNK, dh), jnp.bfloat16)
    tril = tril_ref[...]

    for c in range(n_chunks):
        r0 = c * CHUNK
        b_row = logf_row[:, r0:r0 + CHUNK]
        shift = 1
        while shift < CHUNK:
            b_row = b_row + jnp.where(lane >= shift, pltpu.roll(b_row, shift, axis=1), 0.0)
            shift *= 2
        b_col = sum(_dot(tril, p) for p in _split_bf16(logf_col[r0:r0 + CHUNK, :], 3))
        i_col_all = zif_col[r0:r0 + CHUNK, :]
        i_row_all = zif_row[:, r0:r0 + CHUNK]
        for hd in range(n_heads):
            c0 = hd * dh
            b_c = b_col[:, n_heads + hd:n_heads + hd + 1]
            i_c = i_col_all[:, hd:hd + 1]
            b_r = b_row[n_heads + hd:n_heads + hd + 1, :]
            i_r = i_row_all[hd:hd + 1, :]
            m_prev = m_s[hd, 0:1, 0:1]
            q = q_s[r0:r0 + CHUNK, c0:c0 + dh]
            k = k_s[r0:r0 + CHUNK, c0:c0 + dh]
            vaug = jnp.concatenate([v_s[r0:r0 + CHUNK, c0:c0 + dh], ones_blk], axis=-1)
            cn = cn_s[hd]

            dmat = jnp.where(causal, b_c + (i_r - b_r), -jnp.inf)
            g = b_c + m_prev
            m_t = jnp.maximum(g, jnp.max(dmat, axis=-1, keepdims=True))
            w_intra = jnp.exp(dmat - m_t)
            w_inter = jnp.exp(g - m_t)
            s = lax.dot_general(q, k, _NT, preferred_element_type=jnp.float32) * w_intra
            nd = _dot(s.astype(jnp.bfloat16), vaug) + w_inter * _dot(q, cn.astype(jnp.bfloat16))
            hval = nd[:, :dh] / jnp.maximum(jnp.abs(nd[:, dh:]), jnp.exp(-m_t))

            b_last = b_r[:, CHUNK - 1:CHUNK]
            logw_c = b_last - b_c + i_c
            m_new = jnp.maximum(b_last + m_prev, jnp.max(logw_c, axis=0, keepdims=True))
            decay = jnp.exp(b_last + m_prev - m_new)
            kw = (k.astype(jnp.float32) * jnp.exp(logw_c - m_new)).astype(jnp.bfloat16)
            cn_s[hd] = decay * cn + lax.dot_general(kw, vaug, _TN,
                                                    preferred_element_type=jnp.float32)
            m_s[hd] = jnp.broadcast_to(m_new, m_s.shape[1:])

            hm = o_s[r0:r0 + CHUNK, c0:c0 + dh] * hval
            mu = jnp.mean(hm, axis=-1, keepdims=True)
            cen = hm - mu
            var = jnp.mean(cen * cen, axis=-1, keepdims=True)
            hn = cen * lax.rsqrt(var + EPS) * mng_ref[:, c0:c0 + dh]
            y_s[r0:r0 + CHUNK, dc + c0:dc + c0 + dh] = hn.astype(jnp.bfloat16)

    x1 = x_ref[...] + _dot(y_s[...], wout_ref[...])
    x1_ref[...] = x1
    h2 = x1 * lax.rsqrt(jnp.mean(x1 * x1, axis=-1, keepdims=True) + EPS) * gffn_ref[...]
    _store_packed_rows(h2_ref, h2)

    h2_hi, h2_lo = _split_bf16(h2, 2)
    lg = (lax.dot_general(wrh_ref[...], h2_hi, _NT, preferred_element_type=jnp.float32)
          + lax.dot_general(wrl_ref[...], h2_hi, _NT, preferred_element_type=jnp.float32)
          + lax.dot_general(wrh_ref[...], h2_lo, _NT, preferred_element_type=jnp.float32)
          + br_ref[...])
    gl = lg[0:n_groups, :]
    gidx = lax.broadcasted_iota(jnp.int32, (n_groups, ts), 0)
    gmax = jnp.max(gl, axis=0, keepdims=True)
    grp = jnp.min(jnp.where(gl == gmax, gidx, n_groups), axis=0, keepdims=True)
    p_grp = 1.0 / jnp.sum(jnp.exp(gl - gmax), axis=0, keepdims=True)
    in_group = jnp.zeros((epg, ts), jnp.float32)
    for gi in range(n_groups):
        in_group = jnp.where(grp == gi, lg[SUBLANES + gi * epg:SUBLANES + (gi + 1) * epg, :], in_group)
    eidx = lax.broadcasted_iota(jnp.int32, (epg, ts), 0)
    v1 = jnp.max(in_group, axis=0, keepdims=True)
    i1 = jnp.min(jnp.where(in_group == v1, eidx, epg), axis=0, keepdims=True)
    rest = jnp.where(eidx == i1, -jnp.inf, in_group)
    v2 = jnp.max(rest, axis=0, keepdims=True)
    i2 = jnp.min(jnp.where(rest == v2, eidx, epg), axis=0, keepdims=True)
    e21 = jnp.exp(v2 - v1)
    g1 = p_grp / (1.0 + e21)
    g2 = g1 * e21
    e1 = grp * epg + i1
    e2 = grp * epg + i2

    xidx = lax.broadcasted_iota(jnp.int32, (n_experts, ts), 0)
    sel1 = xidx == e1
    sel2 = xidx == e2
    onehot = jnp.where(sel1 | sel2, 1.0, 0.0).astype(jnp.bfloat16)
    before = cnt_s[:, 0:1] + _dot(onehot, su_ref[...])
    rank1 = jnp.sum(jnp.where(sel1, before, 0.0), axis=0, keepdims=True)
    rank2 = jnp.sum(jnp.where(sel2, before, 0.0), axis=0, keepdims=True)
    cnt_new = cnt_s[...] + jnp.sum(onehot.astype(jnp.float32), axis=1, keepdims=True)
    cnt_s[...] = cnt_new
    cnt_ref[...] = cnt_new

    zero_i = jnp.zeros((SUBLANES - 4, ts), jnp.int32)
    ri_ref[...] = jnp.concatenate(
        [e1, e2, rank1.astype(jnp.int32), rank2.astype(jnp.int32), zero_i], axis=0)
    gates_row = jnp.concatenate([g1, g2, jnp.zeros((LANES - 2, ts), jnp.float32)], axis=0)
    gc_ref[...] = gates_row.T


def _mixer_call(x, p, b0, bsz, *, n_heads, conv_width, qk_width, n_groups, epg):
    _, seq, d = x.shape
    ts = min(SEQ_TILE, seq)
    nt = seq // ts
    dc = p["cw"].shape[1]
    dm = p["mng"].shape[1]
    dh = dm // n_heads
    n_experts = n_groups * epg
    assert seq % ts == 0 and ts % CHUNK == 0 and dh == LANES and d % PACK_COLS == 0
    assert epg == SUBLANES and n_groups <= SUBLANES and conv_width - 1 <= CONV_HALO

    consts = [p[n] for n in ("gmix", "wcv", "bcv", "wqk", "bqk", "wvo", "bvo", "wif", "bif",
                             "wift", "bift", "cw", "cb", "lng", "lnb", "qkw", "qkb", "mng",
                             "wout", "gffn", "wrh", "wrl", "br", "su", "tril")]

    def const_spec(a):
        return pl.BlockSpec(a.shape, lambda b, t, b0r: (0,) * a.ndim)

    kern = functools.partial(_mixer_kernel, n_heads=n_heads, conv_width=conv_width,
                             qk_width=qk_width, n_groups=n_groups, epg=epg)
    grid_spec = pltpu.PrefetchScalarGridSpec(
        num_scalar_prefetch=1,
        grid=(bsz, nt),
        in_specs=([pl.BlockSpec((None, ts, d), lambda b, t, b0r: (b + b0r[0], t, 0))]
                  + [const_spec(a) for a in consts]),
        out_specs=[
            pl.BlockSpec((None, ts, d), lambda b, t, b0r: (b, t, 0)),
            pl.BlockSpec((None, ts * (d // PACK_COLS), LANES), lambda b, t, b0r: (b, t, 0)),
            pl.BlockSpec((None, None, SUBLANES, ts), lambda b, t, b0r: (b, t, 0, 0)),
            pl.BlockSpec((None, ts, LANES), lambda b, t, b0r: (b, t, 0)),
            pl.BlockSpec((n_experts, LANES), lambda b, t, b0r: (0, 0)),
        ],
        scratch_shapes=[
            pltpu.VMEM((CONV_HALO + ts + SUBLANES, dc), jnp.float32),
            pltpu.VMEM((QK_HALO + ts, 2 * dm), jnp.float32),
            pltpu.VMEM((ts, dm), jnp.bfloat16),
            pltpu.VMEM((ts, dm), jnp.bfloat16),
            pltpu.VMEM((ts, dm), jnp.bfloat16),
            pltpu.VMEM((ts, dm), jnp.float32),
            pltpu.VMEM((n_heads, dh, 2 * dh), jnp.float32),
            pltpu.VMEM((n_heads, SUBLANES, LANES), jnp.float32),
            pltpu.VMEM((ts, d), jnp.bfloat16),
            pltpu.VMEM((n_experts, LANES), jnp.float32),
        ],
    )
    return pl.pallas_call(
        kern,
        grid_spec=grid_spec,
        out_shape=[
            jax.ShapeDtypeStruct((bsz, seq, d), jnp.float32),
            jax.ShapeDtypeStruct((bsz, seq * (d // PACK_COLS), LANES), jnp.uint32),
            jax.ShapeDtypeStruct((bsz, nt, SUBLANES, ts), jnp.int32),
            jax.ShapeDtypeStruct((bsz, seq, LANES), jnp.float32),
            jax.ShapeDtypeStruct((n_experts, LANES), jnp.float32),
        ],
        compiler_params=pltpu.CompilerParams(
            dimension_semantics=("arbitrary", "arbitrary"), vmem_limit_bytes=VMEM_LIMIT),
        name="mixer_router",
    )(b0, x, *consts)


def _sc_workers():
    info = plsc.get_sparse_core_info()
    return info.num_cores, info.num_cores * info.num_subcores


def _dest_kernel(pstart_ref, route_ref, dest_ref, *, n_experts):
    tiles, _, ts = route_ref.shape
    per = ts // SC_CHUNK
    for k in range(TOP_K):
        e = route_ref[:, k, :]
        rank = route_ref[:, TOP_K + k, :]
        start = jnp.zeros((tiles, ts), jnp.int32)
        for ex in range(n_experts):
            start = jnp.where(e == ex, pstart_ref[ex], start)
        slot = start + rank
        for j in range(per):
            dest_ref[k, pl.ds(j, tiles, stride=per), :] = slot[:, j * SC_CHUNK:(j + 1) * SC_CHUNK]


def _dest_call(padded_starts, route):
    tiles, _, ts = route.shape
    assert ts % SC_CHUNK == 0 and SC_CHUNK == LANES
    n_rows = tiles * ts // SC_CHUNK
    return pl.pallas_call(
        functools.partial(_dest_kernel, n_experts=padded_starts.shape[0]),
        grid_spec=pltpu.PrefetchScalarGridSpec(
            num_scalar_prefetch=1,
            grid=(1,),
            in_specs=[pl.BlockSpec(route.shape, lambda i, ps: (0, 0, 0))],
            out_specs=pl.BlockSpec((TOP_K, n_rows, SC_CHUNK), lambda i, ps: (0, 0, 0)),
        ),
        out_shape=jax.ShapeDtypeStruct((TOP_K, n_rows, SC_CHUNK), jnp.int32),
        compiler_params=pltpu.CompilerParams(dimension_semantics=("arbitrary",)),
        name="slot_ids",
    )(padded_starts, route)


def _sc_dispatch(dest, rows, n_slots):
    n_tok, nsub, l = rows.shape
    top_k, n_rows, ch = dest.shape
    n_cores, n_workers = _sc_workers()
    assert n_rows * ch == n_tok and n_rows % n_workers == 0
    n_chunks = n_rows // n_workers
    tpw = n_chunks * ch

    @functools.partial(
        pl.kernel,
        mesh=plsc.VectorSubcoreMesh(core_axis_name="c", subcore_axis_name="s"),
        out_type=jax.ShapeDtypeStruct((n_slots, nsub, l), rows.dtype),
        scratch_types=[pltpu.VMEM((top_k, n_chunks, ch), jnp.int32),
                       pltpu.VMEM((ch, nsub, l), rows.dtype),
                       pltpu.SemaphoreType.DMA],
        name="sc_dispatch",
    )
    def run(dest_hbm, rows_hbm, out_hbm, idx_v, buf, sem):
        wid = lax.axis_index("s") * n_cores + lax.axis_index("c")
        for k in range(top_k):
            pltpu.sync_copy(dest_hbm.at[k, pl.ds(wid * n_chunks, n_chunks)], idx_v.at[k])

        @pl.loop(0, n_chunks)
        def _(j):
            pltpu.sync_copy(rows_hbm.at[pl.ds(wid * tpw + j * ch, ch)], buf)
            for k in range(top_k):
                pltpu.async_copy(buf, out_hbm.at[idx_v.at[k, j]], sem).wait()

    return run(dest, rows)


def _sc_gather(dest, ys, n_tok):
    _, nsub, l = ys.shape
    top_k, n_rows, ch = dest.shape
    n_cores, n_workers = _sc_workers()
    assert n_rows * ch == n_tok and n_rows % n_workers == 0
    n_chunks = n_rows // n_workers
    tpw = n_chunks * ch

    @functools.partial(
        pl.kernel,
        mesh=plsc.VectorSubcoreMesh(core_axis_name="c", subcore_axis_name="s"),
        out_type=jax.ShapeDtypeStruct((top_k, n_tok, nsub, l), ys.dtype),
        scratch_types=[pltpu.VMEM((top_k, n_chunks, ch), jnp.int32),
                       pltpu.VMEM((ch, nsub, l), ys.dtype),
                       pltpu.SemaphoreType.DMA],
        name="sc_gather",
    )
    def run(dest_hbm, ys_hbm, out_hbm, idx_v, buf, sem):
        wid = lax.axis_index("s") * n_cores + lax.axis_index("c")
        for k in range(top_k):
            pltpu.sync_copy(dest_hbm.at[k, pl.ds(wid * n_chunks, n_chunks)], idx_v.at[k])

        @pl.loop(0, n_chunks)
        def _(j):
            for k in range(top_k):
                pltpu.async_copy(ys_hbm.at[idx_v.at[k, j]], buf, sem).wait()
                pltpu.sync_copy(buf, out_hbm.at[k, pl.ds(wid * tpw + j * ch, ch)])

    return run(dest, ys)


_T_EXPERT, _T_VALID, _T_NEXT, _T_RUN, _T_USED = range(5)


def _expert_kernel(ecol_ref, xs_ref, wg_hbm, wu_hbm, wd_hbm, ys_ref,
                   wg_f, wu_f, wd_f, wg_b, wu_b, wd_b, tbl_v, tbl_s, sems, tsem, *, nsub, n_blocks):
    i = pl.program_id(0)
    blk = xs_ref.shape[0] // nsub
    l = xs_ref.shape[1]
    n_experts = ecol_ref.shape[0]

    @pl.when(i == 0)
    def _build_block_table():
        width = tbl_v.shape[1]
        blk_end = ecol_ref[:, 0:1]
        slot_end = ecol_ref[:, 1:2]
        b = lax.broadcasted_iota(jnp.int32, (n_experts, width), 1).astype(jnp.float32)
        e_id = lax.broadcasted_iota(jnp.int32, (n_experts, width), 0).astype(jnp.float32)
        be = jnp.minimum(jnp.sum(jnp.where(b >= blk_end, 1.0, 0.0), axis=0, keepdims=True),
                         n_experts - 1.0)
        mine = e_id == be
        valid = jnp.clip(jnp.sum(jnp.where(mine, slot_end, 0.0), axis=0, keepdims=True)
                         - b[0:1, :] * blk, 0.0, float(blk))
        present = jnp.max(jnp.where(mine & (b < n_blocks), 1.0, 0.0), axis=1, keepdims=True) > 0.0
        run = jnp.sum(jnp.where((e_id < be) & present, 1.0, 0.0), axis=0, keepdims=True)
        nxt = jnp.min(jnp.where((e_id > be) & present, e_id, float(n_experts)), axis=0, keepdims=True)
        nxt = jnp.where(nxt == n_experts, -1.0, nxt)
        used = jnp.broadcast_to(jnp.max(blk_end, axis=0, keepdims=True), (1, width))
        rows = [be, valid, nxt, run, used] + [jnp.zeros((1, width), jnp.float32)] * (SUBLANES - 5)
        tbl_v[...] = jnp.concatenate(rows, axis=0).astype(jnp.int32)
        to_smem = pltpu.make_async_copy(tbl_v, tbl_s, tsem)
        to_smem.start()
        to_smem.wait()

    expert = tbl_s[_T_EXPERT, i]
    slot = tbl_s[_T_RUN, i] % 2

    def weight_copies(e, s):
        return [pltpu.make_async_copy(src.at[e], dst.at[s], sems.at[s])
                for src, dst in ((wg_hbm, wg_f), (wu_hbm, wu_f), (wd_hbm, wd_f))]

    @pl.when(i == 0)
    def _fetch_first_weights():
        for c in weight_copies(expert, 0):
            c.start()

    @pl.when((i == 0) | (expert != tbl_s[_T_EXPERT, jnp.maximum(i - 1, 0)]))
    def _new_expert_weights():
        for c in weight_copies(expert, slot):
            c.wait()
        wg_b[...] = wg_f[slot].astype(jnp.bfloat16)
        wu_b[...] = wu_f[slot].astype(jnp.bfloat16)
        wd_b[...] = wd_f[slot].astype(jnp.bfloat16)

        @pl.when(tbl_s[_T_NEXT, i] >= 0)
        def _fetch_next_run_weights():
            for c in weight_copies(tbl_s[_T_NEXT, i], 1 - slot):
                c.start()

    @pl.when(i < tbl_s[_T_USED, i])
    def _compute():
        live = lax.broadcasted_iota(jnp.int32, (blk, l), 0) < tbl_s[_T_VALID, i]
        xb = jnp.concatenate(
            [c.astype(jnp.bfloat16) for c in _load_packed_rows(xs_ref, blk, nsub, live)], axis=-1)
        a = _dot(xb, wg_b[...])
        u = _dot(xb, wu_b[...])
        _store_packed_rows(ys_ref, _dot((_silu(a) * u).astype(jnp.bfloat16), wd_b[...]))

    @pl.when(i >= tbl_s[_T_USED, i])
    def _unused_block():
        ys_ref[...] = jnp.zeros(ys_ref.shape, ys_ref.dtype)


def _expert_call(ecol, xs, wg, wu, wd, nsub):
    rows, l = xs.shape
    n_blocks = rows // (EXPERT_BLOCK * nsub)
    _, d, de = wg.shape
    width = -(-n_blocks // LANES) * LANES
    hbm = pl.BlockSpec(memory_space=pl.ANY)
    return pl.pallas_call(
        functools.partial(_expert_kernel, nsub=nsub, n_blocks=n_blocks),
        grid=(n_blocks,),
        in_specs=[pl.BlockSpec(ecol.shape, lambda i: (0, 0)),
                  pl.BlockSpec((EXPERT_BLOCK * nsub, l), lambda i: (i, 0)), hbm, hbm, hbm],
        out_specs=pl.BlockSpec((EXPERT_BLOCK * nsub, l), lambda i: (i, 0)),
        out_shape=jax.ShapeDtypeStruct((rows, l), xs.dtype),
        scratch_shapes=[pltpu.VMEM((2, d, de), wg.dtype), pltpu.VMEM((2, d, de), wu.dtype),
                        pltpu.VMEM((2, de, d), wd.dtype),
                        pltpu.VMEM((d, de), jnp.bfloat16), pltpu.VMEM((d, de), jnp.bfloat16),
                        pltpu.VMEM((de, d), jnp.bfloat16),
                        pltpu.VMEM((SUBLANES, width), jnp.int32),
                        pltpu.SMEM((SUBLANES, width), jnp.int32),
                        pltpu.SemaphoreType.DMA((2,)), pltpu.SemaphoreType.DMA(())],
        compiler_params=pltpu.CompilerParams(
            dimension_semantics=("arbitrary",), vmem_limit_bytes=VMEM_LIMIT),
        name="experts",
    )(ecol, xs, wg, wu, wd)


def _combine_kernel(x1_ref, gc_ref, gfin_ref, y_ref, out_ref, *, nsub):
    tc, d = x1_ref.shape
    g1 = gc_ref[:, 0:1]
    g2 = gc_ref[:, 1:2]
    pieces = []
    ssq = jnp.zeros((tc, 1), jnp.float32)
    y1_cols = _load_packed_rows(y_ref.at[0], tc, nsub)
    y2_cols = _load_packed_rows(y_ref.at[1], tc, nsub)
    for j, (y1, y2) in enumerate(zip(y1_cols, y2_cols)):
        z = x1_ref[:, j * LANES:(j + 1) * LANES] + (g1 * y1 + g2 * y2)
        pieces.append(z)
        ssq = ssq + jnp.sum(z * z, axis=-1, keepdims=True)
    scale = lax.rsqrt(ssq / d + EPS)
    for j, z in enumerate(pieces):
        out_ref[:, j * LANES:(j + 1) * LANES] = z * scale * gfin_ref[:, j * LANES:(j + 1) * LANES]


def _combine_call(x1, gates_col, gfin, y_tok, nsub, out_prev, tok0, n_tok_total):
    n_tok, d = x1.shape
    tc = min(COMBINE_TILE, n_tok)
    assert n_tok % tc == 0 and tok0 % tc == 0
    l = y_tok.shape[-1]
    in_specs = [
        pl.BlockSpec((tc, d), lambda i: (i, 0)),
        pl.BlockSpec((tc, LANES), lambda i: (i, 0)),
        pl.BlockSpec((1, d), lambda i: (0, 0)),
        pl.BlockSpec((TOP_K, tc * nsub, l), lambda i: (0, i, 0)),
    ]
    args = [x1, gates_col, gfin, y_tok]
    aliases = {}
    kern = functools.partial(_combine_kernel, nsub=nsub)
    if out_prev is not None:
        in_specs.append(pl.BlockSpec(memory_space=pl.ANY))
        args.append(out_prev)
        aliases = {len(args) - 1: 0}
        kern = lambda x1_ref, gc_ref, gfin_ref, y_ref, prev_ref, out_ref: _combine_kernel(
            x1_ref, gc_ref, gfin_ref, y_ref, out_ref, nsub=nsub)
    return pl.pallas_call(
        kern,
        grid=(n_tok // tc,),
        in_specs=in_specs,
        out_specs=pl.BlockSpec((tc, d), lambda i: (i + tok0 // tc, 0)),
        out_shape=jax.ShapeDtypeStruct((n_tok_total, d), jnp.float32),
        input_output_aliases=aliases,
        compiler_params=pltpu.CompilerParams(
            dimension_semantics=("arbitrary",), vmem_limit_bytes=VMEM_LIMIT),
        name="combine",
    )(*args)


def _prepare_layer(l, norm_mix_g, w_in, b_in, conv_w, conv_b, conv_ln_g, conv_ln_b, qk_conv_w,
                   qk_conv_b, mlstm_norm_g, w_out, norm_ffn_g, router_group_w, router_group_b,
                   router_expert_w, router_expert_b, ts):
    f32, bf16 = jnp.float32, jnp.bfloat16
    d = w_in.shape[1]
    dc = conv_w.shape[-1]
    dm = mlstm_norm_g.shape[-1]
    n_heads = (w_in.shape[-1] - 2 * dc - 4 * dm) // 2
    n_groups = router_group_w.shape[-1]
    n_experts = router_expert_w.shape[-1]
    wi, bi = w_in[l], b_in[l]
    o_qk, o_vo, o_if = 2 * dc, 2 * dc + 2 * dm, 2 * dc + 4 * dm
    w_if = wi[:, o_if:]
    b_if = bi[o_if:]
    wr_t = jnp.zeros((SUBLANES + n_experts, d), f32)
    wr_t = wr_t.at[:n_groups].set(router_group_w[l].T).at[SUBLANES:].set(router_expert_w[l].T)
    br = jnp.zeros((SUBLANES + n_experts, 1), f32)
    br = br.at[:n_groups, 0].set(router_group_b[l]).at[SUBLANES:, 0].set(router_expert_b[l])
    wr_hi = wr_t.astype(bf16)
    wr_lo = (wr_t - wr_hi.astype(f32)).astype(bf16)
    pad_rows = lambda a, n: jnp.pad(a, ((0, n - a.shape[0]), (0, 0)))
    idx = jnp.arange(ts)
    cidx = jnp.arange(CHUNK)
    return dict(
        gmix=norm_mix_g[l][None, :],
        wcv=wi[:, :o_qk].astype(bf16), bcv=bi[None, :o_qk],
        wqk=wi[:, o_qk:o_vo].astype(bf16), bqk=bi[None, o_qk:o_vo],
        wvo=wi[:, o_vo:o_if].astype(bf16), bvo=bi[None, o_vo:o_if],
        wif=jnp.pad(w_if, ((0, 0), (0, LANES - 2 * n_heads))).astype(bf16),
        bif=jnp.pad(b_if, (0, LANES - 2 * n_heads))[None, :],
        wift=pad_rows(w_if.T, SUBLANES).astype(bf16),
        bift=pad_rows(b_if[:, None], SUBLANES),
        cw=jnp.repeat(conv_w[l], SUBLANES, axis=0), cb=conv_b[l][None, :],
        lng=conv_ln_g[l][None, :], lnb=conv_ln_b[l][None, :],
        qkw=jnp.repeat(qk_conv_w[l], SUBLANES, axis=0), qkb=qk_conv_b[l][None, :],
        mng=mlstm_norm_g[l][None, :],
        wout=w_out[l].astype(bf16), gffn=norm_ffn_g[l][None, :],
        wrh=wr_hi, wrl=wr_lo, br=br,
        su=(idx[:, None] < idx[None, :]).astype(bf16),
        tril=(cidx[:, None] >= cidx[None, :]).astype(bf16),
    ), dict(n_heads=n_heads, conv_width=conv_w.shape[1], qk_width=qk_conv_w.shape[1],
            n_groups=n_groups, epg=n_experts // n_groups)


def kernel(x, norm_mix_g, w_in, b_in, conv_w, conv_b, conv_ln_g, conv_ln_b, qk_conv_w, qk_conv_b,
           mlstm_norm_g, w_out, norm_ffn_g, router_group_w, router_group_b, router_expert_w,
           router_expert_b, expert_w_gate, expert_w_up, expert_w_down, final_norm_g):
    bsz, seq, d = x.shape
    depth = w_in.shape[0]
    assert depth == 1, "the combine kernel fuses the final norm, so exactly one layer is supported"
    n_experts = router_expert_w.shape[-1]
    n_parts = N_PARTS if bsz % N_PARTS == 0 else 1
    pb = bsz // n_parts
    n_tok = pb * seq
    n_assign = n_tok * TOP_K
    n_blocks = -(-n_assign // EXPERT_BLOCK) + n_experts
    n_slots = n_blocks * EXPERT_BLOCK
    ts = min(SEQ_TILE, seq)
    nsub = d // PACK_COLS
    l = 0
    params, dims = _prepare_layer(
        l, norm_mix_g, w_in, b_in, conv_w, conv_b, conv_ln_g, conv_ln_b, qk_conv_w, qk_conv_b,
        mlstm_norm_g, w_out, norm_ffn_g, router_group_w, router_group_b, router_expert_w,
        router_expert_b, ts)
    wg, wu, wd = expert_w_gate[l], expert_w_up[l], expert_w_down[l]
    out = None
    prev_dest = None
    for part in range(n_parts):
        b0 = jnp.full((1,), part * pb, jnp.int32)
        if prev_dest is not None:
            b0 = b0 + jnp.minimum(prev_dest[0, 0, :1], 0)
        x1, h2_rows, route_i, gates_col, counts = _mixer_call(x, params, b0, pb, **dims)

        cnt = counts[:, 0].astype(jnp.int32)
        padded = (cnt + EXPERT_BLOCK - 1) // EXPERT_BLOCK * EXPERT_BLOCK
        padded_ends = jnp.cumsum(padded)
        padded_starts = padded_ends - padded
        ecol = jnp.pad(jnp.stack([padded_ends // EXPERT_BLOCK, padded_starts + cnt], axis=1),
                       ((0, 0), (0, LANES - 2))).astype(jnp.float32)

        dest = _dest_call(padded_starts, route_i.reshape(pb * (seq // ts), SUBLANES, ts))
        prev_dest = dest
        xs = _sc_dispatch(dest, h2_rows.reshape(n_tok, nsub, LANES), n_slots)
        ys = _expert_call(ecol, xs.reshape(n_slots * nsub, LANES), wg, wu, wd, nsub)
        y_tok = _sc_gather(dest, ys.reshape(n_slots, nsub, LANES), n_tok)
        out = _combine_call(x1.reshape(n_tok, d), gates_col.reshape(n_tok, LANES),
                            final_norm_g[None, :], y_tok.reshape(TOP_K, n_tok * nsub, LANES), nsub,
                            out, part * n_tok, bsz * seq)
    return out.reshape(bsz, seq, d)
```

```python
import functools

import jax
import jax.numpy as jnp
from jax import lax
from jax.experimental import pallas as pl
from jax.experimental.pallas import tpu as pltpu
from jax.experimental.pallas import tpu_sc as plsc

EPS = 1e-6
LANES = 128
SUBLANES = 8
CHUNK = 128
TOP_K = 2
SEQ_TILE = 512
CONV_ROWS = 32
CONV_HALO = 32
QK_HALO = 8
EXPERT_BLOCK = 512
SC_CHUNK = 128
COMBINE_TILE = 1024
N_PARTS = 2
PACK_COLS = 2 * LANES
VMEM_LIMIT = 56 * 1024 * 1024

_NT = (((1,), (1,)), ((), ()))
_TN = (((0,), (0,)), ((), ()))
_HI16 = 0xFFFF0000


def _dot(a, b):
    return jnp.dot(a, b, preferred_element_type=jnp.float32)


def _silu(x):
    return x * jax.nn.sigmoid(x)


def _log_sigmoid(x):
    return jnp.minimum(x, 0.0) - jnp.log1p(jnp.exp(-jnp.abs(x)))


def _split_bf16(x, parts):
    out = []
    for _ in range(parts):
        p = x.astype(jnp.bfloat16)
        out.append(p)
        x = x - p.astype(jnp.float32)
    return out


def _store_packed_rows(ref, v):
    n, d = v.shape
    half = d // 2
    nsub = half // LANES
    as_bits = lambda a: pltpu.bitcast(a.astype(jnp.bfloat16).astype(jnp.float32), jnp.uint32)
    word = (as_bits(v[:, :half]) >> 16) | (as_bits(v[:, half:]) & jnp.uint32(_HI16))
    for s in range(nsub):
        ref[pl.ds(s, n, stride=nsub), :] = word[:, s * LANES:(s + 1) * LANES]


def _load_packed_rows(ref, n, nsub, live=None):
    lo, hi = [], []
    for s in range(nsub):
        w = ref[pl.ds(s, n, stride=nsub), :]
        if live is not None:
            w = jnp.where(live, w, jnp.uint32(0))
        lo.append(pltpu.bitcast(w << 16, jnp.float32))
        hi.append(pltpu.bitcast(w & jnp.uint32(_HI16), jnp.float32))
    return lo + hi


def _mixer_kernel(b0_ref, x_ref, gmix_ref, wcv_ref, bcv_ref, wqk_ref, bqk_ref, wvo_ref, bvo_ref,
                  wif_ref, bif_ref, wift_ref, bift_ref, cw_ref, cb_ref, lng_ref, lnb_ref,
                  qkw_ref, qkb_ref, mng_ref, wout_ref, gffn_ref, wrh_ref, wrl_ref, br_ref,
                  su_ref, tril_ref,
                  x1_ref, h2_ref, ri_ref, gc_ref, cnt_ref,
                  uext, qkext, q_s, k_s, v_s, o_s, cn_s, m_s, y_s, cnt_s,
                  *, n_heads, conv_width, qk_width, n_groups, epg):
    ts, d = x_ref.shape
    dc = cw_ref.shape[1]
    dm = mng_ref.shape[1]
    dh = dm // n_heads
    n_chunks = ts // CHUNK
    n_experts = n_groups * epg
    b_idx = pl.program_id(0)
    t_idx = pl.program_id(1)

    @pl.when(t_idx == 0)
    def _reset_sequence_state():
        uext[0:CONV_HALO, :] = jnp.zeros((CONV_HALO, dc), jnp.float32)
        uext[CONV_HALO + ts:, :] = jnp.zeros((SUBLANES, dc), jnp.float32)
        qkext[0:QK_HALO, :] = jnp.zeros((QK_HALO, 2 * dm), jnp.float32)
        cn_s[...] = jnp.zeros(cn_s.shape, jnp.float32)
        m_s[...] = jnp.zeros(m_s.shape, jnp.float32)

    @pl.when((t_idx == 0) & (b_idx == 0))
    def _reset_counts():
        cnt_s[...] = jnp.zeros(cnt_s.shape, jnp.float32)

    x = x_ref[...]
    h = x * lax.rsqrt(jnp.mean(x * x, axis=-1, keepdims=True) + EPS) * gmix_ref[...]
    hb = h.astype(jnp.bfloat16)

    zc = _dot(hb, wcv_ref[...]) + bcv_ref[...]
    uext[CONV_HALO:CONV_HALO + ts, :] = zc[:, :dc] * jax.nn.sigmoid(zc[:, dc:])
    first = CONV_HALO - (conv_width - 1)
    sub = lax.broadcasted_iota(jnp.int32, (SUBLANES, dc), 0)
    n_grp = CONV_ROWS // SUBLANES
    for rb in range(ts // CONV_ROWS):
        r0 = rb * CONV_ROWS
        acc = jnp.zeros((CONV_ROWS, dc), jnp.float32) + cb_ref[...]
        for r in range(SUBLANES):
            part = None
            for j in range(conv_width):
                if (first + j) % SUBLANES != r:
                    continue
                a0 = r0 + (first + j) - r
                w8 = cw_ref[j * SUBLANES:(j + 1) * SUBLANES, :]
                term = jnp.concatenate([w8] * (n_grp + 1), axis=0) * uext[a0:a0 + CONV_ROWS + SUBLANES, :]
                part = term if part is None else part + term
            if part is None:
                continue
            if r == 0:
                acc = acc + part[:CONV_ROWS, :]
                continue
            rolled = [pltpu.roll(part[g * SUBLANES:(g + 1) * SUBLANES, :], SUBLANES - r, axis=0)
                      for g in range(n_grp + 1)]
            acc = acc + jnp.concatenate(
                [jnp.where(sub < SUBLANES - r, rolled[g], rolled[g + 1]) for g in range(n_grp)], axis=0)
        mu = jnp.mean(acc, axis=-1, keepdims=True)
        cen = acc - mu
        var = jnp.mean(cen * cen, axis=-1, keepdims=True)
        yn = cen * lax.rsqrt(var + EPS) * lng_ref[...] + lnb_ref[...]
        y_s[r0:r0 + CONV_ROWS, 0:dc] = _silu(yn).astype(jnp.bfloat16)
    uext[0:CONV_HALO, :] = uext[ts:ts + CONV_HALO, :]

    qkext[QK_HALO:QK_HALO + ts, :] = _dot(hb, wqk_ref[...]) + bqk_ref[...]
    qfirst = QK_HALO - (qk_width - 1)
    k_scale = dh ** -0.5
    for rb in range(ts // CONV_ROWS):
        r0 = rb * CONV_ROWS
        acc = jnp.zeros((CONV_ROWS, 2 * dm), jnp.float32) + qkb_ref[...]
        for j in range(qk_width):
            w8 = qkw_ref[j * SUBLANES:(j + 1) * SUBLANES, :]
            acc = acc + (jnp.concatenate([w8] * (CONV_ROWS // SUBLANES), axis=0)
                         * qkext[r0 + qfirst + j:r0 + qfirst + j + CONV_ROWS, :])
        act = _silu(acc)
        q_s[r0:r0 + CONV_ROWS, :] = act[:, :dm].astype(jnp.bfloat16)
        k_s[r0:r0 + CONV_ROWS, :] = (act[:, dm:] * k_scale).astype(jnp.bfloat16)
    qkext[0:QK_HALO, :] = qkext[ts:ts + QK_HALO, :]

    zvo = _dot(hb, wvo_ref[...]) + bvo_ref[...]
    v_s[...] = zvo[:, :dm].astype(jnp.bfloat16)
    o_s[...] = jax.nn.sigmoid(zvo[:, dm:])

    zif_col = _dot(hb, wif_ref[...]) + bif_ref[...]
    zif_row = lax.dot_general(wift_ref[...], hb, _NT,
                              preferred_element_type=jnp.float32) + bift_ref[...]
    logf_col = _log_sigmoid(zif_col)
    logf_row = _log_sigmoid(zif_row)
    lane = lax.broadcasted_iota(jnp.int32, (SUBLANES, CHUNK), 1)
    causal = (lax.broadcasted_iota(jnp.int32, (CHUNK, CHUNK), 0)
              >= lax.broadcasted_iota(jnp.int32, (CHUNK, CHUNK), 1))
    ones_blk = jnp.ones((CHUNK, dh), jnp.bfloat16)
    tril = tril_ref[...]

    for c in range(n_chunks):
        r0 = c * CHUNK
        b_row = logf_row[:, r0:r0 + CHUNK]
        shift = 1
        while shift < CHUNK:
            b_row = b_row + jnp.where(lane >= shift, pltpu.roll(b_row, shift, axis=1), 0.0)
            shift *= 2
        b_col = sum(_dot(tril, p) for p in _split_bf16(logf_col[r0:r0 + CHUNK, :], 3))
        i_col_all = zif_col[r0:r0 + CHUNK, :]
        i_row_all = zif_row[:, r0:r0 + CHUNK]
        for hd in range(n_heads):
            c0 = hd * dh
            b_c = b_col[:, n_heads + hd:n_heads + hd + 1]
            i_c = i_col_all[:, hd:hd + 1]
            b_r = b_row[n_heads + hd:n_heads + hd + 1, :]
            i_r = i_row_all[hd:hd + 1, :]
            m_prev = m_s[hd, 0:1, 0:1]
            q = q_s[r0:r0 + CHUNK, c0:c0 + dh]
            k = k_s[r0:r0 + CHUNK, c0:c0 + dh]
            vaug = jnp.concatenate([v_s[r0:r0 + CHUNK, c0:c0 + dh], ones_blk], axis=-1)
            cn = cn_s[hd]

            dmat = jnp.where(causal, b_c + (i_r - b_r), -jnp.inf)
            g = b_c + m_prev
            m_t = jnp.maximum(g, jnp.max(dmat, axis=-1, keepdims=True))
            w_intra = jnp.exp(dmat - m_t)
            w_inter = jnp.exp(g - m_t)
            s = lax.dot_general(q, k, _NT, preferred_element_type=jnp.float32) * w_intra
            nd = _dot(s.astype(jnp.bfloat16), vaug) + w_inter * _dot(q, cn.astype(jnp.bfloat16))
            hval = nd[:, :dh] / jnp.maximum(jnp.abs(nd[:, dh:]), jnp.exp(-m_t))

            b_last = b_r[:, CHUNK - 1:CHUNK]
            logw_c = b_last - b_c + i_c
            m_new = jnp.maximum(b_last + m_prev, jnp.max(logw_c, axis=0, keepdims=True))
            decay = jnp.exp(b_last + m_prev - m_new)
            kw = (k.astype(jnp.float32) * jnp.exp(logw_c - m_new)).astype(jnp.bfloat16)
            cn_s[hd] = decay * cn + lax.dot_general(kw, vaug, _TN,
                                                    preferred_element_type=jnp.float32)
            m_s[hd] = jnp.broadcast_to(m_new, m_s.shape[1:])

            hm = o_s[r0:r0 + CHUNK, c0:c0 + dh] * hval
            mu = jnp.mean(hm, axis=-1, keepdims=True)
            cen = hm - mu
            var = jnp.mean(cen * cen, axis=-1, keepdims=True)
            hn = cen * lax.rsqrt(var + EPS) * mng_ref[:, c0:c0 + dh]
            y_s[r0:r0 + CHUNK, dc + c0:dc + c0 + dh] = hn.astype(jnp.bfloat16)

    x1 = x_ref[...] + (_dot(y_s[:, 0:dc], wout_ref[0:dc, :]) + _dot(y_s[:, dc:d], wout_ref[dc:d, :]))
    x1_ref[...] = x1
    h2 = x1 * lax.rsqrt(jnp.mean(x1 * x1, axis=-1, keepdims=True) + EPS) * gffn_ref[...]
    _store_packed_rows(h2_ref, h2)

    h2_hi, h2_lo = _split_bf16(h2, 2)
    lg = (lax.dot_general(wrh_ref[...], h2_hi, _NT, preferred_element_type=jnp.float32)
          + lax.dot_general(wrl_ref[...], h2_hi, _NT, preferred_element_type=jnp.float32)
          + lax.dot_general(wrh_ref[...], h2_lo, _NT, preferred_element_type=jnp.float32)
          + br_ref[...])
    gl = lg[0:n_groups, :]
    gidx = lax.broadcasted_iota(jnp.int32, (n_groups, ts), 0)
    gmax = jnp.max(gl, axis=0, keepdims=True)
    grp = jnp.min(jnp.where(gl == gmax, gidx, n_groups), axis=0, keepdims=True)
    p_grp = 1.0 / jnp.sum(jnp.exp(gl - gmax), axis=0, keepdims=True)
    in_group = jnp.zeros((epg, ts), jnp.float32)
    for gi in range(n_groups):
        in_group = jnp.where(grp == gi, lg[SUBLANES + gi * epg:SUBLANES + (gi + 1) * epg, :], in_group)
    eidx = lax.broadcasted_iota(jnp.int32, (epg, ts), 0)
    v1 = jnp.max(in_group, axis=0, keepdims=True)
    i1 = jnp.min(jnp.where(in_group == v1, eidx, epg), axis=0, keepdims=True)
    rest = jnp.where(eidx == i1, -jnp.inf, in_group)
    v2 = jnp.max(rest, axis=0, keepdims=True)
    i2 = jnp.min(jnp.where(rest == v2, eidx, epg), axis=0, keepdims=True)
    e21 = jnp.exp(v2 - v1)
    g1 = p_grp / (1.0 + e21)
    g2 = g1 * e21
    e1 = grp * epg + i1
    e2 = grp * epg + i2

    xidx = lax.broadcasted_iota(jnp.int32, (n_experts, ts), 0)
    sel1 = xidx == e1
    sel2 = xidx == e2
    onehot = jnp.where(sel1 | sel2, 1.0, 0.0).astype(jnp.bfloat16)
    before = cnt_s[:, 0:1] + _dot(onehot, su_ref[...])
    rank1 = jnp.sum(jnp.where(sel1, before, 0.0), axis=0, keepdims=True)
    rank2 = jnp.sum(jnp.where(sel2, before, 0.0), axis=0, keepdims=True)
    cnt_new = cnt_s[...] + jnp.sum(onehot.astype(jnp.float32), axis=1, keepdims=True)
    cnt_s[...] = cnt_new
    cnt_ref[...] = cnt_new

    zero_i = jnp.zeros((SUBLANES - 4, ts), jnp.int32)
    ri_ref[...] = jnp.concatenate(
        [e1, e2, rank1.astype(jnp.int32), rank2.astype(jnp.int32), zero_i], axis=0)
    gates_row = jnp.concatenate([g1, g2, jnp.zeros((LANES - 2, ts), jnp.float32)], axis=0)
    gc_ref[...] = gates_row.T


def _mixer_call(x, p, b0, bsz, *, n_heads, conv_width, qk_width, n_groups, epg):
    _, seq, d = x.shape
    ts = min(SEQ_TILE, seq)
    nt = seq // ts
    dc = p["cw"].shape[1]
    dm = p["mng"].shape[1]
    dh = dm // n_heads
    n_experts = n_groups * epg
    assert seq % ts == 0 and ts % CHUNK == 0 and dh == LANES and d % PACK_COLS == 0
    assert epg == SUBLANES and n_groups <= SUBLANES and conv_width - 1 <= CONV_HALO

    consts = [p[n] for n in ("gmix", "wcv", "bcv", "wqk", "bqk", "wvo", "bvo", "wif", "bif",
                             "wift", "bift", "cw", "cb", "lng", "lnb", "qkw", "qkb", "mng",
                             "wout", "gffn", "wrh", "wrl", "br", "su", "tril")]

    def const_spec(a):
        return pl.BlockSpec(a.shape, lambda b, t, b0r: (0,) * a.ndim)

    kern = functools.partial(_mixer_kernel, n_heads=n_heads, conv_width=conv_width,
                             qk_width=qk_width, n_groups=n_groups, epg=epg)
    grid_spec = pltpu.PrefetchScalarGridSpec(
        num_scalar_prefetch=1,
        grid=(bsz, nt),
        in_specs=([pl.BlockSpec((None, ts, d), lambda b, t, b0r: (b + b0r[0], t, 0))]
                  + [const_spec(a) for a in consts]),
        out_specs=[
            pl.BlockSpec((None, ts, d), lambda b, t, b0r: (b, t, 0)),
            pl.BlockSpec((None, ts * (d // PACK_COLS), LANES), lambda b, t, b0r: (b, t, 0)),
            pl.BlockSpec((None, None, SUBLANES, ts), lambda b, t, b0r: (b, t, 0, 0)),
            pl.BlockSpec((None, ts, LANES), lambda b, t, b0r: (b, t, 0)),
            pl.BlockSpec((n_experts, LANES), lambda b, t, b0r: (0, 0)),
        ],
        scratch_shapes=[
            pltpu.VMEM((CONV_HALO + ts + SUBLANES, dc), jnp.float32),
            pltpu.VMEM((QK_HALO + ts, 2 * dm), jnp.float32),
            pltpu.VMEM((ts, dm), jnp.bfloat16),
            pltpu.VMEM((ts, dm), jnp.bfloat16),
            pltpu.VMEM((ts, dm), jnp.bfloat16),
            pltpu.VMEM((ts, dm), jnp.float32),
            pltpu.VMEM((n_heads, dh, 2 * dh), jnp.float32),
            pltpu.VMEM((n_heads, SUBLANES, LANES), jnp.float32),
            pltpu.VMEM((ts, d), jnp.bfloat16),
            pltpu.VMEM((n_experts, LANES), jnp.float32),
        ],
    )
    return pl.pallas_call(
        kern,
        grid_spec=grid_spec,
        out_shape=[
            jax.ShapeDtypeStruct((bsz, seq, d), jnp.float32),
            jax.ShapeDtypeStruct((bsz, seq * (d // PACK_COLS), LANES), jnp.uint32),
            jax.ShapeDtypeStruct((bsz, nt, SUBLANES, ts), jnp.int32),
            jax.ShapeDtypeStruct((bsz, seq, LANES), jnp.float32),
            jax.ShapeDtypeStruct((n_experts, LANES), jnp.float32),
        ],
        compiler_params=pltpu.CompilerParams(
            dimension_semantics=("arbitrary", "arbitrary"), vmem_limit_bytes=VMEM_LIMIT),
        name="mixer_router",
    )(b0, x, *consts)


def _sc_workers():
    info = plsc.get_sparse_core_info()
    return info.num_cores, info.num_cores * info.num_subcores


def _dest_kernel(pstart_ref, route_ref, dest_ref, *, n_experts):
    tiles, _, ts = route_ref.shape
    per = ts // SC_CHUNK
    for k in range(TOP_K):
        e = route_ref[:, k, :]
        rank = route_ref[:, TOP_K + k, :]
        start = jnp.zeros((tiles, ts), jnp.int32)
        for ex in range(n_experts):
            start = jnp.where(e == ex, pstart_ref[ex], start)
        slot = start + rank
        for j in range(per):
            dest_ref[k, pl.ds(j, tiles, stride=per), :] = slot[:, j * SC_CHUNK:(j + 1) * SC_CHUNK]


def _dest_call(padded_starts, route):
    tiles, _, ts = route.shape
    assert ts % SC_CHUNK == 0 and SC_CHUNK == LANES
    n_rows = tiles * ts // SC_CHUNK
    return pl.pallas_call(
        functools.partial(_dest_kernel, n_experts=padded_starts.shape[0]),
        grid_spec=pltpu.PrefetchScalarGridSpec(
            num_scalar_prefetch=1,
            grid=(1,),
            in_specs=[pl.BlockSpec(route.shape, lambda i, ps: (0, 0, 0))],
            out_specs=pl.BlockSpec((TOP_K, n_rows, SC_CHUNK), lambda i, ps: (0, 0, 0)),
        ),
        out_shape=jax.ShapeDtypeStruct((TOP_K, n_rows, SC_CHUNK), jnp.int32),
        compiler_params=pltpu.CompilerParams(dimension_semantics=("arbitrary",)),
        name="slot_ids",
    )(padded_starts, route)


def _sc_dispatch(dest, rows, n_slots):
    n_tok, nsub, l = rows.shape
    top_k, n_rows, ch = dest.shape
    n_cores, n_workers = _sc_workers()
    assert n_rows * ch == n_tok and n_rows % n_workers == 0
    n_chunks = n_rows // n_workers
    tpw = n_chunks * ch

    @functools.partial(
        pl.kernel,
        mesh=plsc.VectorSubcoreMesh(core_axis_name="c", subcore_axis_name="s"),
        out_type=jax.ShapeDtypeStruct((n_slots, nsub, l), rows.dtype),
        scratch_types=[pltpu.VMEM((top_k, n_chunks, ch), jnp.int32),
                       pltpu.VMEM((ch, nsub, l), rows.dtype),
                       pltpu.SemaphoreType.DMA],
        name="sc_dispatch",
    )
    def run(dest_hbm, rows_hbm, out_hbm, idx_v, buf, sem):
        wid = lax.axis_index("s") * n_cores + lax.axis_index("c")
        for k in range(top_k):
            pltpu.sync_copy(dest_hbm.at[k, pl.ds(wid * n_chunks, n_chunks)], idx_v.at[k])

        @pl.loop(0, n_chunks)
        def _(j):
            pltpu.sync_copy(rows_hbm.at[pl.ds(wid * tpw + j * ch, ch)], buf)
            for k in range(top_k):
                pltpu.async_copy(buf, out_hbm.at[idx_v.at[k, j]], sem).wait()

    return run(dest, rows)


def _sc_gather(dest, ys, n_tok):
    _, nsub, l = ys.shape
    top_k, n_rows, ch = dest.shape
    n_cores, n_workers = _sc_workers()
    assert n_rows * ch == n_tok and n_rows % n_workers == 0
    n_chunks = n_rows // n_workers
    tpw = n_chunks * ch

    @functools.partial(
        pl.kernel,
        mesh=plsc.VectorSubcoreMesh(core_axis_name="c", subcore_axis_name="s"),
        out_type=jax.ShapeDtypeStruct((top_k, n_tok, nsub, l), ys.dtype),
        scratch_types=[pltpu.VMEM((top_k, n_chunks, ch), jnp.int32),
                       pltpu.VMEM((ch, nsub, l), ys.dtype),
                       pltpu.SemaphoreType.DMA],
        name="sc_gather",
    )
    def run(dest_hbm, ys_hbm, out_hbm, idx_v, buf, sem):
        wid = lax.axis_index("s") * n_cores + lax.axis_index("c")
        for k in range(top_k):
            pltpu.sync_copy(dest_hbm.at[k, pl.ds(wid * n_chunks, n_chunks)], idx_v.at[k])

        @pl.loop(0, n_chunks)
        def _(j):
            for k in range(top_k):
                pltpu.async_copy(ys_hbm.at[idx_v.at[k, j]], buf, sem).wait()
                pltpu.sync_copy(buf, out_hbm.at[k, pl.ds(wid * tpw + j * ch, ch)])

    return run(dest, ys)


_T_EXPERT, _T_VALID, _T_NEXT, _T_RUN, _T_USED = range(5)


def _expert_kernel(ecol_ref, xs_ref, wg_hbm, wu_hbm, wd_hbm, ys_ref,
                   wg_f, wu_f, wd_f, wg_b, wu_b, wd_b, tbl_v, tbl_s, sems, tsem, *, nsub, n_blocks):
    i = pl.program_id(0)
    blk = xs_ref.shape[0] // nsub
    l = xs_ref.shape[1]
    n_experts = ecol_ref.shape[0]

    @pl.when(i == 0)
    def _build_block_table():
        width = tbl_v.shape[1]
        blk_end = ecol_ref[:, 0:1]
        slot_end = ecol_ref[:, 1:2]
        b = lax.broadcasted_iota(jnp.int32, (n_experts, width), 1).astype(jnp.float32)
        e_id = lax.broadcasted_iota(jnp.int32, (n_experts, width), 0).astype(jnp.float32)
        be = jnp.minimum(jnp.sum(jnp.where(b >= blk_end, 1.0, 0.0), axis=0, keepdims=True),
                         n_experts - 1.0)
        mine = e_id == be
        valid = jnp.clip(jnp.sum(jnp.where(mine, slot_end, 0.0), axis=0, keepdims=True)
                         - b[0:1, :] * blk, 0.0, float(blk))
        present = jnp.max(jnp.where(mine & (b < n_blocks), 1.0, 0.0), axis=1, keepdims=True) > 0.0
        run = jnp.sum(jnp.where((e_id < be) & present, 1.0, 0.0), axis=0, keepdims=True)
        nxt = jnp.min(jnp.where((e_id > be) & present, e_id, float(n_experts)), axis=0, keepdims=True)
        nxt = jnp.where(nxt == n_experts, -1.0, nxt)
        used = jnp.broadcast_to(jnp.max(blk_end, axis=0, keepdims=True), (1, width))
        rows = [be, valid, nxt, run, used] + [jnp.zeros((1, width), jnp.float32)] * (SUBLANES - 5)
        tbl_v[...] = jnp.concatenate(rows, axis=0).astype(jnp.int32)
        to_smem = pltpu.make_async_copy(tbl_v, tbl_s, tsem)
        to_smem.start()
        to_smem.wait()

    expert = tbl_s[_T_EXPERT, i]
    slot = tbl_s[_T_RUN, i] % 2

    def weight_copies(e, s):
        return [pltpu.make_async_copy(src.at[e], dst.at[s], sems.at[s])
                for src, dst in ((wg_hbm, wg_f), (wu_hbm, wu_f), (wd_hbm, wd_f))]

    @pl.when(i == 0)
    def _fetch_first_weights():
        for c in weight_copies(expert, 0):
            c.start()

    @pl.when((i == 0) | (expert != tbl_s[_T_EXPERT, jnp.maximum(i - 1, 0)]))
    def _new_expert_weights():
        for c in weight_copies(expert, slot):
            c.wait()
        wg_b[...] = wg_f[slot].astype(jnp.bfloat16)
        wu_b[...] = wu_f[slot].astype(jnp.bfloat16)
        wd_b[...] = wd_f[slot].astype(jnp.bfloat16)

        @pl.when(tbl_s[_T_NEXT, i] >= 0)
        def _fetch_next_run_weights():
            for c in weight_copies(tbl_s[_T_NEXT, i], 1 - slot):
                c.start()

    @pl.when(i < tbl_s[_T_USED, i])
    def _compute():
        live = lax.broadcasted_iota(jnp.int32, (blk, l), 0) < tbl_s[_T_VALID, i]
        xb = jnp.concatenate(
            [c.astype(jnp.bfloat16) for c in _load_packed_rows(xs_ref, blk, nsub, live)], axis=-1)
        a = _dot(xb, wg_b[...])
        u = _dot(xb, wu_b[...])
        _store_packed_rows(ys_ref, _dot((_silu(a) * u).astype(jnp.bfloat16), wd_b[...]))

    @pl.when(i >= tbl_s[_T_USED, i])
    def _unused_block():
        ys_ref[...] = jnp.zeros(ys_ref.shape, ys_ref.dtype)


def _expert_call(ecol, xs, wg, wu, wd, nsub):
    rows, l = xs.shape
    n_blocks = rows // (EXPERT_BLOCK * nsub)
    _, d, de = wg.shape
    width = -(-n_blocks // LANES) * LANES
    hbm = pl.BlockSpec(memory_space=pl.ANY)
    return pl.pallas_call(
        functools.partial(_expert_kernel, nsub=nsub, n_blocks=n_blocks),
        grid=(n_blocks,),
        in_specs=[pl.BlockSpec(ecol.shape, lambda i: (0, 0)),
                  pl.BlockSpec((EXPERT_BLOCK * nsub, l), lambda i: (i, 0)), hbm, hbm, hbm],
        out_specs=pl.BlockSpec((EXPERT_BLOCK * nsub, l), lambda i: (i, 0)),
        out_shape=jax.ShapeDtypeStruct((rows, l), xs.dtype),
        scratch_shapes=[pltpu.VMEM((2, d, de), wg.dtype), pltpu.VMEM((2, d, de), wu.dtype),
                        pltpu.VMEM((2, de, d), wd.dtype),
                        pltpu.VMEM((d, de), jnp.bfloat16), pltpu.VMEM((d, de), jnp.bfloat16),
                        pltpu.VMEM((de, d), jnp.bfloat16),
                        pltpu.VMEM((SUBLANES, width), jnp.int32),
                        pltpu.SMEM((SUBLANES, width), jnp.int32),
                        pltpu.SemaphoreType.DMA((2,)), pltpu.SemaphoreType.DMA(())],
        compiler_params=pltpu.CompilerParams(
            dimension_semantics=("arbitrary",), vmem_limit_bytes=VMEM_LIMIT),
        name="experts",
    )(ecol, xs, wg, wu, wd)


def _combine_kernel(x1_ref, gc_ref, gfin_ref, y_ref, out_ref, *, nsub):
    tc, d = x1_ref.shape
    g1 = gc_ref[:, 0:1]
    g2 = gc_ref[:, 1:2]
    pieces = []
    ssq = jnp.zeros((tc, 1), jnp.float32)
    y1_cols = _load_packed_rows(y_ref.at[0], tc, nsub)
    y2_cols = _load_packed_rows(y_ref.at[1], tc, nsub)
    for j, (y1, y2) in enumerate(zip(y1_cols, y2_cols)):
        z = x1_ref[:, j * LANES:(j + 1) * LANES] + (g1 * y1 + g2 * y2)
        pieces.append(z)
        ssq = ssq + jnp.sum(z * z, axis=-1, keepdims=True)
    scale = lax.rsqrt(ssq / d + EPS)
    for j, z in enumerate(pieces):
        out_ref[:, j * LANES:(j + 1) * LANES] = z * scale * gfin_ref[:, j * LANES:(j + 1) * LANES]


def _combine_call(x1, gates_col, gfin, y_tok, nsub, out_prev, tok0, n_tok_total):
    n_tok, d = x1.shape
    tc = min(COMBINE_TILE, n_tok)
    assert n_tok % tc == 0 and tok0 % tc == 0
    l = y_tok.shape[-1]
    in_specs = [
        pl.BlockSpec((tc, d), lambda i: (i, 0)),
        pl.BlockSpec((tc, LANES), lambda i: (i, 0)),
        pl.BlockSpec((1, d), lambda i: (0, 0)),
        pl.BlockSpec((TOP_K, tc * nsub, l), lambda i: (0, i, 0)),
    ]
    args = [x1, gates_col, gfin, y_tok]
    aliases = {}
    kern = functools.partial(_combine_kernel, nsub=nsub)
    if out_prev is not None:
        in_specs.append(pl.BlockSpec(memory_space=pl.ANY))
        args.append(out_prev)
        aliases = {len(args) - 1: 0}
        kern = lambda x1_ref, gc_ref, gfin_ref, y_ref, prev_ref, out_ref: _combine_kernel(
            x1_ref, gc_ref, gfin_ref, y_ref, out_ref, nsub=nsub)
    return pl.pallas_call(
        kern,
        grid=(n_tok // tc,),
        in_specs=in_specs,
        out_specs=pl.BlockSpec((tc, d), lambda i: (i + tok0 // tc, 0)),
        out_shape=jax.ShapeDtypeStruct((n_tok_total, d), jnp.float32),
        input_output_aliases=aliases,
        compiler_params=pltpu.CompilerParams(
            dimension_semantics=("arbitrary",), vmem_limit_bytes=VMEM_LIMIT),
        name="combine",
    )(*args)


def _prepare_layer(l, norm_mix_g, w_in, b_in, conv_w, conv_b, conv_ln_g, conv_ln_b, qk_conv_w,
                   qk_conv_b, mlstm_norm_g, w_out, norm_ffn_g, router_group_w, router_group_b,
                   router_expert_w, router_expert_b, ts):
    f32, bf16 = jnp.float32, jnp.bfloat16
    d = w_in.shape[1]
    dc = conv_w.shape[-1]
    dm = mlstm_norm_g.shape[-1]
    n_heads = (w_in.shape[-1] - 2 * dc - 4 * dm) // 2
    n_groups = router_group_w.shape[-1]
    n_experts = router_expert_w.shape[-1]
    wi, bi = w_in[l], b_in[l]
    o_qk, o_vo, o_if = 2 * dc, 2 * dc + 2 * dm, 2 * dc + 4 * dm
    w_if = wi[:, o_if:]
    b_if = bi[o_if:]
    wr_t = jnp.zeros((SUBLANES + n_experts, d), f32)
    wr_t = wr_t.at[:n_groups].set(router_group_w[l].T).at[SUBLANES:].set(router_expert_w[l].T)
    br = jnp.zeros((SUBLANES + n_experts, 1), f32)
    br = br.at[:n_groups, 0].set(router_group_b[l]).at[SUBLANES:, 0].set(router_expert_b[l])
    wr_hi = wr_t.astype(bf16)
    wr_lo = (wr_t - wr_hi.astype(f32)).astype(bf16)
    pad_rows = lambda a, n: jnp.pad(a, ((0, n - a.shape[0]), (0, 0)))
    idx = jnp.arange(ts)
    cidx = jnp.arange(CHUNK)
    return dict(
        gmix=norm_mix_g[l][None, :],
        wcv=wi[:, :o_qk].astype(bf16), bcv=bi[None, :o_qk],
        wqk=wi[:, o_qk:o_vo].astype(bf16), bqk=bi[None, o_qk:o_vo],
        wvo=wi[:, o_vo:o_if].astype(bf16), bvo=bi[None, o_vo:o_if],
        wif=jnp.pad(w_if, ((0, 0), (0, LANES - 2 * n_heads))).astype(bf16),
        bif=jnp.pad(b_if, (0, LANES - 2 * n_heads))[None, :],
        wift=pad_rows(w_if.T, SUBLANES).astype(bf16),
        bift=pad_rows(b_if[:, None], SUBLANES),
        cw=jnp.repeat(conv_w[l], SUBLANES, axis=0), cb=conv_b[l][None, :],
        lng=conv_ln_g[l][None, :], lnb=conv_ln_b[l][None, :],
        qkw=jnp.repeat(qk_conv_w[l], SUBLANES, axis=0), qkb=qk_conv_b[l][None, :],
        mng=mlstm_norm_g[l][None, :],
        wout=w_out[l].astype(bf16), gffn=norm_ffn_g[l][None, :],
        wrh=wr_hi, wrl=wr_lo, br=br,
        su=(idx[:, None] < idx[None, :]).astype(bf16),
        tril=(cidx[:, None] >= cidx[None, :]).astype(bf16),
    ), dict(n_heads=n_heads, conv_width=conv_w.shape[1], qk_width=qk_conv_w.shape[1],
            n_groups=n_groups, epg=n_experts // n_groups)


def kernel(x, norm_mix_g, w_in, b_in, conv_w, conv_b, conv_ln_g, conv_ln_b, qk_conv_w, qk_conv_b,
           mlstm_norm_g, w_out, norm_ffn_g, router_group_w, router_group_b, router_expert_w,
           router_expert_b, expert_w_gate, expert_w_up, expert_w_down, final_norm_g):
    bsz, seq, d = x.shape
    depth = w_in.shape[0]
    assert depth == 1, "the combine kernel fuses the final norm, so exactly one layer is supported"
    n_experts = router_expert_w.shape[-1]
    n_parts = N_PARTS if bsz % N_PARTS == 0 else 1
    pb = bsz // n_parts
    n_tok = pb * seq
    n_assign = n_tok * TOP_K
    n_blocks = -(-n_assign // EXPERT_BLOCK) + n_experts
    n_slots = n_blocks * EXPERT_BLOCK
    ts = min(SEQ_TILE, seq)
    nsub = d // PACK_COLS
    l = 0
    params, dims = _prepare_layer(
        l, norm_mix_g, w_in, b_in, conv_w, conv_b, conv_ln_g, conv_ln_b, qk_conv_w, qk_conv_b,
        mlstm_norm_g, w_out, norm_ffn_g, router_group_w, router_group_b, router_expert_w,
        router_expert_b, ts)
    wg, wu, wd = expert_w_gate[l], expert_w_up[l], expert_w_down[l]
    out = None
    prev_dest = None
    for part in range(n_parts):
        b0 = jnp.full((1,), part * pb, jnp.int32)
        if prev_dest is not None:
            b0 = b0 + jnp.minimum(prev_dest[0, 0, :1], 0)
        x1, h2_rows, route_i, gates_col, counts = _mixer_call(x, params, b0, pb, **dims)

        cnt = counts[:, 0].astype(jnp.int32)
        padded = (cnt + EXPERT_BLOCK - 1) // EXPERT_BLOCK * EXPERT_BLOCK
        padded_ends = jnp.cumsum(padded)
        padded_starts = padded_ends - padded
        ecol = jnp.pad(jnp.stack([padded_ends // EXPERT_BLOCK, padded_starts + cnt], axis=1),
                       ((0, 0), (0, LANES - 2))).astype(jnp.float32)

        dest = _dest_call(padded_starts, route_i.reshape(pb * (seq // ts), SUBLANES, ts))
        prev_dest = dest
        xs = _sc_dispatch(dest, h2_rows.reshape(n_tok, nsub, LANES), n_slots)
        ys = _expert_call(ecol, xs.reshape(n_slots * nsub, LANES), wg, wu, wd, nsub)
        y_tok = _sc_gather(dest, ys.reshape(n_slots, nsub, LANES), n_tok)
        out = _combine_call(x1.reshape(n_tok, d), gates_col.reshape(n_tok, LANES),
                            final_norm_g[None, :], y_tok.reshape(TOP_K, n_tok * nsub, LANES), nsub,
                            out, part * n_tok, bsz * seq)
    return out.reshape(bsz, seq, d)
```

```python
import functools

import jax
import jax.numpy as jnp
from jax import lax
from jax.experimental import pallas as pl
from jax.experimental.pallas import tpu as pltpu
from jax.experimental.pallas import tpu_sc as plsc

EPS = 1e-6
LANES = 128
SUBLANES = 8
CHUNK = 128
TOP_K = 2
SEQ_TILE = 512
CONV_ROWS = 32
OUT_K_GROUP = 256
CONV_HALO = 32
QK_HALO = 8
EXPERT_BLOCK = 512
SC_CHUNK = 128
COMBINE_TILE = 1024
N_PARTS = 2
PACK_COLS = 2 * LANES
VMEM_LIMIT = 56 * 1024 * 1024

_NT = (((1,), (1,)), ((), ()))
_TN = (((0,), (0,)), ((), ()))
_HI16 = 0xFFFF0000


def _dot(a, b):
    return jnp.dot(a, b, preferred_element_type=jnp.float32)


def _silu(x):
    return x * jax.nn.sigmoid(x)


def _log_sigmoid(x):
    return jnp.minimum(x, 0.0) - jnp.log1p(jnp.exp(-jnp.abs(x)))


def _split_bf16(x, parts):
    out = []
    for _ in range(parts):
        p = x.astype(jnp.bfloat16)
        out.append(p)
        x = x - p.astype(jnp.float32)
    return out


def _store_packed_rows(ref, v):
    n, d = v.shape
    half = d // 2
    nsub = half // LANES
    as_bits = lambda a: pltpu.bitcast(a.astype(jnp.bfloat16).astype(jnp.float32), jnp.uint32)
    word = (as_bits(v[:, :half]) >> 16) | (as_bits(v[:, half:]) & jnp.uint32(_HI16))
    for s in range(nsub):
        ref[pl.ds(s, n, stride=nsub), :] = word[:, s * LANES:(s + 1) * LANES]


def _load_packed_rows(ref, n, nsub, live=None):
    lo, hi = [], []
    for s in range(nsub):
        w = ref[pl.ds(s, n, stride=nsub), :]
        if live is not None:
            w = jnp.where(live, w, jnp.uint32(0))
        lo.append(pltpu.bitcast(w << 16, jnp.float32))
        hi.append(pltpu.bitcast(w & jnp.uint32(_HI16), jnp.float32))
    return lo + hi


def _mixer_kernel(b0_ref, x_ref, gmix_ref, wcv_ref, bcv_ref, wqk_ref, bqk_ref, wvo_ref, bvo_ref,
                  wif_ref, bif_ref, wift_ref, bift_ref, cw_ref, cb_ref, lng_ref, lnb_ref,
                  qkw_ref, qkb_ref, mng_ref, wout_ref, gffn_ref, wrh_ref, wrl_ref, br_ref,
                  su_ref, tril_ref,
                  x1_ref, h2_ref, ri_ref, gc_ref, cnt_ref,
                  uext, qkext, q_s, k_s, v_s, o_s, cn_s, m_s, y_s, cnt_s,
                  *, n_heads, conv_width, qk_width, n_groups, epg):
    ts, d = x_ref.shape
    dc = cw_ref.shape[1]
    dm = mng_ref.shape[1]
    dh = dm // n_heads
    n_chunks = ts // CHUNK
    n_experts = n_groups * epg
    b_idx = pl.program_id(0)
    t_idx = pl.program_id(1)

    @pl.when(t_idx == 0)
    def _reset_sequence_state():
        uext[0:CONV_HALO, :] = jnp.zeros((CONV_HALO, dc), jnp.float32)
        uext[CONV_HALO + ts:, :] = jnp.zeros((SUBLANES, dc), jnp.float32)
        qkext[0:QK_HALO, :] = jnp.zeros((QK_HALO, 2 * dm), jnp.float32)
        cn_s[...] = jnp.zeros(cn_s.shape, jnp.float32)
        m_s[...] = jnp.zeros(m_s.shape, jnp.float32)

    @pl.when((t_idx == 0) & (b_idx == 0))
    def _reset_counts():
        cnt_s[...] = jnp.zeros(cnt_s.shape, jnp.float32)

    x = x_ref[...]
    h = x * lax.rsqrt(jnp.mean(x * x, axis=-1, keepdims=True) + EPS) * gmix_ref[...]
    hb = h.astype(jnp.bfloat16)

    zc = _dot(hb, wcv_ref[...]) + bcv_ref[...]
    uext[CONV_HALO:CONV_HALO + ts, :] = zc[:, :dc] * jax.nn.sigmoid(zc[:, dc:])
    first = CONV_HALO - (conv_width - 1)
    sub = lax.broadcasted_iota(jnp.int32, (SUBLANES, dc), 0)
    n_grp = CONV_ROWS // SUBLANES
    for rb in range(ts // CONV_ROWS):
        r0 = rb * CONV_ROWS
        acc = jnp.zeros((CONV_ROWS, dc), jnp.float32) + cb_ref[...]
        for r in range(SUBLANES):
            part = None
            for j in range(conv_width):
                if (first + j) % SUBLANES != r:
                    continue
                a0 = r0 + (first + j) - r
                w8 = cw_ref[j * SUBLANES:(j + 1) * SUBLANES, :]
                term = jnp.concatenate([w8] * (n_grp + 1), axis=0) * uext[a0:a0 + CONV_ROWS + SUBLANES, :]
                part = term if part is None else part + term
            if part is None:
                continue
            if r == 0:
                acc = acc + part[:CONV_ROWS, :]
                continue
            rolled = [pltpu.roll(part[g * SUBLANES:(g + 1) * SUBLANES, :], SUBLANES - r, axis=0)
                      for g in range(n_grp + 1)]
            acc = acc + jnp.concatenate(
                [jnp.where(sub < SUBLANES - r, rolled[g], rolled[g + 1]) for g in range(n_grp)], axis=0)
        mu = jnp.mean(acc, axis=-1, keepdims=True)
        cen = acc - mu
        var = jnp.mean(cen * cen, axis=-1, keepdims=True)
        yn = cen * lax.rsqrt(var + EPS) * lng_ref[...] + lnb_ref[...]
        y_s[r0:r0 + CONV_ROWS, 0:dc] = _silu(yn).astype(jnp.bfloat16)
    uext[0:CONV_HALO, :] = uext[ts:ts + CONV_HALO, :]

    qkext[QK_HALO:QK_HALO + ts, :] = _dot(hb, wqk_ref[...]) + bqk_ref[...]
    qfirst = QK_HALO - (qk_width - 1)
    k_scale = dh ** -0.5
    for rb in range(ts // CONV_ROWS):
        r0 = rb * CONV_ROWS
        acc = jnp.zeros((CONV_ROWS, 2 * dm), jnp.float32) + qkb_ref[...]
        for j in range(qk_width):
            w8 = qkw_ref[j * SUBLANES:(j + 1) * SUBLANES, :]
            acc = acc + (jnp.concatenate([w8] * (CONV_ROWS // SUBLANES), axis=0)
                         * qkext[r0 + qfirst + j:r0 + qfirst + j + CONV_ROWS, :])
        act = _silu(acc)
        q_s[r0:r0 + CONV_ROWS, :] = act[:, :dm].astype(jnp.bfloat16)
        k_s[r0:r0 + CONV_ROWS, :] = (act[:, dm:] * k_scale).astype(jnp.bfloat16)
    qkext[0:QK_HALO, :] = qkext[ts:ts + QK_HALO, :]

    zvo = _dot(hb, wvo_ref[...]) + bvo_ref[...]
    v_s[...] = zvo[:, :dm].astype(jnp.bfloat16)
    o_s[...] = jax.nn.sigmoid(zvo[:, dm:])

    zif_col = _dot(hb, wif_ref[...]) + bif_ref[...]
    zif_row = lax.dot_general(wift_ref[...], hb, _NT,
                              preferred_element_type=jnp.float32) + bift_ref[...]
    logf_col = _log_sigmoid(zif_col)
    logf_row = _log_sigmoid(zif_row)
    lane = lax.broadcasted_iota(jnp.int32, (SUBLANES, CHUNK), 1)
    causal = (lax.broadcasted_iota(jnp.int32, (CHUNK, CHUNK), 0)
              >= lax.broadcasted_iota(jnp.int32, (CHUNK, CHUNK), 1))
    ones_blk = jnp.ones((CHUNK, dh), jnp.bfloat16)
    tril = tril_ref[...]

    for c in range(n_chunks):
        r0 = c * CHUNK
        b_row = logf_row[:, r0:r0 + CHUNK]
        shift = 1
        while shift < CHUNK:
            b_row = b_row + jnp.where(lane >= shift, pltpu.roll(b_row, shift, axis=1), 0.0)
            shift *= 2
        b_col = sum(_dot(tril, p) for p in _split_bf16(logf_col[r0:r0 + CHUNK, :], 3))
        i_col_all = zif_col[r0:r0 + CHUNK, :]
        i_row_all = zif_row[:, r0:r0 + CHUNK]
        for hd in range(n_heads):
            c0 = hd * dh
            b_c = b_col[:, n_heads + hd:n_heads + hd + 1]
            i_c = i_col_all[:, hd:hd + 1]
            b_r = b_row[n_heads + hd:n_heads + hd + 1, :]
            i_r = i_row_all[hd:hd + 1, :]
            m_prev = m_s[hd, 0:1, 0:1]
            q = q_s[r0:r0 + CHUNK, c0:c0 + dh]
            k = k_s[r0:r0 + CHUNK, c0:c0 + dh]
            vaug = jnp.concatenate([v_s[r0:r0 + CHUNK, c0:c0 + dh], ones_blk], axis=-1)
            cn = cn_s[hd]

            dmat = jnp.where(causal, b_c + (i_r - b_r), -jnp.inf)
            g = b_c + m_prev
            m_t = jnp.maximum(g, jnp.max(dmat, axis=-1, keepdims=True))
            w_intra = jnp.exp(dmat - m_t)
            w_inter = jnp.exp(g - m_t)
            s = lax.dot_general(q, k, _NT, preferred_element_type=jnp.float32) * w_intra
            nd = _dot(s.astype(jnp.bfloat16), vaug) + w_inter * _dot(q, cn.astype(jnp.bfloat16))
            hval = nd[:, :dh] / jnp.maximum(jnp.abs(nd[:, dh:]), jnp.exp(-m_t))

            b_last = b_r[:, CHUNK - 1:CHUNK]
            logw_c = b_last - b_c + i_c
            m_new = jnp.maximum(b_last + m_prev, jnp.max(logw_c, axis=0, keepdims=True))
            decay = jnp.exp(b_last + m_prev - m_new)
            kw = (k.astype(jnp.float32) * jnp.exp(logw_c - m_new)).astype(jnp.bfloat16)
            cn_s[hd] = decay * cn + lax.dot_general(kw, vaug, _TN,
                                                    preferred_element_type=jnp.float32)
            m_s[hd] = jnp.broadcast_to(m_new, m_s.shape[1:])

            hm = o_s[r0:r0 + CHUNK, c0:c0 + dh] * hval
            mu = jnp.mean(hm, axis=-1, keepdims=True)
            cen = hm - mu
            var = jnp.mean(cen * cen, axis=-1, keepdims=True)
            hn = cen * lax.rsqrt(var + EPS) * mng_ref[:, c0:c0 + dh]
            y_s[r0:r0 + CHUNK, dc + c0:dc + c0 + dh] = hn.astype(jnp.bfloat16)

    hg = dc + OUT_K_GROUP
    x1 = x_ref[...] + (_dot(y_s[:, 0:dc], wout_ref[0:dc, :])
                       + (_dot(y_s[:, dc:hg], wout_ref[dc:hg, :]) + _dot(y_s[:, hg:d], wout_ref[hg:d, :])))
    x1_ref[...] = x1
    h2 = x1 * lax.rsqrt(jnp.mean(x1 * x1, axis=-1, keepdims=True) + EPS) * gffn_ref[...]
    _store_packed_rows(h2_ref, h2)

    h2_hi, h2_lo = _split_bf16(h2, 2)
    lg = (lax.dot_general(wrh_ref[...], h2_hi, _NT, preferred_element_type=jnp.float32)
          + lax.dot_general(wrl_ref[...], h2_hi, _NT, preferred_element_type=jnp.float32)
          + lax.dot_general(wrh_ref[...], h2_lo, _NT, preferred_element_type=jnp.float32)
          + br_ref[...])
    gl = lg[0:n_groups, :]
    gidx = lax.broadcasted_iota(jnp.int32, (n_groups, ts), 0)
    gmax = jnp.max(gl, axis=0, keepdims=True)
    grp = jnp.min(jnp.where(gl == gmax, gidx, n_groups), axis=0, keepdims=True)
    p_grp = 1.0 / jnp.sum(jnp.exp(gl - gmax), axis=0, keepdims=True)
    in_group = jnp.zeros((epg, ts), jnp.float32)
    for gi in range(n_groups):
        in_group = jnp.where(grp == gi, lg[SUBLANES + gi * epg:SUBLANES + (gi + 1) * epg, :], in_group)
    eidx = lax.broadcasted_iota(jnp.int32, (epg, ts), 0)
    v1 = jnp.max(in_group, axis=0, keepdims=True)
    i1 = jnp.min(jnp.where(in_group == v1, eidx, epg), axis=0, keepdims=True)
    rest = jnp.where(eidx == i1, -jnp.inf, in_group)
    v2 = jnp.max(rest, axis=0, keepdims=True)
    i2 = jnp.min(jnp.where(rest == v2, eidx, epg), axis=0, keepdims=True)
    e21 = jnp.exp(v2 - v1)
    g1 = p_grp / (1.0 + e21)
    g2 = g1 * e21
    e1 = grp * epg + i1
    e2 = grp * epg + i2

    xidx = lax.broadcasted_iota(jnp.int32, (n_experts, ts), 0)
    sel1 = xidx == e1
    sel2 = xidx == e2
    onehot = jnp.where(sel1 | sel2, 1.0, 0.0).astype(jnp.bfloat16)
    before = cnt_s[:, 0:1] + _dot(onehot, su_ref[...])
    rank1 = jnp.sum(jnp.where(sel1, before, 0.0), axis=0, keepdims=True)
    rank2 = jnp.sum(jnp.where(sel2, before, 0.0), axis=0, keepdims=True)
    cnt_new = cnt_s[...] + jnp.sum(onehot.astype(jnp.float32), axis=1, keepdims=True)
    cnt_s[...] = cnt_new
    cnt_ref[...] = cnt_new

    zero_i = jnp.zeros((SUBLANES - 4, ts), jnp.int32)
    ri_ref[...] = jnp.concatenate(
        [e1, e2, rank1.astype(jnp.int32), rank2.astype(jnp.int32), zero_i], axis=0)
    gates_row = jnp.concatenate([g1, g2, jnp.zeros((LANES - 2, ts), jnp.float32)], axis=0)
    gc_ref[...] = gates_row.T


def _mixer_call(x, p, b0, bsz, *, n_heads, conv_width, qk_width, n_groups, epg):
    _, seq, d = x.shape
    ts = min(SEQ_TILE, seq)
    nt = seq // ts
    dc = p["cw"].shape[1]
    dm = p["mng"].shape[1]
    dh = dm // n_heads
    n_experts = n_groups * epg
    assert seq % ts == 0 and ts % CHUNK == 0 and dh == LANES and d % PACK_COLS == 0
    assert epg == SUBLANES and n_groups <= SUBLANES and conv_width - 1 <= CONV_HALO
    assert d - dc == 2 * OUT_K_GROUP

    consts = [p[n] for n in ("gmix", "wcv", "bcv", "wqk", "bqk", "wvo", "bvo", "wif", "bif",
                             "wift", "bift", "cw", "cb", "lng", "lnb", "qkw", "qkb", "mng",
                             "wout", "gffn", "wrh", "wrl", "br", "su", "tril")]

    def const_spec(a):
        return pl.BlockSpec(a.shape, lambda b, t, b0r: (0,) * a.ndim)

    kern = functools.partial(_mixer_kernel, n_heads=n_heads, conv_width=conv_width,
                             qk_width=qk_width, n_groups=n_groups, epg=epg)
    grid_spec = pltpu.PrefetchScalarGridSpec(
        num_scalar_prefetch=1,
        grid=(bsz, nt),
        in_specs=([pl.BlockSpec((None, ts, d), lambda b, t, b0r: (b + b0r[0], t, 0))]
                  + [const_spec(a) for a in consts]),
        out_specs=[
            pl.BlockSpec((None, ts, d), lambda b, t, b0r: (b, t, 0)),
            pl.BlockSpec((None, ts * (d // PACK_COLS), LANES), lambda b, t, b0r: (b, t, 0)),
            pl.BlockSpec((None, None, SUBLANES, ts), lambda b, t, b0r: (b, t, 0, 0)),
            pl.BlockSpec((None, ts, LANES), lambda b, t, b0r: (b, t, 0)),
            pl.BlockSpec((n_experts, LANES), lambda b, t, b0r: (0, 0)),
        ],
        scratch_shapes=[
            pltpu.VMEM((CONV_HALO + ts + SUBLANES, dc), jnp.float32),
            pltpu.VMEM((QK_HALO + ts, 2 * dm), jnp.float32),
            pltpu.VMEM((ts, dm), jnp.bfloat16),
            pltpu.VMEM((ts, dm), jnp.bfloat16),
            pltpu.VMEM((ts, dm), jnp.bfloat16),
            pltpu.VMEM((ts, dm), jnp.float32),
            pltpu.VMEM((n_heads, dh, 2 * dh), jnp.float32),
            pltpu.VMEM((n_heads, SUBLANES, LANES), jnp.float32),
            pltpu.VMEM((ts, d), jnp.bfloat16),
            pltpu.VMEM((n_experts, LANES), jnp.float32),
        ],
    )
    return pl.pallas_call(
        kern,
        grid_spec=grid_spec,
        out_shape=[
            jax.ShapeDtypeStruct((bsz, seq, d), jnp.float32),
            jax.ShapeDtypeStruct((bsz, seq * (d // PACK_COLS), LANES), jnp.uint32),
            jax.ShapeDtypeStruct((bsz, nt, SUBLANES, ts), jnp.int32),
            jax.ShapeDtypeStruct((bsz, seq, LANES), jnp.float32),
            jax.ShapeDtypeStruct((n_experts, LANES), jnp.float32),
        ],
        compiler_params=pltpu.CompilerParams(
            dimension_semantics=("arbitrary", "arbitrary"), vmem_limit_bytes=VMEM_LIMIT),
        name="mixer_router",
    )(b0, x, *consts)


def _sc_workers():
    info = plsc.get_sparse_core_info()
    return info.num_cores, info.num_cores * info.num_subcores


def _dest_kernel(pstart_ref, route_ref, dest_ref, *, n_experts):
    tiles, _, ts = route_ref.shape
    per = ts // SC_CHUNK
    for k in range(TOP_K):
        e = route_ref[:, k, :]
        rank = route_ref[:, TOP_K + k, :]
        start = jnp.zeros((tiles, ts), jnp.int32)
        for ex in range(n_experts):
            start = jnp.where(e == ex, pstart_ref[ex], start)
        slot = start + rank
        for j in range(per):
            dest_ref[k, pl.ds(j, tiles, stride=per), :] = slot[:, j * SC_CHUNK:(j + 1) * SC_CHUNK]


def _dest_call(padded_starts, route):
    tiles, _, ts = route.shape
    assert ts % SC_CHUNK == 0 and SC_CHUNK == LANES
    n_rows = tiles * ts // SC_CHUNK
    return pl.pallas_call(
        functools.partial(_dest_kernel, n_experts=padded_starts.shape[0]),
        grid_spec=pltpu.PrefetchScalarGridSpec(
            num_scalar_prefetch=1,
            grid=(1,),
            in_specs=[pl.BlockSpec(route.shape, lambda i, ps: (0, 0, 0))],
            out_specs=pl.BlockSpec((TOP_K, n_rows, SC_CHUNK), lambda i, ps: (0, 0, 0)),
        ),
        out_shape=jax.ShapeDtypeStruct((TOP_K, n_rows, SC_CHUNK), jnp.int32),
        compiler_params=pltpu.CompilerParams(dimension_semantics=("arbitrary",)),
        name="slot_ids",
    )(padded_starts, route)


def _sc_dispatch(dest, rows, n_slots):
    n_tok, nsub, l = rows.shape
    top_k, n_rows, ch = dest.shape
    n_cores, n_workers = _sc_workers()
    assert n_rows * ch == n_tok and n_rows % n_workers == 0
    n_chunks = n_rows // n_workers
    tpw = n_chunks * ch

    @functools.partial(
        pl.kernel,
        mesh=plsc.VectorSubcoreMesh(core_axis_name="c", subcore_axis_name="s"),
        out_type=jax.ShapeDtypeStruct((n_slots, nsub, l), rows.dtype),
        scratch_types=[pltpu.VMEM((top_k, n_chunks, ch), jnp.int32),
                       pltpu.VMEM((ch, nsub, l), rows.dtype),
                       pltpu.SemaphoreType.DMA],
        name="sc_dispatch",
    )
    def run(dest_hbm, rows_hbm, out_hbm, idx_v, buf, sem):
        wid = lax.axis_index("s") * n_cores + lax.axis_index("c")
        for k in range(top_k):
            pltpu.sync_copy(dest_hbm.at[k, pl.ds(wid * n_chunks, n_chunks)], idx_v.at[k])

        @pl.loop(0, n_chunks)
        def _(j):
            pltpu.sync_copy(rows_hbm.at[pl.ds(wid * tpw + j * ch, ch)], buf)
            for k in range(top_k):
                pltpu.async_copy(buf, out_hbm.at[idx_v.at[k, j]], sem).wait()

    return run(dest, rows)


def _sc_gather(dest, ys, n_tok):
    _, nsub, l = ys.shape
    top_k, n_rows, ch = dest.shape
    n_cores, n_workers = _sc_workers()
    assert n_rows * ch == n_tok and n_rows % n_workers == 0
    n_chunks = n_rows // n_workers
    tpw = n_chunks * ch

    @functools.partial(
        pl.kernel,
        mesh=plsc.VectorSubcoreMesh(core_axis_name="c", subcore_axis_name="s"),
        out_type=jax.ShapeDtypeStruct((top_k, n_tok, nsub, l), ys.dtype),
        scratch_types=[pltpu.VMEM((top_k, n_chunks, ch), jnp.int32),
                       pltpu.VMEM((ch, nsub, l), ys.dtype),
                       pltpu.SemaphoreType.DMA],
        name="sc_gather",
    )
    def run(dest_hbm, ys_hbm, out_hbm, idx_v, buf, sem):
        wid = lax.axis_index("s") * n_cores + lax.axis_index("c")
        for k in range(top_k):
            pltpu.sync_copy(dest_hbm.at[k, pl.ds(wid * n_chunks, n_chunks)], idx_v.at[k])

        @pl.loop(0, n_chunks)
        def _(j):
            for k in range(top_k):
                pltpu.async_copy(ys_hbm.at[idx_v.at[k, j]], buf, sem).wait()
                pltpu.sync_copy(buf, out_hbm.at[k, pl.ds(wid * tpw + j * ch, ch)])

    return run(dest, ys)


_T_EXPERT, _T_VALID, _T_NEXT, _T_RUN, _T_USED = range(5)


def _expert_kernel(ecol_ref, xs_ref, wg_hbm, wu_hbm, wd_hbm, ys_ref,
                   wg_f, wu_f, wd_f, wg_b, wu_b, wd_b, tbl_v, tbl_s, sems, tsem, *, nsub, n_blocks):
    i = pl.program_id(0)
    blk = xs_ref.shape[0] // nsub
    l = xs_ref.shape[1]
    n_experts = ecol_ref.shape[0]

    @pl.when(i == 0)
    def _build_block_table():
        width = tbl_v.shape[1]
        blk_end = ecol_ref[:, 0:1]
        slot_end = ecol_ref[:, 1:2]
        b = lax.broadcasted_iota(jnp.int32, (n_experts, width), 1).astype(jnp.float32)
        e_id = lax.broadcasted_iota(jnp.int32, (n_experts, width), 0).astype(jnp.float32)
        be = jnp.minimum(jnp.sum(jnp.where(b >= blk_end, 1.0, 0.0), axis=0, keepdims=True),
                         n_experts - 1.0)
        mine = e_id == be
        valid = jnp.clip(jnp.sum(jnp.where(mine, slot_end, 0.0), axis=0, keepdims=True)
                         - b[0:1, :] * blk, 0.0, float(blk))
        present = jnp.max(jnp.where(mine & (b < n_blocks), 1.0, 0.0), axis=1, keepdims=True) > 0.0
        run = jnp.sum(jnp.where((e_id < be) & present, 1.0, 0.0), axis=0, keepdims=True)
        nxt = jnp.min(jnp.where((e_id > be) & present, e_id, float(n_experts)), axis=0, keepdims=True)
        nxt = jnp.where(nxt == n_experts, -1.0, nxt)
        used = jnp.broadcast_to(jnp.max(blk_end, axis=0, keepdims=True), (1, width))
        rows = [be, valid, nxt, run, used] + [jnp.zeros((1, width), jnp.float32)] * (SUBLANES - 5)
        tbl_v[...] = jnp.concatenate(rows, axis=0).astype(jnp.int32)
        to_smem = pltpu.make_async_copy(tbl_v, tbl_s, tsem)
        to_smem.start()
        to_smem.wait()

    expert = tbl_s[_T_EXPERT, i]
    slot = tbl_s[_T_RUN, i] % 2

    def weight_copies(e, s):
        return [pltpu.make_async_copy(src.at[e], dst.at[s], sems.at[s])
                for src, dst in ((wg_hbm, wg_f), (wu_hbm, wu_f), (wd_hbm, wd_f))]

    @pl.when(i == 0)
    def _fetch_first_weights():
        for c in weight_copies(expert, 0):
            c.start()

    @pl.when((i == 0) | (expert != tbl_s[_T_EXPERT, jnp.maximum(i - 1, 0)]))
    def _new_expert_weights():
        for c in weight_copies(expert, slot):
            c.wait()
        wg_b[...] = wg_f[slot].astype(jnp.bfloat16)
        wu_b[...] = wu_f[slot].astype(jnp.bfloat16)
        wd_b[...] = wd_f[slot].astype(jnp.bfloat16)

        @pl.when(tbl_s[_T_NEXT, i] >= 0)
        def _fetch_next_run_weights():
            for c in weight_copies(tbl_s[_T_NEXT, i], 1 - slot):
                c.start()

    @pl.when(i < tbl_s[_T_USED, i])
    def _compute():
        live = lax.broadcasted_iota(jnp.int32, (blk, l), 0) < tbl_s[_T_VALID, i]
        xb = jnp.concatenate(
            [c.astype(jnp.bfloat16) for c in _load_packed_rows(xs_ref, blk, nsub, live)], axis=-1)
        a = _dot(xb, wg_b[...])
        u = _dot(xb, wu_b[...])
        _store_packed_rows(ys_ref, _dot((_silu(a) * u).astype(jnp.bfloat16), wd_b[...]))

    @pl.when(i >= tbl_s[_T_USED, i])
    def _unused_block():
        ys_ref[...] = jnp.zeros(ys_ref.shape, ys_ref.dtype)


def _expert_call(ecol, xs, wg, wu, wd, nsub):
    rows, l = xs.shape
    n_blocks = rows // (EXPERT_BLOCK * nsub)
    _, d, de = wg.shape
    width = -(-n_blocks // LANES) * LANES
    hbm = pl.BlockSpec(memory_space=pl.ANY)
    return pl.pallas_call(
        functools.partial(_expert_kernel, nsub=nsub, n_blocks=n_blocks),
        grid=(n_blocks,),
        in_specs=[pl.BlockSpec(ecol.shape, lambda i: (0, 0)),
                  pl.BlockSpec((EXPERT_BLOCK * nsub, l), lambda i: (i, 0)), hbm, hbm, hbm],
        out_specs=pl.BlockSpec((EXPERT_BLOCK * nsub, l), lambda i: (i, 0)),
        out_shape=jax.ShapeDtypeStruct((rows, l), xs.dtype),
        scratch_shapes=[pltpu.VMEM((2, d, de), wg.dtype), pltpu.VMEM((2, d, de), wu.dtype),
                        pltpu.VMEM((2, de, d), wd.dtype),
                        pltpu.VMEM((d, de), jnp.bfloat16), pltpu.VMEM((d, de), jnp.bfloat16),
                        pltpu.VMEM((de, d), jnp.bfloat16),
                        pltpu.VMEM((SUBLANES, width), jnp.int32),
                        pltpu.SMEM((SUBLANES, width), jnp.int32),
                        pltpu.SemaphoreType.DMA((2,)), pltpu.SemaphoreType.DMA(())],
        compiler_params=pltpu.CompilerParams(
            dimension_semantics=("arbitrary",), vmem_limit_bytes=VMEM_LIMIT),
        name="experts",
    )(ecol, xs, wg, wu, wd)


def _combine_kernel(x1_ref, gc_ref, gfin_ref, y_ref, out_ref, *, nsub):
    tc, d = x1_ref.shape
    g1 = gc_ref[:, 0:1]
    g2 = gc_ref[:, 1:2]
    pieces = []
    ssq = jnp.zeros((tc, 1), jnp.float32)
    y1_cols = _load_packed_rows(y_ref.at[0], tc, nsub)
    y2_cols = _load_packed_rows(y_ref.at[1], tc, nsub)
    for j, (y1, y2) in enumerate(zip(y1_cols, y2_cols)):
        z = x1_ref[:, j * LANES:(j + 1) * LANES] + (g1 * y1 + g2 * y2)
        pieces.append(z)
        ssq = ssq + jnp.sum(z * z, axis=-1, keepdims=True)
    scale = lax.rsqrt(ssq / d + EPS)
    for j, z in enumerate(pieces):
        out_ref[:, j * LANES:(j + 1) * LANES] = z * scale * gfin_ref[:, j * LANES:(j + 1) * LANES]


def _combine_call(x1, gates_col, gfin, y_tok, nsub, out_prev, tok0, n_tok_total):
    n_tok, d = x1.shape
    tc = min(COMBINE_TILE, n_tok)
    assert n_tok % tc == 0 and tok0 % tc == 0
    l = y_tok.shape[-1]
    in_specs = [
        pl.BlockSpec((tc, d), lambda i: (i, 0)),
        pl.BlockSpec((tc, LANES), lambda i: (i, 0)),
        pl.BlockSpec((1, d), lambda i: (0, 0)),
        pl.BlockSpec((TOP_K, tc * nsub, l), lambda i: (0, i, 0)),
    ]
    args = [x1, gates_col, gfin, y_tok]
    aliases = {}
    kern = functools.partial(_combine_kernel, nsub=nsub)
    if out_prev is not None:
        in_specs.append(pl.BlockSpec(memory_space=pl.ANY))
        args.append(out_prev)
        aliases = {len(args) - 1: 0}
        kern = lambda x1_ref, gc_ref, gfin_ref, y_ref, prev_ref, out_ref: _combine_kernel(
            x1_ref, gc_ref, gfin_ref, y_ref, out_ref, nsub=nsub)
    return pl.pallas_call(
        kern,
        grid=(n_tok // tc,),
        in_specs=in_specs,
        out_specs=pl.BlockSpec((tc, d), lambda i: (i + tok0 // tc, 0)),
        out_shape=jax.ShapeDtypeStruct((n_tok_total, d), jnp.float32),
        input_output_aliases=aliases,
        compiler_params=pltpu.CompilerParams(
            dimension_semantics=("arbitrary",), vmem_limit_bytes=VMEM_LIMIT),
        name="combine",
    )(*args)


def _prepare_layer(l, norm_mix_g, w_in, b_in, conv_w, conv_b, conv_ln_g, conv_ln_b, qk_conv_w,
                   qk_conv_b, mlstm_norm_g, w_out, norm_ffn_g, router_group_w, router_group_b,
                   router_expert_w, router_expert_b, ts):
    f32, bf16 = jnp.float32, jnp.bfloat16
    d = w_in.shape[1]
    dc = conv_w.shape[-1]
    dm = mlstm_norm_g.shape[-1]
    n_heads = (w_in.shape[-1] - 2 * dc - 4 * dm) // 2
    n_groups = router_group_w.shape[-1]
    n_experts = router_expert_w.shape[-1]
    wi, bi = w_in[l], b_in[l]
    o_qk, o_vo, o_if = 2 * dc, 2 * dc + 2 * dm, 2 * dc + 4 * dm
    w_if = wi[:, o_if:]
    b_if = bi[o_if:]
    wr_t = jnp.zeros((SUBLANES + n_experts, d), f32)
    wr_t = wr_t.at[:n_groups].set(router_group_w[l].T).at[SUBLANES:].set(router_expert_w[l].T)
    br = jnp.zeros((SUBLANES + n_experts, 1), f32)
    br = br.at[:n_groups, 0].set(router_group_b[l]).at[SUBLANES:, 0].set(router_expert_b[l])
    wr_hi = wr_t.astype(bf16)
    wr_lo = (wr_t - wr_hi.astype(f32)).astype(bf16)
    pad_rows = lambda a, n: jnp.pad(a, ((0, n - a.shape[0]), (0, 0)))
    idx = jnp.arange(ts)
    cidx = jnp.arange(CHUNK)
    return dict(
        gmix=norm_mix_g[l][None, :],
        wcv=wi[:, :o_qk].astype(bf16), bcv=bi[None, :o_qk],
        wqk=wi[:, o_qk:o_vo].astype(bf16), bqk=bi[None, o_qk:o_vo],
        wvo=wi[:, o_vo:o_if].astype(bf16), bvo=bi[None, o_vo:o_if],
        wif=jnp.pad(w_if, ((0, 0), (0, LANES - 2 * n_heads))).astype(bf16),
        bif=jnp.pad(b_if, (0, LANES - 2 * n_heads))[None, :],
        wift=pad_rows(w_if.T, SUBLANES).astype(bf16),
        bift=pad_rows(b_if[:, None], SUBLANES),
        cw=jnp.repeat(conv_w[l], SUBLANES, axis=0), cb=conv_b[l][None, :],
        lng=conv_ln_g[l][None, :], lnb=conv_ln_b[l][None, :],
        qkw=jnp.repeat(qk_conv_w[l], SUBLANES, axis=0), qkb=qk_conv_b[l][None, :],
        mng=mlstm_norm_g[l][None, :],
        wout=w_out[l].astype(bf16), gffn=norm_ffn_g[l][None, :],
        wrh=wr_hi, wrl=wr_lo, br=br,
        su=(idx[:, None] < idx[None, :]).astype(bf16),
        tril=(cidx[:, None] >= cidx[None, :]).astype(bf16),
    ), dict(n_heads=n_heads, conv_width=conv_w.shape[1], qk_width=qk_conv_w.shape[1],
            n_groups=n_groups, epg=n_experts // n_groups)


def kernel(x, norm_mix_g, w_in, b_in, conv_w, conv_b, conv_ln_g, conv_ln_b, qk_conv_w, qk_conv_b,
           mlstm_norm_g, w_out, norm_ffn_g, router_group_w, router_group_b, router_expert_w,
           router_expert_b, expert_w_gate, expert_w_up, expert_w_down, final_norm_g):
    bsz, seq, d = x.shape
    depth = w_in.shape[0]
    assert depth == 1, "the combine kernel fuses the final norm, so exactly one layer is supported"
    n_experts = router_expert_w.shape[-1]
    n_parts = N_PARTS if bsz % N_PARTS == 0 else 1
    pb = bsz // n_parts
    n_tok = pb * seq
    n_assign = n_tok * TOP_K
    n_blocks = -(-n_assign // EXPERT_BLOCK) + n_experts
    n_slots = n_blocks * EXPERT_BLOCK
    ts = min(SEQ_TILE, seq)
    nsub = d // PACK_COLS
    l = 0
    params, dims = _prepare_layer(
        l, norm_mix_g, w_in, b_in, conv_w, conv_b, conv_ln_g, conv_ln_b, qk_conv_w, qk_conv_b,
        mlstm_norm_g, w_out, norm_ffn_g, router_group_w, router_group_b, router_expert_w,
        router_expert_b, ts)
    wg, wu, wd = expert_w_gate[l], expert_w_up[l], expert_w_down[l]
    out = None
    prev_dest = None
    for part in range(n_parts):
        b0 = jnp.full((1,), part * pb, jnp.int32)
        if prev_dest is not None:
            b0 = b0 + jnp.minimum(prev_dest[0, 0, :1], 0)
        x1, h2_rows, route_i, gates_col, counts = _mixer_call(x, params, b0, pb, **dims)

        cnt = counts[:, 0].astype(jnp.int32)
        padded = (cnt + EXPERT_BLOCK - 1) // EXPERT_BLOCK * EXPERT_BLOCK
        padded_ends = jnp.cumsum(padded)
        padded_starts = padded_ends - padded
        ecol = jnp.pad(jnp.stack([padded_ends // EXPERT_BLOCK, padded_starts + cnt], axis=1),
                       ((0, 0), (0, LANES - 2))).astype(jnp.float32)

        dest = _dest_call(padded_starts, route_i.reshape(pb * (seq // ts), SUBLANES, ts))
        prev_dest = dest
        xs = _sc_dispatch(dest, h2_rows.reshape(n_tok, nsub, LANES), n_slots)
        ys = _expert_call(ecol, xs.reshape(n_slots * nsub, LANES), wg, wu, wd, nsub)
        y_tok = _sc_gather(dest, ys.reshape(n_slots, nsub, LANES), n_tok)
        out = _combine_call(x1.reshape(n_tok, d), gates_col.reshape(n_tok, LANES),
                            final_norm_g[None, :], y_tok.reshape(TOP_K, n_tok * nsub, LANES), nsub,
                            out, part * n_tok, bsz * seq)
    return out.reshape(bsz, seq, d)
```

```python
import functools

import jax
import jax.numpy as jnp
from jax import lax
from jax.experimental import pallas as pl
from jax.experimental.pallas import tpu as pltpu
from jax.experimental.pallas import tpu_sc as plsc

EPS = 1e-6
LANES = 128
SUBLANES = 8
CHUNK = 128
TOP_K = 2
SEQ_TILE = 512
CONV_ROWS = 32
OUT_K_GROUP = 256
CONV_HALO = 32
QK_HALO = 8
EXPERT_BLOCK = 512
SC_CHUNK = 128
SC_GATHER_SPLIT = 2
COMBINE_TILE = 1024
N_PARTS = 2
PACK_COLS = 2 * LANES
VMEM_LIMIT = 56 * 1024 * 1024

_NT = (((1,), (1,)), ((), ()))
_TN = (((0,), (0,)), ((), ()))
_HI16 = 0xFFFF0000


def _dot(a, b):
    return jnp.dot(a, b, preferred_element_type=jnp.float32)


def _silu(x):
    return x * jax.nn.sigmoid(x)


def _log_sigmoid(x):
    return jnp.minimum(x, 0.0) - jnp.log1p(jnp.exp(-jnp.abs(x)))


def _split_bf16(x, parts):
    out = []
    for _ in range(parts):
        p = x.astype(jnp.bfloat16)
        out.append(p)
        x = x - p.astype(jnp.float32)
    return out


def _store_packed_rows(ref, v):
    n, d = v.shape
    half = d // 2
    nsub = half // LANES
    as_bits = lambda a: pltpu.bitcast(a.astype(jnp.bfloat16).astype(jnp.float32), jnp.uint32)
    word = (as_bits(v[:, :half]) >> 16) | (as_bits(v[:, half:]) & jnp.uint32(_HI16))
    for s in range(nsub):
        ref[pl.ds(s, n, stride=nsub), :] = word[:, s * LANES:(s + 1) * LANES]


def _load_packed_rows(ref, n, nsub, live=None):
    lo, hi = [], []
    for s in range(nsub):
        w = ref[pl.ds(s, n, stride=nsub), :]
        if live is not None:
            w = jnp.where(live, w, jnp.uint32(0))
        lo.append(pltpu.bitcast(w << 16, jnp.float32))
        hi.append(pltpu.bitcast(w & jnp.uint32(_HI16), jnp.float32))
    return lo + hi


def _mixer_kernel(b0_ref, x_ref, gmix_ref, wcv_ref, bcv_ref, wqk_ref, bqk_ref, wvo_ref, bvo_ref,
                  wif_ref, bif_ref, wift_ref, bift_ref, cw_ref, cb_ref, lng_ref, lnb_ref,
                  qkw_ref, qkb_ref, mng_ref, wout_ref, gffn_ref, wrh_ref, wrl_ref, br_ref,
                  su_ref, tril_ref,
                  x1_ref, h2_ref, ri_ref, gc_ref, cnt_ref,
                  uext, qkext, q_s, k_s, v_s, o_s, cn_s, m_s, y_s, cnt_s,
                  *, n_heads, conv_width, qk_width, n_groups, epg):
    ts, d = x_ref.shape
    dc = cw_ref.shape[1]
    dm = mng_ref.shape[1]
    dh = dm // n_heads
    n_chunks = ts // CHUNK
    n_experts = n_groups * epg
    b_idx = pl.program_id(0)
    t_idx = pl.program_id(1)

    @pl.when(t_idx == 0)
    def _reset_sequence_state():
        uext[0:CONV_HALO, :] = jnp.zeros((CONV_HALO, dc), jnp.float32)
        uext[CONV_HALO + ts:, :] = jnp.zeros((SUBLANES, dc), jnp.float32)
        qkext[0:QK_HALO, :] = jnp.zeros((QK_HALO, 2 * dm), jnp.float32)
        cn_s[...] = jnp.zeros(cn_s.shape, jnp.float32)
        m_s[...] = jnp.zeros(m_s.shape, jnp.float32)

    @pl.when((t_idx == 0) & (b_idx == 0))
    def _reset_counts():
        cnt_s[...] = jnp.zeros(cnt_s.shape, jnp.float32)

    x = x_ref[...]
    h = x * lax.rsqrt(jnp.mean(x * x, axis=-1, keepdims=True) + EPS) * gmix_ref[...]
    hb = h.astype(jnp.bfloat16)

    zc = _dot(hb, wcv_ref[...]) + bcv_ref[...]
    uext[CONV_HALO:CONV_HALO + ts, :] = zc[:, :dc] * jax.nn.sigmoid(zc[:, dc:])
    first = CONV_HALO - (conv_width - 1)
    sub = lax.broadcasted_iota(jnp.int32, (SUBLANES, dc), 0)
    n_grp = CONV_ROWS // SUBLANES
    for rb in range(ts // CONV_ROWS):
        r0 = rb * CONV_ROWS
        acc = jnp.zeros((CONV_ROWS, dc), jnp.float32) + cb_ref[...]
        for r in range(SUBLANES):
            part = None
            for j in range(conv_width):
                if (first + j) % SUBLANES != r:
                    continue
                a0 = r0 + (first + j) - r
                w8 = cw_ref[j * SUBLANES:(j + 1) * SUBLANES, :]
                term = jnp.concatenate([w8] * (n_grp + 1), axis=0) * uext[a0:a0 + CONV_ROWS + SUBLANES, :]
                part = term if part is None else part + term
            if part is None:
                continue
            if r == 0:
                acc = acc + part[:CONV_ROWS, :]
                continue
            rolled = [pltpu.roll(part[g * SUBLANES:(g + 1) * SUBLANES, :], SUBLANES - r, axis=0)
                      for g in range(n_grp + 1)]
            acc = acc + jnp.concatenate(
                [jnp.where(sub < SUBLANES - r, rolled[g], rolled[g + 1]) for g in range(n_grp)], axis=0)
        mu = jnp.mean(acc, axis=-1, keepdims=True)
        cen = acc - mu
        var = jnp.mean(cen * cen, axis=-1, keepdims=True)
        yn = cen * lax.rsqrt(var + EPS) * lng_ref[...] + lnb_ref[...]
        y_s[r0:r0 + CONV_ROWS, 0:dc] = _silu(yn).astype(jnp.bfloat16)
    uext[0:CONV_HALO, :] = uext[ts:ts + CONV_HALO, :]

    qkext[QK_HALO:QK_HALO + ts, :] = _dot(hb, wqk_ref[...]) + bqk_ref[...]
    qfirst = QK_HALO - (qk_width - 1)
    k_scale = dh ** -0.5
    for rb in range(ts // CONV_ROWS):
        r0 = rb * CONV_ROWS
        acc = jnp.zeros((CONV_ROWS, 2 * dm), jnp.float32) + qkb_ref[...]
        for j in range(qk_width):
            w8 = qkw_ref[j * SUBLANES:(j + 1) * SUBLANES, :]
            acc = acc + (jnp.concatenate([w8] * (CONV_ROWS // SUBLANES), axis=0)
                         * qkext[r0 + qfirst + j:r0 + qfirst + j + CONV_ROWS, :])
        act = _silu(acc)
        q_s[r0:r0 + CONV_ROWS, :] = act[:, :dm].astype(jnp.bfloat16)
        k_s[r0:r0 + CONV_ROWS, :] = (act[:, dm:] * k_scale).astype(jnp.bfloat16)
    qkext[0:QK_HALO, :] = qkext[ts:ts + QK_HALO, :]

    zvo = _dot(hb, wvo_ref[...]) + bvo_ref[...]
    v_s[...] = zvo[:, :dm].astype(jnp.bfloat16)
    o_s[...] = jax.nn.sigmoid(zvo[:, dm:])

    zif_col = _dot(hb, wif_ref[...]) + bif_ref[...]
    zif_row = lax.dot_general(wift_ref[...], hb, _NT,
                              preferred_element_type=jnp.float32) + bift_ref[...]
    logf_col = _log_sigmoid(zif_col)
    logf_row = _log_sigmoid(zif_row)
    lane = lax.broadcasted_iota(jnp.int32, (SUBLANES, CHUNK), 1)
    causal = (lax.broadcasted_iota(jnp.int32, (CHUNK, CHUNK), 0)
              >= lax.broadcasted_iota(jnp.int32, (CHUNK, CHUNK), 1))
    ones_blk = jnp.ones((CHUNK, dh), jnp.bfloat16)
    tril = tril_ref[...]

    for c in range(n_chunks):
        r0 = c * CHUNK
        b_row = logf_row[:, r0:r0 + CHUNK]
        shift = 1
        while shift < CHUNK:
            b_row = b_row + jnp.where(lane >= shift, pltpu.roll(b_row, shift, axis=1), 0.0)
            shift *= 2
        b_col = sum(_dot(tril, p) for p in _split_bf16(logf_col[r0:r0 + CHUNK, :], 3))
        i_col_all = zif_col[r0:r0 + CHUNK, :]
        i_row_all = zif_row[:, r0:r0 + CHUNK]
        for hd in range(n_heads):
            c0 = hd * dh
            b_c = b_col[:, n_heads + hd:n_heads + hd + 1]
            i_c = i_col_all[:, hd:hd + 1]
            b_r = b_row[n_heads + hd:n_heads + hd + 1, :]
            i_r = i_row_all[hd:hd + 1, :]
            m_prev = m_s[hd, 0:1, 0:1]
            q = q_s[r0:r0 + CHUNK, c0:c0 + dh]
            k = k_s[r0:r0 + CHUNK, c0:c0 + dh]
            vaug = jnp.concatenate([v_s[r0:r0 + CHUNK, c0:c0 + dh], ones_blk], axis=-1)
            cn = cn_s[hd]

            dmat = jnp.where(causal, b_c + (i_r - b_r), -jnp.inf)
            g = b_c + m_prev
            m_t = jnp.maximum(g, jnp.max(dmat, axis=-1, keepdims=True))
            w_intra = jnp.exp(dmat - m_t)
            w_inter = jnp.exp(g - m_t)
            s = lax.dot_general(q, k, _NT, preferred_element_type=jnp.float32) * w_intra
            nd = _dot(s.astype(jnp.bfloat16), vaug) + w_inter * _dot(q, cn.astype(jnp.bfloat16))
            hval = nd[:, :dh] / jnp.maximum(jnp.abs(nd[:, dh:]), jnp.exp(-m_t))

            b_last = b_r[:, CHUNK - 1:CHUNK]
            logw_c = b_last - b_c + i_c
            m_new = jnp.maximum(b_last + m_prev, jnp.max(logw_c, axis=0, keepdims=True))
            decay = jnp.exp(b_last + m_prev - m_new)
            kw = (k.astype(jnp.float32) * jnp.exp(logw_c - m_new)).astype(jnp.bfloat16)
            cn_s[hd] = decay * cn + lax.dot_general(kw, vaug, _TN,
                                                    preferred_element_type=jnp.float32)
            m_s[hd] = jnp.broadcast_to(m_new, m_s.shape[1:])

            hm = o_s[r0:r0 + CHUNK, c0:c0 + dh] * hval
            mu = jnp.mean(hm, axis=-1, keepdims=True)
            cen = hm - mu
            var = jnp.mean(cen * cen, axis=-1, keepdims=True)
            hn = cen * lax.rsqrt(var + EPS) * mng_ref[:, c0:c0 + dh]
            y_s[r0:r0 + CHUNK, dc + c0:dc + c0 + dh] = hn.astype(jnp.bfloat16)

    hg = dc + OUT_K_GROUP
    x1 = x_ref[...] + (_dot(y_s[:, 0:dc], wout_ref[0:dc, :])
                       + (_dot(y_s[:, dc:hg], wout_ref[dc:hg, :]) + _dot(y_s[:, hg:d], wout_ref[hg:d, :])))
    x1_ref[...] = x1
    h2 = x1 * lax.rsqrt(jnp.mean(x1 * x1, axis=-1, keepdims=True) + EPS) * gffn_ref[...]
    _store_packed_rows(h2_ref, h2)

    h2_hi, h2_lo = _split_bf16(h2, 2)
    lg = (lax.dot_general(wrh_ref[...], h2_hi, _NT, preferred_element_type=jnp.float32)
          + lax.dot_general(wrl_ref[...], h2_hi, _NT, preferred_element_type=jnp.float32)
          + lax.dot_general(wrh_ref[...], h2_lo, _NT, preferred_element_type=jnp.float32)
          + br_ref[...])
    gl = lg[0:n_groups, :]
    gidx = lax.broadcasted_iota(jnp.int32, (n_groups, ts), 0)
    gmax = jnp.max(gl, axis=0, keepdims=True)
    grp = jnp.min(jnp.where(gl == gmax, gidx, n_groups), axis=0, keepdims=True)
    p_grp = 1.0 / jnp.sum(jnp.exp(gl - gmax), axis=0, keepdims=True)
    in_group = jnp.zeros((epg, ts), jnp.float32)
    for gi in range(n_groups):
        in_group = jnp.where(grp == gi, lg[SUBLANES + gi * epg:SUBLANES + (gi + 1) * epg, :], in_group)
    eidx = lax.broadcasted_iota(jnp.int32, (epg, ts), 0)
    v1 = jnp.max(in_group, axis=0, keepdims=True)
    i1 = jnp.min(jnp.where(in_group == v1, eidx, epg), axis=0, keepdims=True)
    rest = jnp.where(eidx == i1, -jnp.inf, in_group)
    v2 = jnp.max(rest, axis=0, keepdims=True)
    i2 = jnp.min(jnp.where(rest == v2, eidx, epg), axis=0, keepdims=True)
    e21 = jnp.exp(v2 - v1)
    g1 = p_grp / (1.0 + e21)
    g2 = g1 * e21
    e1 = grp * epg + i1
    e2 = grp * epg + i2

    xidx = lax.broadcasted_iota(jnp.int32, (n_experts, ts), 0)
    sel1 = xidx == e1
    sel2 = xidx == e2
    onehot = jnp.where(sel1 | sel2, 1.0, 0.0).astype(jnp.bfloat16)
    before = cnt_s[:, 0:1] + _dot(onehot, su_ref[...])
    rank1 = jnp.sum(jnp.where(sel1, before, 0.0), axis=0, keepdims=True)
    rank2 = jnp.sum(jnp.where(sel2, before, 0.0), axis=0, keepdims=True)
    cnt_new = cnt_s[...] + jnp.sum(onehot.astype(jnp.float32), axis=1, keepdims=True)
    cnt_s[...] = cnt_new
    cnt_ref[...] = cnt_new

    zero_i = jnp.zeros((SUBLANES - 4, ts), jnp.int32)
    ri_ref[...] = jnp.concatenate(
        [e1, e2, rank1.astype(jnp.int32), rank2.astype(jnp.int32), zero_i], axis=0)
    gates_row = jnp.concatenate([g1, g2, jnp.zeros((LANES - 2, ts), jnp.float32)], axis=0)
    gc_ref[...] = gates_row.T


def _mixer_call(x, p, b0, bsz, *, n_heads, conv_width, qk_width, n_groups, epg):
    _, seq, d = x.shape
    ts = min(SEQ_TILE, seq)
    nt = seq // ts
    dc = p["cw"].shape[1]
    dm = p["mng"].shape[1]
    dh = dm // n_heads
    n_experts = n_groups * epg
    assert seq % ts == 0 and ts % CHUNK == 0 and dh == LANES and d % PACK_COLS == 0
    assert epg == SUBLANES and n_groups <= SUBLANES and conv_width - 1 <= CONV_HALO
    assert d - dc == 2 * OUT_K_GROUP

    consts = [p[n] for n in ("gmix", "wcv", "bcv", "wqk", "bqk", "wvo", "bvo", "wif", "bif",
                             "wift", "bift", "cw", "cb", "lng", "lnb", "qkw", "qkb", "mng",
                             "wout", "gffn", "wrh", "wrl", "br", "su", "tril")]

    def const_spec(a):
        return pl.BlockSpec(a.shape, lambda b, t, b0r: (0,) * a.ndim)

    kern = functools.partial(_mixer_kernel, n_heads=n_heads, conv_width=conv_width,
                             qk_width=qk_width, n_groups=n_groups, epg=epg)
    grid_spec = pltpu.PrefetchScalarGridSpec(
        num_scalar_prefetch=1,
        grid=(bsz, nt),
        in_specs=([pl.BlockSpec((None, ts, d), lambda b, t, b0r: (b + b0r[0], t, 0))]
                  + [const_spec(a) for a in consts]),
        out_specs=[
            pl.BlockSpec((None, ts, d), lambda b, t, b0r: (b, t, 0)),
            pl.BlockSpec((None, ts * (d // PACK_COLS), LANES), lambda b, t, b0r: (b, t, 0)),
            pl.BlockSpec((None, None, SUBLANES, ts), lambda b, t, b0r: (b, t, 0, 0)),
            pl.BlockSpec((None, ts, LANES), lambda b, t, b0r: (b, t, 0)),
            pl.BlockSpec((n_experts, LANES), lambda b, t, b0r: (0, 0)),
        ],
        scratch_shapes=[
            pltpu.VMEM((CONV_HALO + ts + SUBLANES, dc), jnp.float32),
            pltpu.VMEM((QK_HALO + ts, 2 * dm), jnp.float32),
            pltpu.VMEM((ts, dm), jnp.bfloat16),
            pltpu.VMEM((ts, dm), jnp.bfloat16),
            pltpu.VMEM((ts, dm), jnp.bfloat16),
            pltpu.VMEM((ts, dm), jnp.float32),
            pltpu.VMEM((n_heads, dh, 2 * dh), jnp.float32),
            pltpu.VMEM((n_heads, SUBLANES, LANES), jnp.float32),
            pltpu.VMEM((ts, d), jnp.bfloat16),
            pltpu.VMEM((n_experts, LANES), jnp.float32),
        ],
    )
    return pl.pallas_call(
        kern,
        grid_spec=grid_spec,
        out_shape=[
            jax.ShapeDtypeStruct((bsz, seq, d), jnp.float32),
            jax.ShapeDtypeStruct((bsz, seq * (d // PACK_COLS), LANES), jnp.uint32),
            jax.ShapeDtypeStruct((bsz, nt, SUBLANES, ts), jnp.int32),
            jax.ShapeDtypeStruct((bsz, seq, LANES), jnp.float32),
            jax.ShapeDtypeStruct((n_experts, LANES), jnp.float32),
        ],
        compiler_params=pltpu.CompilerParams(
            dimension_semantics=("arbitrary", "arbitrary"), vmem_limit_bytes=VMEM_LIMIT),
        name="mixer_router",
    )(b0, x, *consts)


def _sc_workers():
    info = plsc.get_sparse_core_info()
    return info.num_cores, info.num_cores * info.num_subcores


def _dest_kernel(pstart_ref, route_ref, dest_ref, *, n_experts):
    tiles, _, ts = route_ref.shape
    per = ts // SC_CHUNK
    for k in range(TOP_K):
        e = route_ref[:, k, :]
        rank = route_ref[:, TOP_K + k, :]
        start = jnp.zeros((tiles, ts), jnp.int32)
        for ex in range(n_experts):
            start = jnp.where(e == ex, pstart_ref[ex], start)
        slot = start + rank
        for j in range(per):
            dest_ref[k, pl.ds(j, tiles, stride=per), :] = slot[:, j * SC_CHUNK:(j + 1) * SC_CHUNK]


def _dest_call(padded_starts, route):
    tiles, _, ts = route.shape
    assert ts % SC_CHUNK == 0 and SC_CHUNK == LANES
    n_rows = tiles * ts // SC_CHUNK
    return pl.pallas_call(
        functools.partial(_dest_kernel, n_experts=padded_starts.shape[0]),
        grid_spec=pltpu.PrefetchScalarGridSpec(
            num_scalar_prefetch=1,
            grid=(1,),
            in_specs=[pl.BlockSpec(route.shape, lambda i, ps: (0, 0, 0))],
            out_specs=pl.BlockSpec((TOP_K, n_rows, SC_CHUNK), lambda i, ps: (0, 0, 0)),
        ),
        out_shape=jax.ShapeDtypeStruct((TOP_K, n_rows, SC_CHUNK), jnp.int32),
        compiler_params=pltpu.CompilerParams(dimension_semantics=("arbitrary",)),
        name="slot_ids",
    )(padded_starts, route)


def _sc_dispatch(dest, rows, n_slots):
    n_tok, nsub, l = rows.shape
    top_k, n_rows, ch = dest.shape
    n_cores, n_workers = _sc_workers()
    assert n_rows * ch == n_tok and n_rows % n_workers == 0
    n_chunks = n_rows // n_workers
    tpw = n_chunks * ch

    @functools.partial(
        pl.kernel,
        mesh=plsc.VectorSubcoreMesh(core_axis_name="c", subcore_axis_name="s"),
        out_type=jax.ShapeDtypeStruct((n_slots, nsub, l), rows.dtype),
        scratch_types=[pltpu.VMEM((top_k, n_chunks, ch), jnp.int32),
                       pltpu.VMEM((ch, nsub, l), rows.dtype),
                       pltpu.SemaphoreType.DMA],
        name="sc_dispatch",
    )
    def run(dest_hbm, rows_hbm, out_hbm, idx_v, buf, sem):
        wid = lax.axis_index("s") * n_cores + lax.axis_index("c")
        for k in range(top_k):
            pltpu.sync_copy(dest_hbm.at[k, pl.ds(wid * n_chunks, n_chunks)], idx_v.at[k])

        @pl.loop(0, n_chunks)
        def _(j):
            pltpu.sync_copy(rows_hbm.at[pl.ds(wid * tpw + j * ch, ch)], buf)
            for k in range(top_k):
                pltpu.async_copy(buf, out_hbm.at[idx_v.at[k, j]], sem).wait()

    return run(dest, rows)


def _sc_gather(dest, ys, n_tok):
    _, nsub, l = ys.shape
    top_k, n_rows, ch = dest.shape
    n_cores, n_workers = _sc_workers()
    assert n_rows * ch == n_tok and n_rows % n_workers == 0
    n_chunks = n_rows // n_workers
    tpw = n_chunks * ch

    assert ch % SC_GATHER_SPLIT == 0
    hc = ch // SC_GATHER_SPLIT
    items = [(k, h) for k in range(top_k) for h in range(SC_GATHER_SPLIT)]

    @functools.partial(
        pl.kernel,
        mesh=plsc.VectorSubcoreMesh(core_axis_name="c", subcore_axis_name="s"),
        out_type=jax.ShapeDtypeStruct((top_k, n_tok, nsub, l), ys.dtype),
        scratch_types=[pltpu.VMEM((top_k, n_chunks, ch), jnp.int32),
                       pltpu.VMEM((2, hc, nsub, l), ys.dtype),
                       pltpu.SemaphoreType.DMA((2,)), pltpu.SemaphoreType.DMA((2,))],
        name="sc_gather",
    )
    def run(dest_hbm, ys_hbm, out_hbm, idx_v, bufs, gsem, wsem):
        wid = lax.axis_index("s") * n_cores + lax.axis_index("c")
        for k in range(top_k):
            pltpu.sync_copy(dest_hbm.at[k, pl.ds(wid * n_chunks, n_chunks)], idx_v.at[k])

        @pl.loop(0, n_chunks)
        def _(j):
            def gather(i):
                k, h = items[i]
                return pltpu.make_async_copy(ys_hbm.at[idx_v.at[k, j, pl.ds(h * hc, hc)]],
                                             bufs.at[i % 2], gsem.at[i % 2])

            def write(i):
                k, h = items[i]
                return pltpu.make_async_copy(
                    bufs.at[i % 2], out_hbm.at[k, pl.ds(wid * tpw + j * ch + h * hc, hc)], wsem.at[i % 2])

            gather(0).start()
            for i in range(len(items)):
                if i + 1 < len(items):
                    if i >= 1:
                        write(i - 1).wait()
                    gather(i + 1).start()
                gather(i).wait()
                write(i).start()
            write(len(items) - 2).wait()
            write(len(items) - 1).wait()

    return run(dest, ys)


_T_EXPERT, _T_VALID, _T_NEXT, _T_RUN, _T_USED = range(5)


def _expert_kernel(ecol_ref, xs_ref, wg_hbm, wu_hbm, wd_hbm, ys_ref,
                   wg_f, wu_f, wd_f, wg_b, wu_b, wd_b, tbl_v, tbl_s, sems, tsem, *, nsub, n_blocks):
    i = pl.program_id(0)
    blk = xs_ref.shape[0] // nsub
    l = xs_ref.shape[1]
    n_experts = ecol_ref.shape[0]

    @pl.when(i == 0)
    def _build_block_table():
        width = tbl_v.shape[1]
        blk_end = ecol_ref[:, 0:1]
        slot_end = ecol_ref[:, 1:2]
        b = lax.broadcasted_iota(jnp.int32, (n_experts, width), 1).astype(jnp.float32)
        e_id = lax.broadcasted_iota(jnp.int32, (n_experts, width), 0).astype(jnp.float32)
        be = jnp.minimum(jnp.sum(jnp.where(b >= blk_end, 1.0, 0.0), axis=0, keepdims=True),
                         n_experts - 1.0)
        mine = e_id == be
        valid = jnp.clip(jnp.sum(jnp.where(mine, slot_end, 0.0), axis=0, keepdims=True)
                         - b[0:1, :] * blk, 0.0, float(blk))
        present = jnp.max(jnp.where(mine & (b < n_blocks), 1.0, 0.0), axis=1, keepdims=True) > 0.0
        run = jnp.sum(jnp.where((e_id < be) & present, 1.0, 0.0), axis=0, keepdims=True)
        nxt = jnp.min(jnp.where((e_id > be) & present, e_id, float(n_experts)), axis=0, keepdims=True)
        nxt = jnp.where(nxt == n_experts, -1.0, nxt)
        used = jnp.broadcast_to(jnp.max(blk_end, axis=0, keepdims=True), (1, width))
        rows = [be, valid, nxt, run, used] + [jnp.zeros((1, width), jnp.float32)] * (SUBLANES - 5)
        tbl_v[...] = jnp.concatenate(rows, axis=0).astype(jnp.int32)
        to_smem = pltpu.make_async_copy(tbl_v, tbl_s, tsem)
        to_smem.start()
        to_smem.wait()

    expert = tbl_s[_T_EXPERT, i]
    slot = tbl_s[_T_RUN, i] % 2

    def weight_copies(e, s):
        return [pltpu.make_async_copy(src.at[e], dst.at[s], sems.at[s])
                for src, dst in ((wg_hbm, wg_f), (wu_hbm, wu_f), (wd_hbm, wd_f))]

    @pl.when(i == 0)
    def _fetch_first_weights():
        for c in weight_copies(expert, 0):
            c.start()

    @pl.when((i == 0) | (expert != tbl_s[_T_EXPERT, jnp.maximum(i - 1, 0)]))
    def _new_expert_weights():
        for c in weight_copies(expert, slot):
            c.wait()
        wg_b[...] = wg_f[slot].astype(jnp.bfloat16)
        wu_b[...] = wu_f[slot].astype(jnp.bfloat16)
        wd_b[...] = wd_f[slot].astype(jnp.bfloat16)

        @pl.when(tbl_s[_T_NEXT, i] >= 0)
        def _fetch_next_run_weights():
            for c in weight_copies(tbl_s[_T_NEXT, i], 1 - slot):
                c.start()

    @pl.when(i < tbl_s[_T_USED, i])
    def _compute():
        live = lax.broadcasted_iota(jnp.int32, (blk, l), 0) < tbl_s[_T_VALID, i]
        xb = jnp.concatenate(
            [c.astype(jnp.bfloat16) for c in _load_packed_rows(xs_ref, blk, nsub, live)], axis=-1)
        a = _dot(xb, wg_b[...])
        u = _dot(xb, wu_b[...])
        _store_packed_rows(ys_ref, _dot((_silu(a) * u).astype(jnp.bfloat16), wd_b[...]))

    @pl.when(i >= tbl_s[_T_USED, i])
    def _unused_block():
        ys_ref[...] = jnp.zeros(ys_ref.shape, ys_ref.dtype)


def _expert_call(ecol, xs, wg, wu, wd, nsub):
    rows, l = xs.shape
    n_blocks = rows // (EXPERT_BLOCK * nsub)
    _, d, de = wg.shape
    width = -(-n_blocks // LANES) * LANES
    hbm = pl.BlockSpec(memory_space=pl.ANY)
    return pl.pallas_call(
        functools.partial(_expert_kernel, nsub=nsub, n_blocks=n_blocks),
        grid=(n_blocks,),
        in_specs=[pl.BlockSpec(ecol.shape, lambda i: (0, 0)),
                  pl.BlockSpec((EXPERT_BLOCK * nsub, l), lambda i: (i, 0)), hbm, hbm, hbm],
        out_specs=pl.BlockSpec((EXPERT_BLOCK * nsub, l), lambda i: (i, 0)),
        out_shape=jax.ShapeDtypeStruct((rows, l), xs.dtype),
        scratch_shapes=[pltpu.VMEM((2, d, de), wg.dtype), pltpu.VMEM((2, d, de), wu.dtype),
                        pltpu.VMEM((2, de, d), wd.dtype),
                        pltpu.VMEM((d, de), jnp.bfloat16), pltpu.VMEM((d, de), jnp.bfloat16),
                        pltpu.VMEM((de, d), jnp.bfloat16),
                        pltpu.VMEM((SUBLANES, width), jnp.int32),
                        pltpu.SMEM((SUBLANES, width), jnp.int32),
                        pltpu.SemaphoreType.DMA((2,)), pltpu.SemaphoreType.DMA(())],
        compiler_params=pltpu.CompilerParams(
            dimension_semantics=("arbitrary",), vmem_limit_bytes=VMEM_LIMIT),
        name="experts",
    )(ecol, xs, wg, wu, wd)


def _combine_kernel(x1_ref, gc_ref, gfin_ref, y_ref, out_ref, *, nsub):
    tc, d = x1_ref.shape
    g1 = gc_ref[:, 0:1]
    g2 = gc_ref[:, 1:2]
    pieces = []
    ssq = jnp.zeros((tc, 1), jnp.float32)
    y1_cols = _load_packed_rows(y_ref.at[0], tc, nsub)
    y2_cols = _load_packed_rows(y_ref.at[1], tc, nsub)
    for j, (y1, y2) in enumerate(zip(y1_cols, y2_cols)):
        z = x1_ref[:, j * LANES:(j + 1) * LANES] + (g1 * y1 + g2 * y2)
        pieces.append(z)
        ssq = ssq + jnp.sum(z * z, axis=-1, keepdims=True)
    scale = lax.rsqrt(ssq / d + EPS)
    for j, z in enumerate(pieces):
        out_ref[:, j * LANES:(j + 1) * LANES] = z * scale * gfin_ref[:, j * LANES:(j + 1) * LANES]


def _combine_call(x1, gates_col, gfin, y_tok, nsub, out_prev, tok0, n_tok_total):
    n_tok, d = x1.shape
    tc = min(COMBINE_TILE, n_tok)
    assert n_tok % tc == 0 and tok0 % tc == 0
    l = y_tok.shape[-1]
    in_specs = [
        pl.BlockSpec((tc, d), lambda i: (i, 0)),
        pl.BlockSpec((tc, LANES), lambda i: (i, 0)),
        pl.BlockSpec((1, d), lambda i: (0, 0)),
        pl.BlockSpec((TOP_K, tc * nsub, l), lambda i: (0, i, 0)),
    ]
    args = [x1, gates_col, gfin, y_tok]
    aliases = {}
    kern = functools.partial(_combine_kernel, nsub=nsub)
    if out_prev is not None:
        in_specs.append(pl.BlockSpec(memory_space=pl.ANY))
        args.append(out_prev)
        aliases = {len(args) - 1: 0}
        kern = lambda x1_ref, gc_ref, gfin_ref, y_ref, prev_ref, out_ref: _combine_kernel(
            x1_ref, gc_ref, gfin_ref, y_ref, out_ref, nsub=nsub)
    return pl.pallas_call(
        kern,
        grid=(n_tok // tc,),
        in_specs=in_specs,
        out_specs=pl.BlockSpec((tc, d), lambda i: (i + tok0 // tc, 0)),
        out_shape=jax.ShapeDtypeStruct((n_tok_total, d), jnp.float32),
        input_output_aliases=aliases,
        compiler_params=pltpu.CompilerParams(
            dimension_semantics=("arbitrary",), vmem_limit_bytes=VMEM_LIMIT),
        name="combine",
    )(*args)


def _prepare_layer(l, norm_mix_g, w_in, b_in, conv_w, conv_b, conv_ln_g, conv_ln_b, qk_conv_w,
                   qk_conv_b, mlstm_norm_g, w_out, norm_ffn_g, router_group_w, router_group_b,
                   router_expert_w, router_expert_b, ts):
    f32, bf16 = jnp.float32, jnp.bfloat16
    d = w_in.shape[1]
    dc = conv_w.shape[-1]
    dm = mlstm_norm_g.shape[-1]
    n_heads = (w_in.shape[-1] - 2 * dc - 4 * dm) // 2
    n_groups = router_group_w.shape[-1]
    n_experts = router_expert_w.shape[-1]
    wi, bi = w_in[l], b_in[l]
    o_qk, o_vo, o_if = 2 * dc, 2 * dc + 2 * dm, 2 * dc + 4 * dm
    w_if = wi[:, o_if:]
    b_if = bi[o_if:]
    wr_t = jnp.zeros((SUBLANES + n_experts, d), f32)
    wr_t = wr_t.at[:n_groups].set(router_group_w[l].T).at[SUBLANES:].set(router_expert_w[l].T)
    br = jnp.zeros((SUBLANES + n_experts, 1), f32)
    br = br.at[:n_groups, 0].set(router_group_b[l]).at[SUBLANES:, 0].set(router_expert_b[l])
    wr_hi = wr_t.astype(bf16)
    wr_lo = (wr_t - wr_hi.astype(f32)).astype(bf16)
    pad_rows = lambda a, n: jnp.pad(a, ((0, n - a.shape[0]), (0, 0)))
    idx = jnp.arange(ts)
    cidx = jnp.arange(CHUNK)
    return dict(
        gmix=norm_mix_g[l][None, :],
        wcv=wi[:, :o_qk].astype(bf16), bcv=bi[None, :o_qk],
        wqk=wi[:, o_qk:o_vo].astype(bf16), bqk=bi[None, o_qk:o_vo],
        wvo=wi[:, o_vo:o_if].astype(bf16), bvo=bi[None, o_vo:o_if],
        wif=jnp.pad(w_if, ((0, 0), (0, LANES - 2 * n_heads))).astype(bf16),
        bif=jnp.pad(b_if, (0, LANES - 2 * n_heads))[None, :],
        wift=pad_rows(w_if.T, SUBLANES).astype(bf16),
        bift=pad_rows(b_if[:, None], SUBLANES),
        cw=jnp.repeat(conv_w[l], SUBLANES, axis=0), cb=conv_b[l][None, :],
        lng=conv_ln_g[l][None, :], lnb=conv_ln_b[l][None, :],
        qkw=jnp.repeat(qk_conv_w[l], SUBLANES, axis=0), qkb=qk_conv_b[l][None, :],
        mng=mlstm_norm_g[l][None, :],
        wout=w_out[l].astype(bf16), gffn=norm_ffn_g[l][None, :],
        wrh=wr_hi, wrl=wr_lo, br=br,
        su=(idx[:, None] < idx[None, :]).astype(bf16),
        tril=(cidx[:, None] >= cidx[None, :]).astype(bf16),
    ), dict(n_heads=n_heads, conv_width=conv_w.shape[1], qk_width=qk_conv_w.shape[1],
            n_groups=n_groups, epg=n_experts // n_groups)


def kernel(x, norm_mix_g, w_in, b_in, conv_w, conv_b, conv_ln_g, conv_ln_b, qk_conv_w, qk_conv_b,
           mlstm_norm_g, w_out, norm_ffn_g, router_group_w, router_group_b, router_expert_w,
           router_expert_b, expert_w_gate, expert_w_up, expert_w_down, final_norm_g):
    bsz, seq, d = x.shape
    depth = w_in.shape[0]
    assert depth == 1, "the combine kernel fuses the final norm, so exactly one layer is supported"
    n_experts = router_expert_w.shape[-1]
    n_parts = N_PARTS if bsz % N_PARTS == 0 else 1
    pb = bsz // n_parts
    n_tok = pb * seq
    n_assign = n_tok * TOP_K
    n_blocks = -(-n_assign // EXPERT_BLOCK) + n_experts
    n_slots = n_blocks * EXPERT_BLOCK
    ts = min(SEQ_TILE, seq)
    nsub = d // PACK_COLS
    l = 0
    params, dims = _prepare_layer(
        l, norm_mix_g, w_in, b_in, conv_w, conv_b, conv_ln_g, conv_ln_b, qk_conv_w, qk_conv_b,
        mlstm_norm_g, w_out, norm_ffn_g, router_group_w, router_group_b, router_expert_w,
        router_expert_b, ts)
    wg, wu, wd = expert_w_gate[l], expert_w_up[l], expert_w_down[l]
    out = None
    prev_dest = None
    for part in range(n_parts):
        b0 = jnp.full((1,), part * pb, jnp.int32)
        if prev_dest is not None:
            b0 = b0 + jnp.minimum(prev_dest[0, 0, :1], 0)
        x1, h2_rows, route_i, gates_col, counts = _mixer_call(x, params, b0, pb, **dims)

        cnt = counts[:, 0].astype(jnp.int32)
        padded = (cnt + EXPERT_BLOCK - 1) // EXPERT_BLOCK * EXPERT_BLOCK
        padded_ends = jnp.cumsum(padded)
        padded_starts = padded_ends - padded
        ecol = jnp.pad(jnp.stack([padded_ends // EXPERT_BLOCK, padded_starts + cnt], axis=1),
                       ((0, 0), (0, LANES - 2))).astype(jnp.float32)

        dest = _dest_call(padded_starts, route_i.reshape(pb * (seq // ts), SUBLANES, ts))
        prev_dest = dest
        xs = _sc_dispatch(dest, h2_rows.reshape(n_tok, nsub, LANES), n_slots)
        ys = _expert_call(ecol, xs.reshape(n_slots * nsub, LANES), wg, wu, wd, nsub)
        y_tok = _sc_gather(dest, ys.reshape(n_slots, nsub, LANES), n_tok)
        out = _combine_call(x1.reshape(n_tok, d), gates_col.reshape(n_tok, LANES),
                            final_norm_g[None, :], y_tok.reshape(TOP_K, n_tok * nsub, LANES), nsub,
                            out, part * n_tok, bsz * seq)
    return out.reshape(bsz, seq, d)
```

```python
import functools

import jax
import jax.numpy as jnp
from jax import lax
from jax.experimental import pallas as pl
from jax.experimental.pallas import tpu as pltpu
from jax.experimental.pallas import tpu_sc as plsc

EPS = 1e-6
LANES = 128
SUBLANES = 8
CHUNK = 128
TOP_K = 2
SEQ_TILE = 512
CONV_ROWS = 32
OUT_K_GROUP = 256
CONV_HALO = 32
QK_HALO = 8
EXPERT_BLOCK = 512
SC_CHUNK = 128
COMBINE_TILE = 1024
N_PARTS = 2
PACK_COLS = 2 * LANES
VMEM_LIMIT = 56 * 1024 * 1024

_NT = (((1,), (1,)), ((), ()))
_TN = (((0,), (0,)), ((), ()))
_HI16 = 0xFFFF0000


def _dot(a, b):
    return jnp.dot(a, b, preferred_element_type=jnp.float32)


def _silu(x):
    return x * jax.nn.sigmoid(x)


def _log_sigmoid(x):
    return jnp.minimum(x, 0.0) - jnp.log1p(jnp.exp(-jnp.abs(x)))


def _split_bf16(x, parts):
    out = []
    for _ in range(parts):
        p = x.astype(jnp.bfloat16)
        out.append(p)
        x = x - p.astype(jnp.float32)
    return out


def _store_packed_rows(ref, v):
    n, d = v.shape
    half = d // 2
    nsub = half // LANES
    as_bits = lambda a: pltpu.bitcast(a.astype(jnp.bfloat16).astype(jnp.float32), jnp.uint32)
    word = (as_bits(v[:, :half]) >> 16) | (as_bits(v[:, half:]) & jnp.uint32(_HI16))
    for s in range(nsub):
        ref[pl.ds(s, n, stride=nsub), :] = word[:, s * LANES:(s + 1) * LANES]


def _load_packed_rows(ref, n, nsub, live=None):
    lo, hi = [], []
    for s in range(nsub):
        w = ref[pl.ds(s, n, stride=nsub), :]
        if live is not None:
            w = jnp.where(live, w, jnp.uint32(0))
        lo.append(pltpu.bitcast(w << 16, jnp.float32))
        hi.append(pltpu.bitcast(w & jnp.uint32(_HI16), jnp.float32))
    return lo + hi


def _mixer_kernel(b0_ref, x_ref, gmix_ref, wcv_ref, bcv_ref, wqk_ref, bqk_ref, wvo_ref, bvo_ref,
                  wif_ref, bif_ref, wift_ref, bift_ref, cw_ref, cb_ref, lng_ref, lnb_ref,
                  qkw_ref, qkb_ref, mng_ref, wout_ref, gffn_ref, wrh_ref, wrl_ref, br_ref,
                  su_ref, tril_ref,
                  x1_ref, h2_ref, ri_ref, gc_ref, cnt_ref,
                  uext, qkext, q_s, k_s, v_s, o_s, cn_s, m_s, y_s, cnt_s,
                  *, n_heads, conv_width, qk_width, n_groups, epg):
    ts, d = x_ref.shape
    dc = cw_ref.shape[1]
    dm = mng_ref.shape[1]
    dh = dm // n_heads
    n_chunks = ts // CHUNK
    n_experts = n_groups * epg
    b_idx = pl.program_id(0)
    t_idx = pl.program_id(1)

    @pl.when(t_idx == 0)
    def _reset_sequence_state():
        uext[0:CONV_HALO, :] = jnp.zeros((CONV_HALO, dc), jnp.float32)
        uext[CONV_HALO + ts:, :] = jnp.zeros((SUBLANES, dc), jnp.float32)
        qkext[0:QK_HALO, :] = jnp.zeros((QK_HALO, 2 * dm), jnp.float32)
        cn_s[...] = jnp.zeros(cn_s.shape, jnp.float32)
        m_s[...] = jnp.zeros(m_s.shape, jnp.float32)

    @pl.when((t_idx == 0) & (b_idx == 0))
    def _reset_counts():
        cnt_s[...] = jnp.zeros(cnt_s.shape, jnp.float32)

    x = x_ref[...]
    h = x * lax.rsqrt(jnp.mean(x * x, axis=-1, keepdims=True) + EPS) * gmix_ref[...]
    hb = h.astype(jnp.bfloat16)

    zc = _dot(hb, wcv_ref[...]) + bcv_ref[...]
    uext[CONV_HALO:CONV_HALO + ts, :] = zc[:, :dc] * jax.nn.sigmoid(zc[:, dc:])
    first = CONV_HALO - (conv_width - 1)
    sub = lax.broadcasted_iota(jnp.int32, (SUBLANES, dc), 0)
    n_grp = CONV_ROWS // SUBLANES
    for rb in range(ts // CONV_ROWS):
        r0 = rb * CONV_ROWS
        acc = jnp.zeros((CONV_ROWS, dc), jnp.float32) + cb_ref[...]
        for r in range(SUBLANES):
            part = None
            for j in range(conv_width):
                if (first + j) % SUBLANES != r:
                    continue
                a0 = r0 + (first + j) - r
                w8 = cw_ref[j * SUBLANES:(j + 1) * SUBLANES, :]
                term = jnp.concatenate([w8] * (n_grp + 1), axis=0) * uext[a0:a0 + CONV_ROWS + SUBLANES, :]
                part = term if part is None else part + term
            if part is None:
                continue
            if r == 0:
                acc = acc + part[:CONV_ROWS, :]
                continue
            rolled = [pltpu.roll(part[g * SUBLANES:(g + 1) * SUBLANES, :], SUBLANES - r, axis=0)
                      for g in range(n_grp + 1)]
            acc = acc + jnp.concatenate(
                [jnp.where(sub < SUBLANES - r, rolled[g], rolled[g + 1]) for g in range(n_grp)], axis=0)
        mu = jnp.mean(acc, axis=-1, keepdims=True)
        cen = acc - mu
        var = jnp.mean(cen * cen, axis=-1, keepdims=True)
        yn = cen * lax.rsqrt(var + EPS) * lng_ref[...] + lnb_ref[...]
        y_s[r0:r0 + CONV_ROWS, 0:dc] = _silu(yn).astype(jnp.bfloat16)
    uext[0:CONV_HALO, :] = uext[ts:ts + CONV_HALO, :]

    qkext[QK_HALO:QK_HALO + ts, :] = _dot(hb, wqk_ref[...]) + bqk_ref[...]
    qfirst = QK_HALO - (qk_width - 1)
    k_scale = dh ** -0.5
    for rb in range(ts // CONV_ROWS):
        r0 = rb * CONV_ROWS
        acc = jnp.zeros((CONV_ROWS, 2 * dm), jnp.float32) + qkb_ref[...]
        for j in range(qk_width):
            w8 = qkw_ref[j * SUBLANES:(j + 1) * SUBLANES, :]
            acc = acc + (jnp.concatenate([w8] * (CONV_ROWS // SUBLANES), axis=0)
                         * qkext[r0 + qfirst + j:r0 + qfirst + j + CONV_ROWS, :])
        act = _silu(acc)
        q_s[r0:r0 + CONV_ROWS, :] = act[:, :dm].astype(jnp.bfloat16)
        k_s[r0:r0 + CONV_ROWS, :] = (act[:, dm:] * k_scale).astype(jnp.bfloat16)
    qkext[0:QK_HALO, :] = qkext[ts:ts + QK_HALO, :]

    zvo = _dot(hb, wvo_ref[...]) + bvo_ref[...]
    v_s[...] = zvo[:, :dm].astype(jnp.bfloat16)
    o_s[...] = jax.nn.sigmoid(zvo[:, dm:])

    zif_col = _dot(hb, wif_ref[...]) + bif_ref[...]
    zif_row = lax.dot_general(wift_ref[...], hb, _NT,
                              preferred_element_type=jnp.float32) + bift_ref[...]
    logf_col = _log_sigmoid(zif_col)
    logf_row = _log_sigmoid(zif_row)
    lane = lax.broadcasted_iota(jnp.int32, (SUBLANES, CHUNK), 1)
    causal = (lax.broadcasted_iota(jnp.int32, (CHUNK, CHUNK), 0)
              >= lax.broadcasted_iota(jnp.int32, (CHUNK, CHUNK), 1))
    ones_blk = jnp.ones((CHUNK, dh), jnp.bfloat16)
    tril = tril_ref[...]

    for c in range(n_chunks):
        r0 = c * CHUNK
        b_row = logf_row[:, r0:r0 + CHUNK]
        shift = 1
        while shift < CHUNK:
            b_row = b_row + jnp.where(lane >= shift, pltpu.roll(b_row, shift, axis=1), 0.0)
            shift *= 2
        b_col = sum(_dot(tril, p) for p in _split_bf16(logf_col[r0:r0 + CHUNK, :], 3))
        i_col_all = zif_col[r0:r0 + CHUNK, :]
        i_row_all = zif_row[:, r0:r0 + CHUNK]
        for hd in range(n_heads):
            c0 = hd * dh
            b_c = b_col[:, n_heads + hd:n_heads + hd + 1]
            i_c = i_col_all[:, hd:hd + 1]
            b_r = b_row[n_heads + hd:n_heads + hd + 1, :]
            i_r = i_row_all[hd:hd + 1, :]
            m_prev = m_s[hd, 0:1, 0:1]
            q = q_s[r0:r0 + CHUNK, c0:c0 + dh]
            k = k_s[r0:r0 + CHUNK, c0:c0 + dh]
            vaug = jnp.concatenate([v_s[r0:r0 + CHUNK, c0:c0 + dh], ones_blk], axis=-1)
            cn = cn_s[hd]

            dmat = jnp.where(causal, b_c + (i_r - b_r), -jnp.inf)
            g = b_c + m_prev
            m_t = jnp.maximum(g, jnp.max(dmat, axis=-1, keepdims=True))
            w_intra = jnp.exp(dmat - m_t)
            w_inter = jnp.exp(g - m_t)
            s = lax.dot_general(q, k, _NT, preferred_element_type=jnp.float32) * w_intra
            nd = _dot(s.astype(jnp.bfloat16), vaug) + w_inter * _dot(q, cn.astype(jnp.bfloat16))
            hval = nd[:, :dh] / jnp.maximum(jnp.abs(nd[:, dh:]), jnp.exp(-m_t))

            b_last = b_r[:, CHUNK - 1:CHUNK]
            logw_c = b_last - b_c + i_c
            m_new = jnp.maximum(b_last + m_prev, jnp.max(logw_c, axis=0, keepdims=True))
            decay = jnp.exp(b_last + m_prev - m_new)
            kw = (k.astype(jnp.float32) * jnp.exp(logw_c - m_new)).astype(jnp.bfloat16)
            cn_s[hd] = decay * cn + lax.dot_general(kw, vaug, _TN,
                                                    preferred_element_type=jnp.float32)
            m_s[hd] = jnp.broadcast_to(m_new, m_s.shape[1:])

            hm = o_s[r0:r0 + CHUNK, c0:c0 + dh] * hval
            mu = jnp.mean(hm, axis=-1, keepdims=True)
            cen = hm - mu
            var = jnp.mean(cen * cen, axis=-1, keepdims=True)
            hn = cen * lax.rsqrt(var + EPS) * mng_ref[:, c0:c0 + dh]
            y_s[r0:r0 + CHUNK, dc + c0:dc + c0 + dh] = hn.astype(jnp.bfloat16)

    hg = dc + OUT_K_GROUP
    x1 = x_ref[...] + (_dot(y_s[:, 0:dc], wout_ref[0:dc, :])
                       + (_dot(y_s[:, dc:hg], wout_ref[dc:hg, :]) + _dot(y_s[:, hg:d], wout_ref[hg:d, :])))
    x1_ref[...] = x1
    h2 = x1 * lax.rsqrt(jnp.mean(x1 * x1, axis=-1, keepdims=True) + EPS) * gffn_ref[...]
    _store_packed_rows(h2_ref, h2)

    h2_hi, h2_lo = _split_bf16(h2, 2)
    lg = (lax.dot_general(wrh_ref[...], h2_hi, _NT, preferred_element_type=jnp.float32)
          + lax.dot_general(wrl_ref[...], h2_hi, _NT, preferred_element_type=jnp.float32)
          + lax.dot_general(wrh_ref[...], h2_lo, _NT, preferred_element_type=jnp.float32)
          + br_ref[...])
    gl = lg[0:n_groups, :]
    gidx = lax.broadcasted_iota(jnp.int32, (n_groups, ts), 0)
    gmax = jnp.max(gl, axis=0, keepdims=True)
    grp = jnp.min(jnp.where(gl == gmax, gidx, n_groups), axis=0, keepdims=True)
    p_grp = 1.0 / jnp.sum(jnp.exp(gl - gmax), axis=0, keepdims=True)
    in_group = jnp.zeros((epg, ts), jnp.float32)
    for gi in range(n_groups):
        in_group = jnp.where(grp == gi, lg[SUBLANES + gi * epg:SUBLANES + (gi + 1) * epg, :], in_group)
    eidx = lax.broadcasted_iota(jnp.int32, (epg, ts), 0)
    v1 = jnp.max(in_group, axis=0, keepdims=True)
    i1 = jnp.min(jnp.where(in_group == v1, eidx, epg), axis=0, keepdims=True)
    rest = jnp.where(eidx == i1, -jnp.inf, in_group)
    v2 = jnp.max(rest, axis=0, keepdims=True)
    i2 = jnp.min(jnp.where(rest == v2, eidx, epg), axis=0, keepdims=True)
    e21 = jnp.exp(v2 - v1)
    g1 = p_grp / (1.0 + e21)
    g2 = g1 * e21
    e1 = grp * epg + i1
    e2 = grp * epg + i2

    xidx = lax.broadcasted_iota(jnp.int32, (n_experts, ts), 0)
    sel1 = xidx == e1
    sel2 = xidx == e2
    onehot = jnp.where(sel1 | sel2, 1.0, 0.0).astype(jnp.bfloat16)
    before = cnt_s[:, 0:1] + _dot(onehot, su_ref[...])
    rank1 = jnp.sum(jnp.where(sel1, before, 0.0), axis=0, keepdims=True)
    rank2 = jnp.sum(jnp.where(sel2, before, 0.0), axis=0, keepdims=True)
    cnt_new = cnt_s[...] + jnp.sum(onehot.astype(jnp.float32), axis=1, keepdims=True)
    cnt_s[...] = cnt_new
    cnt_ref[...] = cnt_new

    zero_i = jnp.zeros((SUBLANES - 4, ts), jnp.int32)
    ri_ref[...] = jnp.concatenate(
        [e1, e2, rank1.astype(jnp.int32), rank2.astype(jnp.int32), zero_i], axis=0)
    gates_row = jnp.concatenate([g1, g2, jnp.zeros((LANES - 2, ts), jnp.float32)], axis=0)
    gc_ref[...] = gates_row.T


def _mixer_call(x, p, b0, bsz, *, n_heads, conv_width, qk_width, n_groups, epg):
    _, seq, d = x.shape
    ts = min(SEQ_TILE, seq)
    nt = seq // ts
    dc = p["cw"].shape[1]
    dm = p["mng"].shape[1]
    dh = dm // n_heads
    n_experts = n_groups * epg
    assert seq % ts == 0 and ts % CHUNK == 0 and dh == LANES and d % PACK_COLS == 0
    assert epg == SUBLANES and n_groups <= SUBLANES and conv_width - 1 <= CONV_HALO
    assert d - dc == 2 * OUT_K_GROUP

    consts = [p[n] for n in ("gmix", "wcv", "bcv", "wqk", "bqk", "wvo", "bvo", "wif", "bif",
                             "wift", "bift", "cw", "cb", "lng", "lnb", "qkw", "qkb", "mng",
                             "wout", "gffn", "wrh", "wrl", "br", "su", "tril")]

    def const_spec(a):
        return pl.BlockSpec(a.shape, lambda b, t, b0r: (0,) * a.ndim)

    kern = functools.partial(_mixer_kernel, n_heads=n_heads, conv_width=conv_width,
                             qk_width=qk_width, n_groups=n_groups, epg=epg)
    grid_spec = pltpu.PrefetchScalarGridSpec(
        num_scalar_prefetch=1,
        grid=(bsz, nt),
        in_specs=([pl.BlockSpec((None, ts, d), lambda b, t, b0r: (b + b0r[0], t, 0))]
                  + [const_spec(a) for a in consts]),
        out_specs=[
            pl.BlockSpec((None, ts, d), lambda b, t, b0r: (b, t, 0)),
            pl.BlockSpec((None, ts * (d // PACK_COLS), LANES), lambda b, t, b0r: (b, t, 0)),
            pl.BlockSpec((None, None, SUBLANES, ts), lambda b, t, b0r: (b, t, 0, 0)),
            pl.BlockSpec((None, ts, LANES), lambda b, t, b0r: (b, t, 0)),
            pl.BlockSpec((n_experts, LANES), lambda b, t, b0r: (0, 0)),
        ],
        scratch_shapes=[
            pltpu.VMEM((CONV_HALO + ts + SUBLANES, dc), jnp.float32),
            pltpu.VMEM((QK_HALO + ts, 2 * dm), jnp.float32),
            pltpu.VMEM((ts, dm), jnp.bfloat16),
            pltpu.VMEM((ts, dm), jnp.bfloat16),
            pltpu.VMEM((ts, dm), jnp.bfloat16),
            pltpu.VMEM((ts, dm), jnp.float32),
            pltpu.VMEM((n_heads, dh, 2 * dh), jnp.float32),
            pltpu.VMEM((n_heads, SUBLANES, LANES), jnp.float32),
            pltpu.VMEM((ts, d), jnp.bfloat16),
            pltpu.VMEM((n_experts, LANES), jnp.float32),
        ],
    )
    return pl.pallas_call(
        kern,
        grid_spec=grid_spec,
        out_shape=[
            jax.ShapeDtypeStruct((bsz, seq, d), jnp.float32),
            jax.ShapeDtypeStruct((bsz, seq * (d // PACK_COLS), LANES), jnp.uint32),
            jax.ShapeDtypeStruct((bsz, nt, SUBLANES, ts), jnp.int32),
            jax.ShapeDtypeStruct((bsz, seq, LANES), jnp.float32),
            jax.ShapeDtypeStruct((n_experts, LANES), jnp.float32),
        ],
        compiler_params=pltpu.CompilerParams(
            dimension_semantics=("arbitrary", "arbitrary"), vmem_limit_bytes=VMEM_LIMIT),
        name="mixer_router",
    )(b0, x, *consts)


def _sc_workers():
    info = plsc.get_sparse_core_info()
    return info.num_cores, info.num_cores * info.num_subcores


def _dest_kernel(pstart_ref, route_ref, dest_ref, *, n_experts):
    tiles, _, ts = route_ref.shape
    per = ts // SC_CHUNK
    for k in range(TOP_K):
        e = route_ref[:, k, :]
        rank = route_ref[:, TOP_K + k, :]
        start = jnp.zeros((tiles, ts), jnp.int32)
        for ex in range(n_experts):
            start = jnp.where(e == ex, pstart_ref[ex], start)
        slot = start + rank
        for j in range(per):
            dest_ref[k, pl.ds(j, tiles, stride=per), :] = slot[:, j * SC_CHUNK:(j + 1) * SC_CHUNK]


def _dest_call(padded_starts, route):
    tiles, _, ts = route.shape
    assert ts % SC_CHUNK == 0 and SC_CHUNK == LANES
    n_rows = tiles * ts // SC_CHUNK
    return pl.pallas_call(
        functools.partial(_dest_kernel, n_experts=padded_starts.shape[0]),
        grid_spec=pltpu.PrefetchScalarGridSpec(
            num_scalar_prefetch=1,
            grid=(1,),
            in_specs=[pl.BlockSpec(route.shape, lambda i, ps: (0, 0, 0))],
            out_specs=pl.BlockSpec((TOP_K, n_rows, SC_CHUNK), lambda i, ps: (0, 0, 0)),
        ),
        out_shape=jax.ShapeDtypeStruct((TOP_K, n_rows, SC_CHUNK), jnp.int32),
        compiler_params=pltpu.CompilerParams(dimension_semantics=("arbitrary",)),
        name="slot_ids",
    )(padded_starts, route)


def _sc_dispatch(dest, rows, n_slots):
    n_tok, nsub, l = rows.shape
    top_k, n_rows, ch = dest.shape
    n_cores, n_workers = _sc_workers()
    assert n_rows * ch == n_tok and n_rows % n_workers == 0
    n_chunks = n_rows // n_workers
    tpw = n_chunks * ch

    @functools.partial(
        pl.kernel,
        mesh=plsc.VectorSubcoreMesh(core_axis_name="c", subcore_axis_name="s"),
        out_type=jax.ShapeDtypeStruct((n_slots, nsub, l), rows.dtype),
        scratch_types=[pltpu.VMEM((top_k, n_chunks, ch), jnp.int32),
                       pltpu.VMEM((ch, nsub, l), rows.dtype),
                       pltpu.SemaphoreType.DMA],
        name="sc_dispatch",
    )
    def run(dest_hbm, rows_hbm, out_hbm, idx_v, buf, sem):
        wid = lax.axis_index("s") * n_cores + lax.axis_index("c")
        for k in range(top_k):
            pltpu.sync_copy(dest_hbm.at[k, pl.ds(wid * n_chunks, n_chunks)], idx_v.at[k])

        @pl.loop(0, n_chunks)
        def _(j):
            pltpu.sync_copy(rows_hbm.at[pl.ds(wid * tpw + j * ch, ch)], buf)
            for k in range(top_k):
                pltpu.async_copy(buf, out_hbm.at[idx_v.at[k, j]], sem).wait()

    return run(dest, rows)


def _sc_gather(dest, ys, n_tok):
    _, nsub, l = ys.shape
    top_k, n_rows, ch = dest.shape
    n_cores, n_workers = _sc_workers()
    assert n_rows * ch == n_tok and n_rows % n_workers == 0
    n_chunks = n_rows // n_workers
    tpw = n_chunks * ch

    @functools.partial(
        pl.kernel,
        mesh=plsc.VectorSubcoreMesh(core_axis_name="c", subcore_axis_name="s"),
        out_type=jax.ShapeDtypeStruct((top_k, n_tok, nsub, l), ys.dtype),
        scratch_types=[pltpu.VMEM((top_k, n_chunks, ch), jnp.int32),
                       pltpu.VMEM((ch, nsub, l), ys.dtype),
                       pltpu.SemaphoreType.DMA],
        name="sc_gather",
    )
    def run(dest_hbm, ys_hbm, out_hbm, idx_v, buf, sem):
        wid = lax.axis_index("s") * n_cores + lax.axis_index("c")
        for k in range(top_k):
            pltpu.sync_copy(dest_hbm.at[k, pl.ds(wid * n_chunks, n_chunks)], idx_v.at[k])

        @pl.loop(0, n_chunks)
        def _(j):
            for k in range(top_k):
                pltpu.async_copy(ys_hbm.at[idx_v.at[k, j]], buf, sem).wait()
                pltpu.sync_copy(buf, out_hbm.at[k, pl.ds(wid * tpw + j * ch, ch)])

    return run(dest, ys)


_T_EXPERT, _T_VALID, _T_NEXT, _T_RUN, _T_USED = range(5)


def _expert_kernel(ecol_ref, xs_ref, wg_hbm, wu_hbm, wd_hbm, ys_ref,
                   wg_f, wu_f, wd_f, wg_b, wu_b, wd_b, tbl_v, tbl_s, sems, tsem, *, nsub, n_blocks):
    i = pl.program_id(0)
    blk = xs_ref.shape[0] // nsub
    l = xs_ref.shape[1]
    n_experts = ecol_ref.shape[0]

    @pl.when(i == 0)
    def _build_block_table():
        width = tbl_v.shape[1]
        blk_end = ecol_ref[:, 0:1]
        slot_end = ecol_ref[:, 1:2]
        b = lax.broadcasted_iota(jnp.int32, (n_experts, width), 1).astype(jnp.float32)
        e_id = lax.broadcasted_iota(jnp.int32, (n_experts, width), 0).astype(jnp.float32)
        be = jnp.minimum(jnp.sum(jnp.where(b >= blk_end, 1.0, 0.0), axis=0, keepdims=True),
                         n_experts - 1.0)
        mine = e_id == be
        valid = jnp.clip(jnp.sum(jnp.where(mine, slot_end, 0.0), axis=0, keepdims=True)
                         - b[0:1, :] * blk, 0.0, float(blk))
        present = jnp.max(jnp.where(mine & (b < n_blocks), 1.0, 0.0), axis=1, keepdims=True) > 0.0
        run = jnp.sum(jnp.where((e_id < be) & present, 1.0, 0.0), axis=0, keepdims=True)
        nxt = jnp.min(jnp.where((e_id > be) & present, e_id, float(n_experts)), axis=0, keepdims=True)
        nxt = jnp.where(nxt == n_experts, -1.0, nxt)
        used = jnp.broadcast_to(jnp.max(blk_end, axis=0, keepdims=True), (1, width))
        rows = [be, valid, nxt, run, used] + [jnp.zeros((1, width), jnp.float32)] * (SUBLANES - 5)
        tbl_v[...] = jnp.concatenate(rows, axis=0).astype(jnp.int32)
        to_smem = pltpu.make_async_copy(tbl_v, tbl_s, tsem)
        to_smem.start()
        to_smem.wait()

    expert = tbl_s[_T_EXPERT, i]
    slot = tbl_s[_T_RUN, i] % 2

    def weight_copies(e, s):
        return [pltpu.make_async_copy(src.at[e], dst.at[s], sems.at[s])
                for src, dst in ((wg_hbm, wg_f), (wu_hbm, wu_f), (wd_hbm, wd_f))]

    @pl.when(i == 0)
    def _fetch_first_weights():
        for c in weight_copies(expert, 0):
            c.start()

    @pl.when((i == 0) | (expert != tbl_s[_T_EXPERT, jnp.maximum(i - 1, 0)]))
    def _new_expert_weights():
        for c in weight_copies(expert, slot):
            c.wait()
        wg_b[...] = wg_f[slot].astype(jnp.bfloat16)
        wu_b[...] = wu_f[slot].astype(jnp.bfloat16)
        wd_b[...] = wd_f[slot].astype(jnp.bfloat16)

        @pl.when(tbl_s[_T_NEXT, i] >= 0)
        def _fetch_next_run_weights():
            for c in weight_copies(tbl_s[_T_NEXT, i], 1 - slot):
                c.start(priority=1)

    @pl.when(i < tbl_s[_T_USED, i])
    def _compute():
        live = lax.broadcasted_iota(jnp.int32, (blk, l), 0) < tbl_s[_T_VALID, i]
        xb = jnp.concatenate(
            [c.astype(jnp.bfloat16) for c in _load_packed_rows(xs_ref, blk, nsub, live)], axis=-1)
        a = _dot(xb, wg_b[...])
        u = _dot(xb, wu_b[...])
        _store_packed_rows(ys_ref, _dot((_silu(a) * u).astype(jnp.bfloat16), wd_b[...]))

    @pl.when(i >= tbl_s[_T_USED, i])
    def _unused_block():
        ys_ref[...] = jnp.zeros(ys_ref.shape, ys_ref.dtype)


def _expert_call(ecol, xs, wg, wu, wd, nsub):
    rows, l = xs.shape
    n_blocks = rows // (EXPERT_BLOCK * nsub)
    _, d, de = wg.shape
    width = -(-n_blocks // LANES) * LANES
    hbm = pl.BlockSpec(memory_space=pl.ANY)
    return pl.pallas_call(
        functools.partial(_expert_kernel, nsub=nsub, n_blocks=n_blocks),
        grid=(n_blocks,),
        in_specs=[pl.BlockSpec(ecol.shape, lambda i: (0, 0)),
                  pl.BlockSpec((EXPERT_BLOCK * nsub, l), lambda i: (i, 0)), hbm, hbm, hbm],
        out_specs=pl.BlockSpec((EXPERT_BLOCK * nsub, l), lambda i: (i, 0)),
        out_shape=jax.ShapeDtypeStruct((rows, l), xs.dtype),
        scratch_shapes=[pltpu.VMEM((2, d, de), wg.dtype), pltpu.VMEM((2, d, de), wu.dtype),
                        pltpu.VMEM((2, de, d), wd.dtype),
                        pltpu.VMEM((d, de), jnp.bfloat16), pltpu.VMEM((d, de), jnp.bfloat16),
                        pltpu.VMEM((de, d), jnp.bfloat16),
                        pltpu.VMEM((SUBLANES, width), jnp.int32),
                        pltpu.SMEM((SUBLANES, width), jnp.int32),
                        pltpu.SemaphoreType.DMA((2,)), pltpu.SemaphoreType.DMA(())],
        compiler_params=pltpu.CompilerParams(
            dimension_semantics=("arbitrary",), vmem_limit_bytes=VMEM_LIMIT),
        name="experts",
    )(ecol, xs, wg, wu, wd)


def _combine_kernel(x1_ref, gc_ref, gfin_ref, y_ref, out_ref, *, nsub):
    tc, d = x1_ref.shape
    g1 = gc_ref[:, 0:1]
    g2 = gc_ref[:, 1:2]
    pieces = []
    ssq = jnp.zeros((tc, 1), jnp.float32)
    y1_cols = _load_packed_rows(y_ref.at[0], tc, nsub)
    y2_cols = _load_packed_rows(y_ref.at[1], tc, nsub)
    for j, (y1, y2) in enumerate(zip(y1_cols, y2_cols)):
        z = x1_ref[:, j * LANES:(j + 1) * LANES] + (g1 * y1 + g2 * y2)
        pieces.append(z)
        ssq = ssq + jnp.sum(z * z, axis=-1, keepdims=True)
    scale = lax.rsqrt(ssq / d + EPS)
    for j, z in enumerate(pieces):
        out_ref[:, j * LANES:(j + 1) * LANES] = z * scale * gfin_ref[:, j * LANES:(j + 1) * LANES]


def _combine_call(x1, gates_col, gfin, y_tok, nsub, out_prev, tok0, n_tok_total):
    n_tok, d = x1.shape
    tc = min(COMBINE_TILE, n_tok)
    assert n_tok % tc == 0 and tok0 % tc == 0
    l = y_tok.shape[-1]
    in_specs = [
        pl.BlockSpec((tc, d), lambda i: (i, 0)),
        pl.BlockSpec((tc, LANES), lambda i: (i, 0)),
        pl.BlockSpec((1, d), lambda i: (0, 0)),
        pl.BlockSpec((TOP_K, tc * nsub, l), lambda i: (0, i, 0)),
    ]
    args = [x1, gates_col, gfin, y_tok]
    aliases = {}
    kern = functools.partial(_combine_kernel, nsub=nsub)
    if out_prev is not None:
        in_specs.append(pl.BlockSpec(memory_space=pl.ANY))
        args.append(out_prev)
        aliases = {len(args) - 1: 0}
        kern = lambda x1_ref, gc_ref, gfin_ref, y_ref, prev_ref, out_ref: _combine_kernel(
            x1_ref, gc_ref, gfin_ref, y_ref, out_ref, nsub=nsub)
    return pl.pallas_call(
        kern,
        grid=(n_tok // tc,),
        in_specs=in_specs,
        out_specs=pl.BlockSpec((tc, d), lambda i: (i + tok0 // tc, 0)),
        out_shape=jax.ShapeDtypeStruct((n_tok_total, d), jnp.float32),
        input_output_aliases=aliases,
        compiler_params=pltpu.CompilerParams(
            dimension_semantics=("arbitrary",), vmem_limit_bytes=VMEM_LIMIT),
        name="combine",
    )(*args)


def _prepare_layer(l, norm_mix_g, w_in, b_in, conv_w, conv_b, conv_ln_g, conv_ln_b, qk_conv_w,
                   qk_conv_b, mlstm_norm_g, w_out, norm_ffn_g, router_group_w, router_group_b,
                   router_expert_w, router_expert_b, ts):
    f32, bf16 = jnp.float32, jnp.bfloat16
    d = w_in.shape[1]
    dc = conv_w.shape[-1]
    dm = mlstm_norm_g.shape[-1]
    n_heads = (w_in.shape[-1] - 2 * dc - 4 * dm) // 2
    n_groups = router_group_w.shape[-1]
    n_experts = router_expert_w.shape[-1]
    wi, bi = w_in[l], b_in[l]
    o_qk, o_vo, o_if = 2 * dc, 2 * dc + 2 * dm, 2 * dc + 4 * dm
    w_if = wi[:, o_if:]
    b_if = bi[o_if:]
    wr_t = jnp.zeros((SUBLANES + n_experts, d), f32)
    wr_t = wr_t.at[:n_groups].set(router_group_w[l].T).at[SUBLANES:].set(router_expert_w[l].T)
    br = jnp.zeros((SUBLANES + n_experts, 1), f32)
    br = br.at[:n_groups, 0].set(router_group_b[l]).at[SUBLANES:, 0].set(router_expert_b[l])
    wr_hi = wr_t.astype(bf16)
    wr_lo = (wr_t - wr_hi.astype(f32)).astype(bf16)
    pad_rows = lambda a, n: jnp.pad(a, ((0, n - a.shape[0]), (0, 0)))
    idx = jnp.arange(ts)
    cidx = jnp.arange(CHUNK)
    return dict(
        gmix=norm_mix_g[l][None, :],
        wcv=wi[:, :o_qk].astype(bf16), bcv=bi[None, :o_qk],
        wqk=wi[:, o_qk:o_vo].astype(bf16), bqk=bi[None, o_qk:o_vo],
        wvo=wi[:, o_vo:o_if].astype(bf16), bvo=bi[None, o_vo:o_if],
        wif=jnp.pad(w_if, ((0, 0), (0, LANES - 2 * n_heads))).astype(bf16),
        bif=jnp.pad(b_if, (0, LANES - 2 * n_heads))[None, :],
        wift=pad_rows(w_if.T, SUBLANES).astype(bf16),
        bift=pad_rows(b_if[:, None], SUBLANES),
        cw=jnp.repeat(conv_w[l], SUBLANES, axis=0), cb=conv_b[l][None, :],
        lng=conv_ln_g[l][None, :], lnb=conv_ln_b[l][None, :],
        qkw=jnp.repeat(qk_conv_w[l], SUBLANES, axis=0), qkb=qk_conv_b[l][None, :],
        mng=mlstm_norm_g[l][None, :],
        wout=w_out[l].astype(bf16), gffn=norm_ffn_g[l][None, :],
        wrh=wr_hi, wrl=wr_lo, br=br,
        su=(idx[:, None] < idx[None, :]).astype(bf16),
        tril=(cidx[:, None] >= cidx[None, :]).astype(bf16),
    ), dict(n_heads=n_heads, conv_width=conv_w.shape[1], qk_width=qk_conv_w.shape[1],
            n_groups=n_groups, epg=n_experts // n_groups)


def kernel(x, norm_mix_g, w_in, b_in, conv_w, conv_b, conv_ln_g, conv_ln_b, qk_conv_w, qk_conv_b,
           mlstm_norm_g, w_out, norm_ffn_g, router_group_w, router_group_b, router_expert_w,
           router_expert_b, expert_w_gate, expert_w_up, expert_w_down, final_norm_g):
    bsz, seq, d = x.shape
    depth = w_in.shape[0]
    assert depth == 1, "the combine kernel fuses the final norm, so exactly one layer is supported"
    n_experts = router_expert_w.shape[-1]
    n_parts = N_PARTS if bsz % N_PARTS == 0 else 1
    pb = bsz // n_parts
    n_tok = pb * seq
    n_assign = n_tok * TOP_K
    n_blocks = -(-n_assign // EXPERT_BLOCK) + n_experts
    n_slots = n_blocks * EXPERT_BLOCK
    ts = min(SEQ_TILE, seq)
    nsub = d // PACK_COLS
    l = 0
    params, dims = _prepare_layer(
        l, norm_mix_g, w_in, b_in, conv_w, conv_b, conv_ln_g, conv_ln_b, qk_conv_w, qk_conv_b,
        mlstm_norm_g, w_out, norm_ffn_g, router_group_w, router_group_b, router_expert_w,
        router_expert_b, ts)
    wg, wu, wd = expert_w_gate[l], expert_w_up[l], expert_w_down[l]
    out = None
    prev_dest = None
    for part in range(n_parts):
        b0 = jnp.full((1,), part * pb, jnp.int32)
        if prev_dest is not None:
            b0 = b0 + jnp.minimum(prev_dest[0, 0, :1], 0)
        x1, h2_rows, route_i, gates_col, counts = _mixer_call(x, params, b0, pb, **dims)

        cnt = counts[:, 0].astype(jnp.int32)
        padded = (cnt + EXPERT_BLOCK - 1) // EXPERT_BLOCK * EXPERT_BLOCK
        padded_ends = jnp.cumsum(padded)
        padded_starts = padded_ends - padded
        ecol = jnp.pad(jnp.stack([padded_ends // EXPERT_BLOCK, padded_starts + cnt], axis=1),
                       ((0, 0), (0, LANES - 2))).astype(jnp.float32)

        dest = _dest_call(padded_starts, route_i.reshape(pb * (seq // ts), SUBLANES, ts))
        prev_dest = dest
        xs = _sc_dispatch(dest, h2_rows.reshape(n_tok, nsub, LANES), n_slots)
        ys = _expert_call(ecol, xs.reshape(n_slots * nsub, LANES), wg, wu, wd, nsub)
        y_tok = _sc_gather(dest, ys.reshape(n_slots, nsub, LANES), n_tok)
        out = _combine_call(x1.reshape(n_tok, d), gates_col.reshape(n_tok, LANES),
                            final_norm_g[None, :], y_tok.reshape(TOP_K, n_tok * nsub, LANES), nsub,
                            out, part * n_tok, bsz * seq)
    return out.reshape(bsz, seq, d)
```
